```python
import math
import jax, jax.numpy as jnp
from jax import lax
import numpy as np

D_MODEL = 1024
BATCH = 8
SEQ = 2048
DEPTH = 2
DEC_BATCH = 128
DEC_SEQ = 1
PAST_LEN = 16384
PAGE_SIZE = 128

H_RET = 4
RET_DK = 128
RET_DV = 128
RET_W = H_RET * RET_DV
RET_CHUNK = 128
GM_GROUPS = 4
GM_CH = 128
GM_W = GM_GROUPS * GM_CH
GMLP_CHUNK = 128
MIX_W = RET_W + GM_W
IN_W = 4 * RET_W + 2 * GM_W
D_FF = 2048
CONV_W = 3
ROPE_BASE = 10000.0
EPS = 1e-6

kernel_name = "hybrid_retention_gmlp_convffn_step"


def rmsnorm(x, g):
    xf = x.astype(jnp.float32)
    r = lax.rsqrt(jnp.mean(xf * xf, axis=-1, keepdims=True) + EPS)
    return (xf * r).astype(x.dtype) * g


def modulate(h, shift, scale):
    return h * (1.0 + scale[:, None, :]) + shift[:, None, :]


def rope(x, pos):
    half = x.shape[-1] // 2
    freqs = jnp.exp(-math.log(ROPE_BASE) * jnp.arange(half, dtype=jnp.float32) / half)
    ang = pos.astype(jnp.float32)[:, None] * freqs[None, :]
    cos = jnp.cos(ang)[None, :, None, :]
    sin = jnp.sin(ang)[None, :, None, :]
    xf = x.astype(jnp.float32)
    x1, x2 = xf[..., :half], xf[..., half:]
    return jnp.concatenate([x1 * cos - x2 * sin, x1 * sin + x2 * cos], axis=-1)


def retention(q, k, v, s0):
    B, L, H, _ = q.shape
    C = RET_CHUNK if L % RET_CHUNK == 0 else L
    n = L // C
    lg = jnp.log1p(-jnp.exp2(-5.0 - jnp.arange(H, dtype=jnp.float32)))
    i = jnp.arange(C, dtype=jnp.float32)
    diff = i[:, None] - i[None, :]
    dmask = jnp.where(diff[None] >= 0.0,
                      jnp.exp(jnp.maximum(diff, 0.0)[None] * lg[:, None, None]), 0.0)
    q_dec = jnp.exp((i[:, None] + 1.0) * lg[None, :])
    k_dec = jnp.exp((C - 1.0 - i)[:, None] * lg[None, :])
    chunk_dec = jnp.exp(C * lg)

    def to_chunks(t):
        return t.reshape(B, n, C, H, t.shape[-1]).swapaxes(0, 1)

    def step(S, inp):
        qc, kc, vc = inp
        att = jnp.einsum('bihd,bjhd->bhij', qc, kc) * dmask[None]
        o = (jnp.einsum('bhij,bjhe->bihe', att, vc)
             + jnp.einsum('bihd,bhde->bihe', qc, S) * q_dec[None, :, :, None])
        S = (chunk_dec[None, :, None, None] * S
             + jnp.einsum('bjhd,bjhe->bhde', kc * k_dec[None, :, :, None], vc))
        return S, o

    S, o = lax.scan(step, s0, (to_chunks(q), to_chunks(k), to_chunks(v)))
    return o.swapaxes(0, 1).reshape(B, L, H, -1), S


def chunk_gmlp(u, v, w_s, b_s, ln_g):
    B, L, _ = u.shape
    vg = v.reshape(B, L, GM_GROUPS, GM_CH).astype(jnp.float32)
    mu = jnp.mean(vg, axis=-1, keepdims=True)
    var = jnp.mean(jnp.square(vg - mu), axis=-1, keepdims=True)
    vn = (vg - mu) * lax.rsqrt(var + EPS) * ln_g.astype(jnp.float32).reshape(GM_GROUPS, GM_CH)
    C = GMLP_CHUNK if L % GMLP_CHUNK == 0 else L
    n = L // C
    mask = jnp.tril(jnp.ones((C, C), dtype=bool))
    W = jnp.where(mask[None], w_s[:, :C, :C].astype(jnp.float32), 0.0)
    s = (jnp.einsum('gij,bnjgc->bnigc', W, vn.reshape(B, n, C, GM_GROUPS, GM_CH))
         + b_s[:, :C].astype(jnp.float32).T[None, None, :, :, None])
    out = u.astype(jnp.float32) * s.reshape(B, L, GM_W)
    return out.astype(u.dtype), vn.reshape(B, L, GM_W).astype(u.dtype)


def causal_dwconv(a, buf, w, b):
    L = a.shape[1]
    full = jnp.concatenate([buf.astype(a.dtype), a], axis=1)
    out = b + sum(full[:, t:t + L, :] * w[t] for t in range(CONV_W))
    return out, full[:, -(CONV_W - 1):, :]


def layer(x, c, pos, s_ret, conv_buf, w_ada, b_ada, g_mix, w_in, ret_gn_gain, gmlp_ln_gain,
          w_s, b_s, w_out, g_ffn, w_up, conv_w, conv_b, w_down):
    B, L, _ = x.shape
    mod = jax.nn.silu(c) @ w_ada + b_ada
    sh1, sc1, gt1, sh2, sc2, gt2 = jnp.split(mod, 6, axis=-1)

    h = modulate(rmsnorm(x, g_mix), sh1, sc1)
    p = h @ w_in
    q, k, v, g, u, vv = jnp.split(p, [RET_W, 2 * RET_W, 3 * RET_W, 4 * RET_W, 4 * RET_W + GM_W], axis=-1)
    q = rope(q.reshape(B, L, H_RET, RET_DK), pos)
    k = rope(k.reshape(B, L, H_RET, RET_DK), pos) * (RET_DK ** -0.5)
    v = v.reshape(B, L, H_RET, RET_DV).astype(jnp.float32)
    o, s_new = retention(q, k, v, s_ret.astype(jnp.float32))
    mu = jnp.mean(o, axis=-1, keepdims=True)
    var = jnp.mean(jnp.square(o - mu), axis=-1, keepdims=True)
    o = (o - mu) * lax.rsqrt(var + EPS) * ret_gn_gain.astype(jnp.float32).reshape(H_RET, RET_DV)
    o_ret = (jax.nn.silu(g.astype(jnp.float32)) * o.reshape(B, L, RET_W)).astype(x.dtype)
    o_gm, v_rows = chunk_gmlp(jax.nn.gelu(u), jax.nn.gelu(vv), w_s, b_s, gmlp_ln_gain)
    mix = jnp.concatenate([o_ret, o_gm], axis=-1) @ w_out
    x = x + gt1[:, None, :] * mix

    h2 = modulate(rmsnorm(x, g_ffn), sh2, sc2)
    a, buf_new = causal_dwconv(h2 @ w_up, conv_buf, conv_w, conv_b)
    f = jax.nn.silu(a[..., :D_FF]) * a[..., D_FF:]
    x = x + gt2[:, None, :] * (f @ w_down)
    return x, s_new.astype(s_ret.dtype), buf_new, v_rows


def setup_inputs(seed: int = 0) -> dict:
    key = jax.random.key(seed)
    ks = jax.random.split(key, 24)

    def nrm(k, shape, s):
        return jax.random.normal(k, shape, jnp.float32) * s

    D = D_MODEL
    return {
        "x_prompt": nrm(ks[0], (BATCH, SEQ, D), 1.0),
        "x_sample": nrm(ks[1], (DEC_BATCH, DEC_SEQ, D), 1.0),
        "state_ret": nrm(ks[2], (DEPTH, DEC_BATCH, H_RET, RET_DK, RET_DV), 0.5),
        "state_conv": nrm(ks[3], (DEPTH, DEC_BATCH, CONV_W - 1, 2 * D_FF), 1.0),
        "c_prompt": nrm(ks[4], (BATCH, D), 1.0),
        "c_sample": nrm(ks[5], (DEC_BATCH, D), 1.0),
        "w_ada": nrm(ks[6], (DEPTH, D, 6 * D), D ** -0.5),
        "b_ada": nrm(ks[7], (DEPTH, 6 * D), 0.01),
        "g_mix": 1.0 + nrm(ks[8], (DEPTH, D), 0.01),
        "w_in": nrm(ks[9], (DEPTH, D, IN_W), D ** -0.5),
        "ret_gn_gain": 1.0 + nrm(ks[10], (DEPTH, RET_W), 0.01),
        "gmlp_ln_gain": 1.0 + nrm(ks[11], (DEPTH, GM_W), 0.01),
        "w_s": nrm(ks[12], (DEPTH, GM_GROUPS, GMLP_CHUNK, GMLP_CHUNK), GMLP_CHUNK ** -0.5),
        "b_s": 1.0 + nrm(ks[13], (DEPTH, GM_GROUPS, GMLP_CHUNK), 0.1),
        "w_out": nrm(ks[14], (DEPTH, MIX_W, D), MIX_W ** -0.5),
        "g_ffn": 1.0 + nrm(ks[15], (DEPTH, D), 0.01),
        "w_up": nrm(ks[16], (DEPTH, D, 2 * D_FF), D ** -0.5),
        "conv_w": nrm(ks[17], (DEPTH, CONV_W, 2 * D_FF), CONV_W ** -0.5),
        "conv_b": nrm(ks[18], (DEPTH, 2 * D_FF), 0.01),
        "w_down": nrm(ks[19], (DEPTH, D_FF, D), D_FF ** -0.5),
        "g_final": 1.0 + nrm(ks[20], (D,), 0.01),
    }


def reference(x_prompt, x_sample, state_ret, state_conv, c_prompt, c_sample, w_ada, b_ada, g_mix,
              w_in, ret_gn_gain, gmlp_ln_gain, w_s, b_s, w_out, g_ffn, w_up, conv_w, conv_b,
              w_down, g_final):
    pos_p = jnp.arange(SEQ, dtype=jnp.int32)
    pos_s = PAST_LEN + jnp.arange(DEC_SEQ, dtype=jnp.int32)
    nb = x_prompt.shape[0]
    xp, xs = x_prompt, x_sample
    ret_p, conv_p, ret_s, conv_s, v_s = [], [], [], [], []
    for l in range(DEPTH):
        params = (w_ada[l], b_ada[l], g_mix[l], w_in[l], ret_gn_gain[l], gmlp_ln_gain[l], w_s[l],
                  b_s[l], w_out[l], g_ffn[l], w_up[l], conv_w[l], conv_b[l], w_down[l])
        s0 = jnp.zeros((nb, H_RET, RET_DK, RET_DV), jnp.float32)
        b0 = jnp.zeros((nb, CONV_W - 1, 2 * D_FF), xp.dtype)
        xp, sp, bp, _ = layer(xp, c_prompt, pos_p, s0, b0, *params)
        xs, ss, bs, vs = layer(xs, c_sample, pos_s, state_ret[l], state_conv[l], *params)
        ret_p.append(sp); conv_p.append(bp); ret_s.append(ss); conv_s.append(bs); v_s.append(vs)
    y_prompt = rmsnorm(xp, g_final)
    y_sample = rmsnorm(xs, g_final)
    new_ret_prompt = jnp.stack(ret_p)
    new_conv_prompt = jnp.stack(conv_p)
    new_ret_sample = jnp.stack(ret_s)
    new_conv_sample = jnp.stack(conv_s)
    new_gmlp_v_sample = jnp.stack(v_s)
    return (y_prompt, y_sample, new_ret_prompt, new_conv_prompt, new_ret_sample, new_conv_sample, new_gmlp_v_sample)
```

```python
import functools
import math

import jax
import jax.numpy as jnp
from jax import lax
from jax.experimental import pallas as pl
from jax.experimental.pallas import tpu as pltpu

D_MODEL = 1024
N_HEADS = 4
HEAD_W = 128
RET_W = N_HEADS * HEAD_W
GM_W = N_HEADS * HEAD_W
IN_W = 4 * RET_W + 2 * GM_W
D_FF = 2048
CONV_W = 3
CHUNK = 128
ROPE_BASE = 10000.0
PAST_LEN = 16384
EPS = 1e-6
N_MOD = 6

V7X_VMEM_LIMIT_BYTES = 56 * 1024 * 1024
PROMPT_TILE = 256
ROW_BLOCK = 32
CONV_PAD = 8
SAMPLE_SEQ_BLOCK = 8

F32 = jnp.float32
BF16 = jnp.bfloat16
NT_DIMS = (((1,), (1,)), ((), ()))
TN_DIMS = (((0,), (0,)), ((), ()))


def _silu(x):
    return x * (1.0 / (1.0 + jnp.exp(-x)))


def _gelu_tanh(x):
    c = math.sqrt(2.0 / math.pi)
    return 0.5 * x * (1.0 + jnp.tanh(c * (x + 0.044715 * (x * x * x))))


def _rms_unit(x):
    return x * lax.rsqrt(jnp.mean(x * x, axis=-1, keepdims=True) + EPS)


def _center_unit(x):
    xc = x - jnp.mean(x, axis=-1, keepdims=True)
    return xc * lax.rsqrt(jnp.mean(xc * xc, axis=-1, keepdims=True) + EPS)


def _rope(x, cos_full, sin_signed):
    return x * cos_full + pltpu.roll(x, HEAD_W // 2, 1) * sin_signed


def _dot(a, b):
    return jnp.dot(a, b, preferred_element_type=F32)


def _row_loop(n_rows, body):
    def step(i, carry):
        body(pl.ds(pl.multiple_of(i * ROW_BLOCK, ROW_BLOCK), ROW_BLOCK))
        return carry

    lax.fori_loop(0, n_rows // ROW_BLOCK, step, 0)


def _adaln_body(c_ref, w_ref, b_ref, o_ref):
    c = _silu(c_ref[...]).astype(BF16)
    o_ref[...] = _dot(c, w_ref[...].astype(BF16)) + b_ref[...]


def _adaln(c_all, w_ada, b_ada):
    depth = w_ada.shape[0]
    rows = c_all.shape[0]
    n_out = w_ada.shape[2]
    bn = 1536
    return pl.pallas_call(
        _adaln_body,
        grid=(depth, n_out // bn),
        in_specs=[
            pl.BlockSpec((rows, D_MODEL), lambda l, j: (0, 0)),
            pl.BlockSpec((None, D_MODEL, bn), lambda l, j: (l, 0, j)),
            pl.BlockSpec((None, 1, bn), lambda l, j: (l, 0, j)),
        ],
        out_specs=pl.BlockSpec((None, rows, bn), lambda l, j: (l, 0, j)),
        out_shape=jax.ShapeDtypeStruct((depth, rows, n_out), F32),
        compiler_params=pltpu.CompilerParams(
            dimension_semantics=("arbitrary", "arbitrary"), vmem_limit_bytes=V7X_VMEM_LIMIT_BYTES),
        name="adaln",
    )(c_all, w_ada, b_ada.reshape(depth, 1, n_out))


def _prompt_layer_body(x_ref, mod_ref, gmix_ref, gffn_ref, win_ref, wout_ref, wup_ref, wdown_ref,
                       cos_ref, sin_ref, dmask_ref, qdec_ref, kdec_ref, cdec_ref, gn_ref, ln_ref,
                       ws_ref, bs_ref, convw_ref, convb_ref, gfin_ref,
                       y_ref, ret_ref, conv_ref,
                       h_ref, p_ref, mix_ref, mo_ref, x1_ref, a_ref, f_ref, *, tm, final_norm):
    t = pl.program_id(1)

    @pl.when(t == 0)
    def _():
        ret_ref[...] = jnp.zeros_like(ret_ref)
        conv_ref[...] = jnp.zeros_like(conv_ref)

    sh1, sc1, gt1 = mod_ref[0:1, :], mod_ref[1:2, :], mod_ref[2:3, :]
    sh2, sc2, gt2 = mod_ref[3:4, :], mod_ref[4:5, :], mod_ref[5:6, :]

    def norm1(rows):
        h = _rms_unit(x_ref[rows, :]) * gmix_ref[...]
        h_ref[rows, :] = (h * (1.0 + sc1) + sh1).astype(BF16)

    _row_loop(tm, norm1)
    p_ref[...] = _dot(h_ref[...], win_ref[...])

    def chunk(c, carry):
        rows = pl.ds(pl.multiple_of(c * CHUNK, CHUNK), CHUNK)
        cos_full, sin_signed = cos_ref[rows, :], sin_ref[rows, :]
        for hh in range(N_HEADS):
            cs = slice(hh * HEAD_W, (hh + 1) * HEAD_W)
            q = _rope(p_ref[rows, hh * HEAD_W:(hh + 1) * HEAD_W], cos_full, sin_signed)
            k = _rope(p_ref[rows, RET_W + hh * HEAD_W:RET_W + (hh + 1) * HEAD_W], cos_full, sin_signed)
            k = k * (HEAD_W ** -0.5)
            vb = p_ref[rows, 2 * RET_W + hh * HEAD_W:2 * RET_W + (hh + 1) * HEAD_W].astype(BF16)
            g = p_ref[rows, 3 * RET_W + hh * HEAD_W:3 * RET_W + (hh + 1) * HEAD_W]
            qb = q.astype(BF16)
            qdb = (q * qdec_ref[:, cs]).astype(BF16)
            kb = k.astype(BF16)
            kdb = (k * kdec_ref[:, cs]).astype(BF16)
            scores = lax.dot_general(qb, kb, NT_DIMS, preferred_element_type=F32)
            att = (scores * dmask_ref[hh]).astype(BF16)
            state = ret_ref[hh]
            o = _dot(att, vb) + _dot(qdb, state.astype(BF16))
            ret_ref[hh] = cdec_ref[hh] * state + lax.dot_general(
                kdb, vb, TN_DIMS, preferred_element_type=F32)
            on = _center_unit(o) * gn_ref[:, cs]
            mix_ref[rows, cs] = (_silu(g) * on).astype(BF16)
        for gg in range(N_HEADS):
            cs = slice(gg * HEAD_W, (gg + 1) * HEAD_W)
            u = _gelu_tanh(p_ref[rows, 4 * RET_W + gg * HEAD_W:4 * RET_W + (gg + 1) * HEAD_W])
            vv = _gelu_tanh(p_ref[rows, 4 * RET_W + GM_W + gg * HEAD_W:4 * RET_W + GM_W + (gg + 1) * HEAD_W])
            vn = _center_unit(vv) * ln_ref[:, cs]
            s = _dot(ws_ref[gg], vn.astype(BF16)) + bs_ref[:, cs]
            mix_ref[rows, RET_W + gg * HEAD_W:RET_W + (gg + 1) * HEAD_W] = (u * s).astype(BF16)
        return carry

    lax.fori_loop(0, tm // CHUNK, chunk, 0)

    mo_ref[...] = _dot(mix_ref[...], wout_ref[...])

    def resid1(rows):
        x1 = x_ref[rows, :] + gt1 * mo_ref[rows, :]
        x1_ref[rows, :] = x1
        h = _rms_unit(x1) * gffn_ref[...]
        h_ref[rows, :] = (h * (1.0 + sc2) + sh2).astype(BF16)

    _row_loop(tm, resid1)

    a_ref[CONV_PAD - (CONV_W - 1):CONV_PAD, :] = conv_ref[...]
    a_ref[CONV_PAD:CONV_PAD + tm, :] = _dot(h_ref[...], wup_ref[...])
    conv_ref[...] = a_ref[CONV_PAD + tm - (CONV_W - 1):CONV_PAD + tm, :]
    cb = 512
    for j in range(D_FF // cb):
        for r in range(tm // ROW_BLOCK):
            halves = []
            for c0 in (j * cb, D_FF + j * cb):
                acc = convb_ref[:, c0:c0 + cb]
                for tap in range(CONV_W):
                    r0 = CONV_PAD - (CONV_W - 1) + tap + r * ROW_BLOCK
                    acc = acc + a_ref[r0:r0 + ROW_BLOCK, c0:c0 + cb] * convw_ref[tap:tap + 1, c0:c0 + cb]
                halves.append(acc)
            f_ref[r * ROW_BLOCK:(r + 1) * ROW_BLOCK, j * cb:(j + 1) * cb] = (
                _silu(halves[0]) * halves[1]).astype(BF16)
    mo_ref[...] = _dot(f_ref[...], wdown_ref[...])

    def resid2(rows):
        x2 = x1_ref[rows, :] + gt2 * mo_ref[rows, :]
        if final_norm:
            x2 = _rms_unit(x2) * gfin_ref[...]
        y_ref[rows, :] = x2

    _row_loop(tm, resid2)


def _const_spec(shape):
    zeros = (0,) * len(shape)
    return pl.BlockSpec(shape, lambda b, t: zeros, pipeline_mode=pl.Buffered(1))


def _prompt_layer(x, mod, lw, tabs, g_final, final_norm):
    batch, seq, _ = x.shape
    tm = PROMPT_TILE
    body = functools.partial(_prompt_layer_body, tm=tm, final_norm=final_norm)
    in_specs = [
        pl.BlockSpec((None, tm, D_MODEL), lambda b, t: (b, t, 0)),
        pl.BlockSpec((None, N_MOD, D_MODEL), lambda b, t: (b, 0, 0)),
        _const_spec((1, D_MODEL)), _const_spec((1, D_MODEL)),
        _const_spec((D_MODEL, IN_W)), _const_spec((RET_W + GM_W, D_MODEL)),
        _const_spec((D_MODEL, 2 * D_FF)), _const_spec((D_FF, D_MODEL)),
        pl.BlockSpec((tm, HEAD_W), lambda b, t: (t, 0)),
        pl.BlockSpec((tm, HEAD_W), lambda b, t: (t, 0)),
        _const_spec((N_HEADS, CHUNK, CHUNK)), _const_spec((CHUNK, RET_W)), _const_spec((CHUNK, RET_W)),
        _const_spec((N_HEADS, 1, HEAD_W)), _const_spec((1, RET_W)), _const_spec((1, GM_W)),
        _const_spec((N_HEADS, CHUNK, CHUNK)), _const_spec((CHUNK, GM_W)),
        _const_spec((CONV_W, 2 * D_FF)), _const_spec((1, 2 * D_FF)), _const_spec((1, D_MODEL)),
    ]
    out_specs = [
        pl.BlockSpec((None, tm, D_MODEL), lambda b, t: (b, t, 0)),
        pl.BlockSpec((None, N_HEADS, HEAD_W, HEAD_W), lambda b, t: (b, 0, 0, 0)),
        pl.BlockSpec((None, CONV_W - 1, 2 * D_FF), lambda b, t: (b, 0, 0)),
    ]
    out_shape = [
        jax.ShapeDtypeStruct((batch, seq, D_MODEL), F32),
        jax.ShapeDtypeStruct((batch, N_HEADS, HEAD_W, HEAD_W), F32),
        jax.ShapeDtypeStruct((batch, CONV_W - 1, 2 * D_FF), F32),
    ]
    scratch = [
        pltpu.VMEM((tm, D_MODEL), BF16),
        pltpu.VMEM((tm, IN_W), F32),
        pltpu.VMEM((tm, RET_W + GM_W), BF16),
        pltpu.VMEM((tm, D_MODEL), F32),
        pltpu.VMEM((tm, D_MODEL), F32),
        pltpu.VMEM((CONV_PAD + tm, 2 * D_FF), F32),
        pltpu.VMEM((tm, D_FF), BF16),
    ]
    return pl.pallas_call(
        body,
        grid=(batch, seq // tm),
        in_specs=in_specs,
        out_specs=out_specs,
        out_shape=out_shape,
        scratch_shapes=scratch,
        compiler_params=pltpu.CompilerParams(
            dimension_semantics=("arbitrary", "arbitrary"), vmem_limit_bytes=V7X_VMEM_LIMIT_BYTES),
        name="prompt_layer",
    )(x, mod, lw["g_mix"], lw["g_ffn"], lw["w_in"], lw["w_out"], lw["w_up"], lw["w_down"],
      tabs["cos"], tabs["sin"], tabs["dmask"], tabs["qdec"], tabs["kdec"], tabs["cdec"],
      lw["gn"], lw["ln"], lw["ws_tril"], lw["bs_full"], lw["conv_w"], lw["conv_b"], g_final)


def _sample_in_body(x_ref, mod_ref, gmix_ref, win_ref, cos_ref, sin_ref, ln_ref, ws0_ref, bs0_ref,
                    q_ref, k_ref, v_ref, sg_ref, ogm_ref, vn_ref):
    sh1, sc1 = mod_ref[:, 0:D_MODEL], mod_ref[:, D_MODEL:2 * D_MODEL]
    h = _rms_unit(x_ref[...]) * gmix_ref[...]
    h = (h * (1.0 + sc1) + sh1).astype(BF16)
    p = _dot(h, win_ref[...])
    for hh in range(N_HEADS):
        cs = slice(hh * HEAD_W, (hh + 1) * HEAD_W)
        q_ref[:, cs] = _rope(p[:, hh * HEAD_W:(hh + 1) * HEAD_W], cos_ref[...], sin_ref[...])
        k_ref[:, cs] = _rope(p[:, RET_W + hh * HEAD_W:RET_W + (hh + 1) * HEAD_W],
                             cos_ref[...], sin_ref[...]) * (HEAD_W ** -0.5)
        u = _gelu_tanh(p[:, 4 * RET_W + hh * HEAD_W:4 * RET_W + (hh + 1) * HEAD_W])
        vv = _gelu_tanh(p[:, 4 * RET_W + GM_W + hh * HEAD_W:4 * RET_W + GM_W + (hh + 1) * HEAD_W])
        vn = _center_unit(vv) * ln_ref[:, cs]
        vn_ref[:, cs] = vn
        ogm_ref[:, cs] = u * (ws0_ref[:, cs] * vn + bs0_ref[:, cs])
    v_ref[...] = p[:, 2 * RET_W:3 * RET_W]
    sg_ref[...] = _silu(p[:, 3 * RET_W:4 * RET_W])


def _sample_ret_body(q_ref, k_ref, v_ref, qt_ref, kt_ref, s_ref, gam_ref, o_ref, snew_ref, *, sb):
    for hh in range(N_HEADS):
        cs = slice(hh * HEAD_W, (hh + 1) * HEAD_W)
        gam = gam_ref[:, cs]
        v = v_ref[:, cs]
        qt = qt_ref[hh]
        kt = kt_ref[hh]
        for s in range(sb):
            state = s_ref[s, hh]
            snew_ref[s, hh] = gam * state + kt[:, s:s + 1] * v[s:s + 1, :]
            o_ref[s:s + 1, cs] = gam * jnp.sum(qt[:, s:s + 1] * state, axis=0, keepdims=True)
        qk = jnp.sum(q_ref[:, cs] * k_ref[:, cs], axis=-1, keepdims=True)
        o_ref[:, cs] = o_ref[:, cs] + qk * v


def _sample_out_body(x_ref, mod_ref, o_ref, sg_ref, ogm_ref, gn_ref, wout_ref, gffn_ref, wup_ref,
                     cbuf_ref, convw_ref, convb_ref, wdown_ref, gfin_ref,
                     y_ref, cnew_ref, *, final_norm):
    gt1 = mod_ref[:, 2 * D_MODEL:3 * D_MODEL]
    sh2, sc2 = mod_ref[:, 3 * D_MODEL:4 * D_MODEL], mod_ref[:, 4 * D_MODEL:5 * D_MODEL]
    gt2 = mod_ref[:, 5 * D_MODEL:6 * D_MODEL]
    parts = []
    for hh in range(N_HEADS):
        cs = slice(hh * HEAD_W, (hh + 1) * HEAD_W)
        on = _center_unit(o_ref[:, cs]) * gn_ref[:, cs]
        parts.append((sg_ref[:, cs] * on).astype(BF16))
    parts.append(ogm_ref[...].astype(BF16))
    mix = _dot(jnp.concatenate(parts, axis=1), wout_ref[...])
    x1 = x_ref[...] + gt1 * mix
    h = _rms_unit(x1) * gffn_ref[...]
    h = (h * (1.0 + sc2) + sh2).astype(BF16)
    a = _dot(h, wup_ref[...])
    w = 2 * D_FF
    buf0, buf1 = cbuf_ref[:, 0:w], cbuf_ref[:, w:2 * w]
    conv = convb_ref[...] + buf0 * convw_ref[0:1, :] + buf1 * convw_ref[1:2, :] + a * convw_ref[2:3, :]
    cnew_ref[:, 0:w] = buf1
    cnew_ref[:, w:2 * w] = a
    f = (_silu(conv[:, 0:D_FF]) * conv[:, D_FF:w]).astype(BF16)
    x2 = x1 + gt2 * _dot(f, wdown_ref[...])
    if final_norm:
        x2 = _rms_unit(x2) * gfin_ref[...]
    y_ref[...] = x2


def _whole(shape):
    zeros = (0,) * len(shape)
    return pl.BlockSpec(shape, lambda *_: zeros)


def _sample_layer(x, mod, state_ret, state_conv, lw, tabs, g_final, final_norm):
    n = x.shape[0]
    params = pltpu.CompilerParams(dimension_semantics=("arbitrary",), vmem_limit_bytes=V7X_VMEM_LIMIT_BYTES)
    act = jax.ShapeDtypeStruct((n, RET_W), F32)
    in_args = (x, mod, lw["g_mix"], lw["w_in"], tabs["cos_s"], tabs["sin_s"], lw["ln"], lw["ws0"], lw["bs0"])
    q, k, v, sg, ogm, vn = pl.pallas_call(
        _sample_in_body,
        grid=(1,),
        in_specs=[_whole(a.shape) for a in in_args],
        out_specs=[_whole(act.shape)] * 6,
        out_shape=[act] * 6,
        compiler_params=params,
        name="sample_in",
    )(*in_args)

    sb = SAMPLE_SEQ_BLOCK
    row_spec = pl.BlockSpec((sb, RET_W), lambda i: (i, 0))
    state_spec = pl.BlockSpec((sb, N_HEADS, HEAD_W, HEAD_W), lambda i: (i, 0, 0, 0))
    col_spec = pl.BlockSpec((None, N_HEADS, HEAD_W, sb), lambda i: (i, 0, 0, 0))

    def head_columns(a):
        return a.reshape(n // sb, sb, N_HEADS, HEAD_W).transpose(0, 2, 3, 1)

    o, s_new = pl.pallas_call(
        functools.partial(_sample_ret_body, sb=sb),
        grid=(n // sb,),
        in_specs=[row_spec, row_spec, row_spec, col_spec, col_spec, state_spec, _whole((1, RET_W))],
        out_specs=[row_spec, state_spec],
        out_shape=[act, jax.ShapeDtypeStruct(state_ret.shape, F32)],
        compiler_params=params,
        name="sample_ret",
    )(q, k, v, head_columns(q), head_columns(k), state_ret, tabs["gamma_s"])

    cbuf = state_conv.reshape(n, (CONV_W - 1) * 2 * D_FF)
    out_args = (x, mod, o, sg, ogm, lw["gn"], lw["w_out"], lw["g_ffn"], lw["w_up"], cbuf,
                lw["conv_w"], lw["conv_b"], lw["w_down"], g_final)
    y, c_new = pl.pallas_call(
        functools.partial(_sample_out_body, final_norm=final_norm),
        grid=(1,),
        in_specs=[_whole(a.shape) for a in out_args],
        out_specs=[_whole((n, D_MODEL)), _whole(cbuf.shape)],
        out_shape=[jax.ShapeDtypeStruct((n, D_MODEL), F32), jax.ShapeDtypeStruct(cbuf.shape, F32)],
        compiler_params=params,
        name="sample_out",
    )(*out_args)
    return y, s_new, c_new.reshape(state_conv.shape), vn


def _rope_tables(pos):
    half = HEAD_W // 2
    freqs = jnp.exp(-math.log(ROPE_BASE) * jnp.arange(half, dtype=F32) / half)
    ang = pos.astype(F32)[:, None] * freqs[None, :]
    cos, sin = jnp.cos(ang), jnp.sin(ang)
    return jnp.concatenate([cos, cos], axis=-1), jnp.concatenate([-sin, sin], axis=-1)


def _decay_tables(chunk):
    lg = jnp.log1p(-jnp.exp2(-5.0 - jnp.arange(N_HEADS, dtype=F32)))
    i = jnp.arange(chunk, dtype=F32)
    diff = i[:, None] - i[None, :]
    dmask = jnp.where(diff[None] >= 0.0, jnp.exp(jnp.maximum(diff, 0.0)[None] * lg[:, None, None]), 0.0)
    q_dec = jnp.exp((i[:, None] + 1.0) * lg[None, :])
    k_dec = jnp.exp((chunk - 1.0 - i)[:, None] * lg[None, :])
    chunk_dec = jnp.exp(chunk * lg)
    return dmask, q_dec, k_dec, chunk_dec


def _per_head_lanes(a):
    return jnp.repeat(a, HEAD_W, axis=-1)


def kernel(x_prompt, x_sample, state_ret, state_conv, c_prompt, c_sample, w_ada, b_ada, g_mix, w_in,
           ret_gn_gain, gmlp_ln_gain, w_s, b_s, w_out, g_ffn, w_up, conv_w, conv_b, w_down, g_final):
    depth = w_in.shape[0]
    batch, seq, _ = x_prompt.shape
    n_dec, dec_seq, _ = x_sample.shape
    assert dec_seq == 1 and seq % PROMPT_TILE == 0 and PROMPT_TILE % CHUNK == 0

    cos_p, sin_p = _rope_tables(jnp.arange(seq, dtype=jnp.int32))
    cos_s, sin_s = _rope_tables(PAST_LEN + jnp.arange(dec_seq, dtype=jnp.int32))
    dmask, q_dec, k_dec, chunk_dec = _decay_tables(CHUNK)
    _, _, _, gamma_s = _decay_tables(dec_seq)
    tabs = {
        "cos": cos_p, "sin": sin_p, "cos_s": cos_s, "sin_s": sin_s,
        "dmask": dmask, "qdec": _per_head_lanes(q_dec), "kdec": _per_head_lanes(k_dec),
        "cdec": jnp.broadcast_to(chunk_dec[:, None, None], (N_HEADS, 1, HEAD_W)),
        "gamma_s": _per_head_lanes(gamma_s[None, :]),
    }

    mod = _adaln(jnp.concatenate([c_prompt, c_sample], axis=0), w_ada, b_ada)
    tril = jnp.tril(jnp.ones((CHUNK, CHUNK), dtype=bool))
    g_fin = g_final.reshape(1, D_MODEL)

    xp, xs = x_prompt, x_sample.reshape(n_dec, D_MODEL)
    ret_p, conv_p, ret_s, conv_s, v_s = [], [], [], [], []
    for l in range(depth):
        lw = {
            "g_mix": g_mix[l].reshape(1, D_MODEL), "g_ffn": g_ffn[l].reshape(1, D_MODEL),
            "w_in": w_in[l].astype(BF16), "w_out": w_out[l].astype(BF16),
            "w_up": w_up[l].astype(BF16), "w_down": w_down[l].astype(BF16),
            "gn": ret_gn_gain[l].reshape(1, RET_W), "ln": gmlp_ln_gain[l].reshape(1, GM_W),
            "ws_tril": jnp.where(tril[None], w_s[l], 0.0).astype(BF16),
            "bs_full": _per_head_lanes(b_s[l].T),
            "ws0": _per_head_lanes(w_s[l][:, 0, 0][None, :]), "bs0": _per_head_lanes(b_s[l][:, 0][None, :]),
            "conv_w": conv_w[l], "conv_b": conv_b[l].reshape(1, 2 * D_FF),
        }
        last = l == depth - 1
        mod_p = mod[l, :batch].reshape(batch, N_MOD, D_MODEL)
        xp, sp, bp = _prompt_layer(xp, mod_p, lw, tabs, g_fin, last)
        xs, ss, bs, vs = _sample_layer(xs, mod[l, batch:], state_ret[l], state_conv[l], lw, tabs, g_fin, last)
        ret_p.append(sp); conv_p.append(bp); ret_s.append(ss); conv_s.append(bs)
        v_s.append(vs.reshape(n_dec, dec_seq, GM_W))
    return (xp, xs.reshape(n_dec, dec_seq, D_MODEL), jnp.stack(ret_p), jnp.stack(conv_p),
            jnp.stack(ret_s), jnp.stack(conv_s), jnp.stack(v_s))
```

```python
import functools
import math

import jax
import jax.numpy as jnp
from jax import lax
from jax.experimental import pallas as pl
from jax.experimental.pallas import tpu as pltpu

D_MODEL = 1024
N_HEADS = 4
HEAD_W = 128
RET_W = N_HEADS * HEAD_W
GM_W = N_HEADS * HEAD_W
IN_W = 4 * RET_W + 2 * GM_W
D_FF = 2048
CONV_W = 3
CHUNK = 128
ROPE_BASE = 10000.0
PAST_LEN = 16384
EPS = 1e-6
N_MOD = 6

V7X_VMEM_LIMIT_BYTES = 56 * 1024 * 1024
PROMPT_TILE = 256
ROW_BLOCK = 32
CONV_PAD = 8
SAMPLE_SEQ_BLOCK = 8

F32 = jnp.float32
BF16 = jnp.bfloat16
NT_DIMS = (((1,), (1,)), ((), ()))
TN_DIMS = (((0,), (0,)), ((), ()))


def _silu(x):
    return x * (1.0 / (1.0 + jnp.exp(-x)))


def _gelu_tanh(x):
    c = math.sqrt(2.0 / math.pi)
    return 0.5 * x * (1.0 + jnp.tanh(c * (x + 0.044715 * (x * x * x))))


def _rms_unit(x):
    return x * lax.rsqrt(jnp.mean(x * x, axis=-1, keepdims=True) + EPS)


def _center_unit(x):
    xc = x - jnp.mean(x, axis=-1, keepdims=True)
    return xc * lax.rsqrt(jnp.mean(xc * xc, axis=-1, keepdims=True) + EPS)


def _rope(x, cos_full, sin_signed):
    return x * cos_full + pltpu.roll(x, HEAD_W // 2, 1) * sin_signed


def _dot(a, b):
    return jnp.dot(a, b, preferred_element_type=F32)


def _row_loop(n_rows, body):
    def step(i, carry):
        body(pl.ds(pl.multiple_of(i * ROW_BLOCK, ROW_BLOCK), ROW_BLOCK))
        return carry

    lax.fori_loop(0, n_rows // ROW_BLOCK, step, 0)


def _adaln_body(c_ref, w_ref, b_ref, o_ref):
    c = _silu(c_ref[...]).astype(BF16)
    o_ref[...] = _dot(c, w_ref[...].astype(BF16)) + b_ref[...]


def _adaln(c_all, w_ada, b_ada):
    depth = w_ada.shape[0]
    rows = c_all.shape[0]
    n_out = w_ada.shape[2]
    bn = 1536
    return pl.pallas_call(
        _adaln_body,
        grid=(depth, n_out // bn),
        in_specs=[
            pl.BlockSpec((rows, D_MODEL), lambda l, j: (0, 0)),
            pl.BlockSpec((None, D_MODEL, bn), lambda l, j: (l, 0, j)),
            pl.BlockSpec((None, 1, bn), lambda l, j: (l, 0, j)),
        ],
        out_specs=pl.BlockSpec((None, rows, bn), lambda l, j: (l, 0, j)),
        out_shape=jax.ShapeDtypeStruct((depth, rows, n_out), F32),
        compiler_params=pltpu.CompilerParams(
            dimension_semantics=("arbitrary", "arbitrary"), vmem_limit_bytes=V7X_VMEM_LIMIT_BYTES),
        name="adaln",
    )(c_all, w_ada, b_ada.reshape(depth, 1, n_out))


def _prompt_layer_body(x_ref, modf_ref, modb_ref, gmix_ref, gffn_ref, win_ref, wout_ref, wup_ref, wdown_ref,
                       cos_ref, sin_ref, dmask_ref, qdec_ref, kdec_ref, cdec_ref, gn_ref, ln_ref,
                       ws_ref, bs_ref, convw_ref, convb_ref, gfin_ref,
                       y_ref, ret_ref, conv_ref,
                       h_ref, p_ref, mix_ref, mo_ref, x1_ref, h2_ref, a_ref, f_ref, mo2_ref,
                       *, tm, tiles_per_seq, n_tiles, final_norm):
    s = pl.program_id(0)
    front_on = s < n_tiles
    t_front = lax.rem(jnp.minimum(s, n_tiles - 1), tiles_per_seq)
    t_back = lax.rem(jnp.maximum(s - 1, 0), tiles_per_seq)

    @pl.when(s == 0)
    def _():
        x1_ref[...] = jnp.zeros_like(x1_ref)
        h2_ref[...] = jnp.zeros_like(h2_ref)

    @pl.when(jnp.logical_and(front_on, t_front == 0))
    def _():
        ret_ref[...] = jnp.zeros_like(ret_ref)

    @pl.when(t_back == 0)
    def _():
        conv_ref[...] = jnp.zeros_like(conv_ref)

    sh1, sc1, gt1 = modf_ref[0:1, :], modf_ref[1:2, :], modf_ref[2:3, :]
    sh2, sc2 = modf_ref[3:4, :], modf_ref[4:5, :]
    gt2 = modb_ref[5:6, :]
    row_blocks = [slice(r * ROW_BLOCK, (r + 1) * ROW_BLOCK) for r in range(tm // ROW_BLOCK)]

    y_ref[...] = x1_ref[...]
    a_ref[CONV_PAD - (CONV_W - 1):CONV_PAD, :] = conv_ref[...]
    a_ref[CONV_PAD:CONV_PAD + tm, :] = _dot(h2_ref[...], wup_ref[...])
    conv_ref[...] = a_ref[CONV_PAD + tm - (CONV_W - 1):CONV_PAD + tm, :]
    cb = 512
    for j in range(D_FF // cb):
        for r, rows in enumerate(row_blocks):
            halves = []
            for c0 in (j * cb, D_FF + j * cb):
                acc = convb_ref[:, c0:c0 + cb]
                for tap in range(CONV_W):
                    r0 = CONV_PAD - (CONV_W - 1) + tap + r * ROW_BLOCK
                    acc = acc + a_ref[r0:r0 + ROW_BLOCK, c0:c0 + cb] * convw_ref[tap:tap + 1, c0:c0 + cb]
                halves.append(acc)
            f_ref[rows, j * cb:(j + 1) * cb] = (_silu(halves[0]) * halves[1]).astype(BF16)
    mo2_ref[...] = _dot(f_ref[...], wdown_ref[...])
    for rows in row_blocks:
        x2 = y_ref[rows, :] + gt2 * mo2_ref[rows, :]
        if final_norm:
            x2 = _rms_unit(x2) * gfin_ref[...]
        y_ref[rows, :] = x2

    for rows in row_blocks:
        h = _rms_unit(x_ref[rows, :]) * gmix_ref[...]
        h_ref[rows, :] = (h * (1.0 + sc1) + sh1).astype(BF16)
    p_ref[...] = _dot(h_ref[...], win_ref[...])

    for c in range(tm // CHUNK):
        rows = slice(c * CHUNK, (c + 1) * CHUNK)
        cos_full, sin_signed = cos_ref[rows, :], sin_ref[rows, :]
        for hh in range(N_HEADS):
            cs = slice(hh * HEAD_W, (hh + 1) * HEAD_W)
            q = _rope(p_ref[rows, hh * HEAD_W:(hh + 1) * HEAD_W], cos_full, sin_signed)
            k = _rope(p_ref[rows, RET_W + hh * HEAD_W:RET_W + (hh + 1) * HEAD_W], cos_full, sin_signed)
            k = k * (HEAD_W ** -0.5)
            vb = p_ref[rows, 2 * RET_W + hh * HEAD_W:2 * RET_W + (hh + 1) * HEAD_W].astype(BF16)
            g = p_ref[rows, 3 * RET_W + hh * HEAD_W:3 * RET_W + (hh + 1) * HEAD_W]
            qb = q.astype(BF16)
            qdb = (q * qdec_ref[:, cs]).astype(BF16)
            kb = k.astype(BF16)
            kdb = (k * kdec_ref[:, cs]).astype(BF16)
            scores = lax.dot_general(qb, kb, NT_DIMS, preferred_element_type=F32)
            att = (scores * dmask_ref[hh]).astype(BF16)
            state = ret_ref[hh]
            o = _dot(att, vb) + _dot(qdb, state.astype(BF16))
            new_state = cdec_ref[hh] * state + lax.dot_general(kdb, vb, TN_DIMS, preferred_element_type=F32)
            ret_ref[hh] = jnp.where(front_on, new_state, state)
            on = _center_unit(o) * gn_ref[:, cs]
            mix_ref[rows, cs] = (_silu(g) * on).astype(BF16)
        for gg in range(N_HEADS):
            cs = slice(gg * HEAD_W, (gg + 1) * HEAD_W)
            u = _gelu_tanh(p_ref[rows, 4 * RET_W + gg * HEAD_W:4 * RET_W + (gg + 1) * HEAD_W])
            vv = _gelu_tanh(p_ref[rows, 4 * RET_W + GM_W + gg * HEAD_W:4 * RET_W + GM_W + (gg + 1) * HEAD_W])
            vn = _center_unit(vv) * ln_ref[:, cs]
            sg = _dot(ws_ref[gg], vn.astype(BF16)) + bs_ref[:, cs]
            mix_ref[rows, RET_W + gg * HEAD_W:RET_W + (gg + 1) * HEAD_W] = (u * sg).astype(BF16)

    mo_ref[...] = _dot(mix_ref[...], wout_ref[...])
    for rows in row_blocks:
        x1 = x_ref[rows, :] + gt1 * mo_ref[rows, :]
        x1_ref[rows, :] = x1
        h = _rms_unit(x1) * gffn_ref[...]
        h2_ref[rows, :] = (h * (1.0 + sc2) + sh2).astype(BF16)


def _const_spec(shape):
    zeros = (0,) * len(shape)
    return pl.BlockSpec(shape, lambda s: zeros, pipeline_mode=pl.Buffered(1))


def _prompt_layer(x, mod, lw, tabs, g_final, final_norm):
    batch, seq, _ = x.shape
    tm = PROMPT_TILE
    tiles_per_seq = seq // tm
    n_tiles = batch * tiles_per_seq
    body = functools.partial(_prompt_layer_body, tm=tm, tiles_per_seq=tiles_per_seq, n_tiles=n_tiles,
                             final_norm=final_norm)

    def front(s):
        i = jnp.minimum(s, n_tiles - 1)
        return i // tiles_per_seq, lax.rem(i, tiles_per_seq)

    def back(s):
        i = jnp.maximum(s - 1, 0)
        return i // tiles_per_seq, lax.rem(i, tiles_per_seq)

    in_specs = [
        pl.BlockSpec((None, tm, D_MODEL), lambda s: (*front(s), 0)),
        pl.BlockSpec((None, N_MOD, D_MODEL), lambda s: (front(s)[0], 0, 0)),
        pl.BlockSpec((None, N_MOD, D_MODEL), lambda s: (back(s)[0], 0, 0)),
        _const_spec((1, D_MODEL)), _const_spec((1, D_MODEL)),
        _const_spec((D_MODEL, IN_W)), _const_spec((RET_W + GM_W, D_MODEL)),
        _const_spec((D_MODEL, 2 * D_FF)), _const_spec((D_FF, D_MODEL)),
        pl.BlockSpec((tm, HEAD_W), lambda s: (front(s)[1], 0)),
        pl.BlockSpec((tm, HEAD_W), lambda s: (front(s)[1], 0)),
        _const_spec((N_HEADS, CHUNK, CHUNK)), _const_spec((CHUNK, RET_W)), _const_spec((CHUNK, RET_W)),
        _const_spec((N_HEADS, 1, HEAD_W)), _const_spec((1, RET_W)), _const_spec((1, GM_W)),
        _const_spec((N_HEADS, CHUNK, CHUNK)), _const_spec((CHUNK, GM_W)),
        _const_spec((CONV_W, 2 * D_FF)), _const_spec((1, 2 * D_FF)), _const_spec((1, D_MODEL)),
    ]
    out_specs = [
        pl.BlockSpec((None, tm, D_MODEL), lambda s: (*back(s), 0)),
        pl.BlockSpec((None, N_HEADS, HEAD_W, HEAD_W), lambda s: (front(s)[0], 0, 0, 0)),
        pl.BlockSpec((None, CONV_W - 1, 2 * D_FF), lambda s: (back(s)[0], 0, 0)),
    ]
    out_shape = [
        jax.ShapeDtypeStruct((batch, seq, D_MODEL), F32),
        jax.ShapeDtypeStruct((batch, N_HEADS, HEAD_W, HEAD_W), F32),
        jax.ShapeDtypeStruct((batch, CONV_W - 1, 2 * D_FF), F32),
    ]
    scratch = [
        pltpu.VMEM((tm, D_MODEL), BF16),
        pltpu.VMEM((tm, IN_W), F32),
        pltpu.VMEM((tm, RET_W + GM_W), BF16),
        pltpu.VMEM((tm, D_MODEL), F32),
        pltpu.VMEM((tm, D_MODEL), F32),
        pltpu.VMEM((tm, D_MODEL), BF16),
        pltpu.VMEM((CONV_PAD + tm, 2 * D_FF), F32),
        pltpu.VMEM((tm, D_FF), BF16),
        pltpu.VMEM((tm, D_MODEL), F32),
    ]
    return pl.pallas_call(
        body,
        grid=(n_tiles + 1,),
        in_specs=in_specs,
        out_specs=out_specs,
        out_shape=out_shape,
        scratch_shapes=scratch,
        compiler_params=pltpu.CompilerParams(
            dimension_semantics=("arbitrary",), vmem_limit_bytes=V7X_VMEM_LIMIT_BYTES),
        name="prompt_layer",
    )(x, mod, mod, lw["g_mix"], lw["g_ffn"], lw["w_in"], lw["w_out"], lw["w_up"], lw["w_down"],
      tabs["cos"], tabs["sin"], tabs["dmask"], tabs["qdec"], tabs["kdec"], tabs["cdec"],
      lw["gn"], lw["ln"], lw["ws_tril"], lw["bs_full"], lw["conv_w"], lw["conv_b"], g_final)


def _sample_in_body(x_ref, mod_ref, gmix_ref, win_ref, cos_ref, sin_ref, ln_ref, ws0_ref, bs0_ref,
                    q_ref, k_ref, v_ref, sg_ref, ogm_ref, vn_ref):
    sh1, sc1 = mod_ref[:, 0:D_MODEL], mod_ref[:, D_MODEL:2 * D_MODEL]
    h = _rms_unit(x_ref[...]) * gmix_ref[...]
    h = (h * (1.0 + sc1) + sh1).astype(BF16)
    p = _dot(h, win_ref[...])
    for hh in range(N_HEADS):
        cs = slice(hh * HEAD_W, (hh + 1) * HEAD_W)
        q_ref[:, cs] = _rope(p[:, hh * HEAD_W:(hh + 1) * HEAD_W], cos_ref[...], sin_ref[...])
        k_ref[:, cs] = _rope(p[:, RET_W + hh * HEAD_W:RET_W + (hh + 1) * HEAD_W],
                             cos_ref[...], sin_ref[...]) * (HEAD_W ** -0.5)
        u = _gelu_tanh(p[:, 4 * RET_W + hh * HEAD_W:4 * RET_W + (hh + 1) * HEAD_W])
        vv = _gelu_tanh(p[:, 4 * RET_W + GM_W + hh * HEAD_W:4 * RET_W + GM_W + (hh + 1) * HEAD_W])
        vn = _center_unit(vv) * ln_ref[:, cs]
        vn_ref[:, cs] = vn
        ogm_ref[:, cs] = u * (ws0_ref[:, cs] * vn + bs0_ref[:, cs])
    v_ref[...] = p[:, 2 * RET_W:3 * RET_W]
    sg_ref[...] = _silu(p[:, 3 * RET_W:4 * RET_W])


def _sample_ret_body(q_ref, k_ref, v_ref, qt_ref, kt_ref, s_ref, gam_ref, o_ref, snew_ref, *, sb):
    for hh in range(N_HEADS):
        cs = slice(hh * HEAD_W, (hh + 1) * HEAD_W)
        gam = gam_ref[:, cs]
        v = v_ref[:, cs]
        qt = qt_ref[hh]
        kt = kt_ref[hh]
        for s in range(sb):
            state = s_ref[s, hh]
            snew_ref[s, hh] = gam * state + kt[:, s:s + 1] * v[s:s + 1, :]
            o_ref[s:s + 1, cs] = gam * jnp.sum(qt[:, s:s + 1] * state, axis=0, keepdims=True)
        qk = jnp.sum(q_ref[:, cs] * k_ref[:, cs], axis=-1, keepdims=True)
        o_ref[:, cs] = o_ref[:, cs] + qk * v


def _sample_out_body(x_ref, mod_ref, o_ref, sg_ref, ogm_ref, gn_ref, wout_ref, gffn_ref, wup_ref,
                     cbuf_ref, convw_ref, convb_ref, wdown_ref, gfin_ref,
                     y_ref, cnew_ref, *, final_norm):
    gt1 = mod_ref[:, 2 * D_MODEL:3 * D_MODEL]
    sh2, sc2 = mod_ref[:, 3 * D_MODEL:4 * D_MODEL], mod_ref[:, 4 * D_MODEL:5 * D_MODEL]
    gt2 = mod_ref[:, 5 * D_MODEL:6 * D_MODEL]
    parts = []
    for hh in range(N_HEADS):
        cs = slice(hh * HEAD_W, (hh + 1) * HEAD_W)
        on = _center_unit(o_ref[:, cs]) * gn_ref[:, cs]
        parts.append((sg_ref[:, cs] * on).astype(BF16))
    parts.append(ogm_ref[...].astype(BF16))
    mix = _dot(jnp.concatenate(parts, axis=1), wout_ref[...])
    x1 = x_ref[...] + gt1 * mix
    h = _rms_unit(x1) * gffn_ref[...]
    h = (h * (1.0 + sc2) + sh2).astype(BF16)
    a = _dot(h, wup_ref[...])
    w = 2 * D_FF
    buf0, buf1 = cbuf_ref[:, 0:w], cbuf_ref[:, w:2 * w]
    conv = convb_ref[...] + buf0 * convw_ref[0:1, :] + buf1 * convw_ref[1:2, :] + a * convw_ref[2:3, :]
    cnew_ref[:, 0:w] = buf1
    cnew_ref[:, w:2 * w] = a
    f = (_silu(conv[:, 0:D_FF]) * conv[:, D_FF:w]).astype(BF16)
    x2 = x1 + gt2 * _dot(f, wdown_ref[...])
    if final_norm:
        x2 = _rms_unit(x2) * gfin_ref[...]
    y_ref[...] = x2


def _whole(shape):
    zeros = (0,) * len(shape)
    return pl.BlockSpec(shape, lambda *_: zeros)


def _sample_layer(x, mod, state_ret, state_conv, lw, tabs, g_final, final_norm):
    n = x.shape[0]
    params = pltpu.CompilerParams(dimension_semantics=("arbitrary",), vmem_limit_bytes=V7X_VMEM_LIMIT_BYTES)
    act = jax.ShapeDtypeStruct((n, RET_W), F32)
    in_args = (x, mod, lw["g_mix"], lw["w_in"], tabs["cos_s"], tabs["sin_s"], lw["ln"], lw["ws0"], lw["bs0"])
    q, k, v, sg, ogm, vn = pl.pallas_call(
        _sample_in_body,
        grid=(1,),
        in_specs=[_whole(a.shape) for a in in_args],
        out_specs=[_whole(act.shape)] * 6,
        out_shape=[act] * 6,
        compiler_params=params,
        name="sample_in",
    )(*in_args)

    sb = SAMPLE_SEQ_BLOCK
    row_spec = pl.BlockSpec((sb, RET_W), lambda i: (i, 0))
    state_spec = pl.BlockSpec((sb, N_HEADS, HEAD_W, HEAD_W), lambda i: (i, 0, 0, 0))
    col_spec = pl.BlockSpec((None, N_HEADS, HEAD_W, sb), lambda i: (i, 0, 0, 0))

    def head_columns(a):
        return a.reshape(n // sb, sb, N_HEADS, HEAD_W).transpose(0, 2, 3, 1)

    o, s_new = pl.pallas_call(
        functools.partial(_sample_ret_body, sb=sb),
        grid=(n // sb,),
        in_specs=[row_spec, row_spec, row_spec, col_spec, col_spec, state_spec, _whole((1, RET_W))],
        out_specs=[row_spec, state_spec],
        out_shape=[act, jax.ShapeDtypeStruct(state_ret.shape, F32)],
        compiler_params=params,
        name="sample_ret",
    )(q, k, v, head_columns(q), head_columns(k), state_ret, tabs["gamma_s"])

    cbuf = state_conv.reshape(n, (CONV_W - 1) * 2 * D_FF)
    out_args = (x, mod, o, sg, ogm, lw["gn"], lw["w_out"], lw["g_ffn"], lw["w_up"], cbuf,
                lw["conv_w"], lw["conv_b"], lw["w_down"], g_final)
    y, c_new = pl.pallas_call(
        functools.partial(_sample_out_body, final_norm=final_norm),
        grid=(1,),
        in_specs=[_whole(a.shape) for a in out_args],
        out_specs=[_whole((n, D_MODEL)), _whole(cbuf.shape)],
        out_shape=[jax.ShapeDtypeStruct((n, D_MODEL), F32), jax.ShapeDtypeStruct(cbuf.shape, F32)],
        compiler_params=params,
        name="sample_out",
    )(*out_args)
    return y, s_new, c_new.reshape(state_conv.shape), vn


def _rope_tables(pos):
    half = HEAD_W // 2
    freqs = jnp.exp(-math.log(ROPE_BASE) * jnp.arange(half, dtype=F32) / half)
    ang = pos.astype(F32)[:, None] * freqs[None, :]
    cos, sin = jnp.cos(ang), jnp.sin(ang)
    return jnp.concatenate([cos, cos], axis=-1), jnp.concatenate([-sin, sin], axis=-1)


def _decay_tables(chunk):
    lg = jnp.log1p(-jnp.exp2(-5.0 - jnp.arange(N_HEADS, dtype=F32)))
    i = jnp.arange(chunk, dtype=F32)
    diff = i[:, None] - i[None, :]
    dmask = jnp.where(diff[None] >= 0.0, jnp.exp(jnp.maximum(diff, 0.0)[None] * lg[:, None, None]), 0.0)
    q_dec = jnp.exp((i[:, None] + 1.0) * lg[None, :])
    k_dec = jnp.exp((chunk - 1.0 - i)[:, None] * lg[None, :])
    chunk_dec = jnp.exp(chunk * lg)
    return dmask, q_dec, k_dec, chunk_dec


def _per_head_lanes(a):
    return jnp.repeat(a, HEAD_W, axis=-1)


def kernel(x_prompt, x_sample, state_ret, state_conv, c_prompt, c_sample, w_ada, b_ada, g_mix, w_in,
           ret_gn_gain, gmlp_ln_gain, w_s, b_s, w_out, g_ffn, w_up, conv_w, conv_b, w_down, g_final):
    depth = w_in.shape[0]
    batch, seq, _ = x_prompt.shape
    n_dec, dec_seq, _ = x_sample.shape
    assert dec_seq == 1 and seq % PROMPT_TILE == 0 and PROMPT_TILE % CHUNK == 0

    cos_p, sin_p = _rope_tables(jnp.arange(seq, dtype=jnp.int32))
    cos_s, sin_s = _rope_tables(PAST_LEN + jnp.arange(dec_seq, dtype=jnp.int32))
    dmask, q_dec, k_dec, chunk_dec = _decay_tables(CHUNK)
    _, _, _, gamma_s = _decay_tables(dec_seq)
    tabs = {
        "cos": cos_p, "sin": sin_p, "cos_s": cos_s, "sin_s": sin_s,
        "dmask": dmask, "qdec": _per_head_lanes(q_dec), "kdec": _per_head_lanes(k_dec),
        "cdec": jnp.broadcast_to(chunk_dec[:, None, None], (N_HEADS, 1, HEAD_W)),
        "gamma_s": _per_head_lanes(gamma_s[None, :]),
    }

    mod = _adaln(jnp.concatenate([c_prompt, c_sample], axis=0), w_ada, b_ada)
    tril = jnp.tril(jnp.ones((CHUNK, CHUNK), dtype=bool))
    g_fin = g_final.reshape(1, D_MODEL)

    xp, xs = x_prompt, x_sample.reshape(n_dec, D_MODEL)
    ret_p, conv_p, ret_s, conv_s, v_s = [], [], [], [], []
    for l in range(depth):
        lw = {
            "g_mix": g_mix[l].reshape(1, D_MODEL), "g_ffn": g_ffn[l].reshape(1, D_MODEL),
            "w_in": w_in[l].astype(BF16), "w_out": w_out[l].astype(BF16),
            "w_up": w_up[l].astype(BF16), "w_down": w_down[l].astype(BF16),
            "gn": ret_gn_gain[l].reshape(1, RET_W), "ln": gmlp_ln_gain[l].reshape(1, GM_W),
            "ws_tril": jnp.where(tril[None], w_s[l], 0.0).astype(BF16),
            "bs_full": _per_head_lanes(b_s[l].T),
            "ws0": _per_head_lanes(w_s[l][:, 0, 0][None, :]), "bs0": _per_head_lanes(b_s[l][:, 0][None, :]),
            "conv_w": conv_w[l], "conv_b": conv_b[l].reshape(1, 2 * D_FF),
        }
        last = l == depth - 1
        mod_p = mod[l, :batch].reshape(batch, N_MOD, D_MODEL)
        xp, sp, bp = _prompt_layer(xp, mod_p, lw, tabs, g_fin, last)
        xs, ss, bs, vs = _sample_layer(xs, mod[l, batch:], state_ret[l], state_conv[l], lw, tabs, g_fin, last)
        ret_p.append(sp); conv_p.append(bp); ret_s.append(ss); conv_s.append(bs)
        v_s.append(vs.reshape(n_dec, dec_seq, GM_W))
    return (xp, xs.reshape(n_dec, dec_seq, D_MODEL), jnp.stack(ret_p), jnp.stack(conv_p),
            jnp.stack(ret_s), jnp.stack(conv_s), jnp.stack(v_s))
```

```python
import functools
import math

import jax
import jax.numpy as jnp
import numpy as np
from jax import lax
from jax.experimental import pallas as pl
from jax.experimental.pallas import tpu as pltpu

D_MODEL = 1024
N_HEADS = 4
HEAD_W = 128
RET_W = N_HEADS * HEAD_W
GM_W = N_HEADS * HEAD_W
IN_W = 4 * RET_W + 2 * GM_W
D_FF = 2048
CONV_W = 3
CHUNK = 128
ROPE_BASE = 10000.0
PAST_LEN = 16384
EPS = 1e-6
N_MOD = 6

V7X_VMEM_LIMIT_BYTES = 56 * 1024 * 1024
PROMPT_TILE = 256
ROW_BLOCK = 32
CONV_PAD = 8
SAMPLE_SEQ_BLOCK = 8

F32 = jnp.float32
BF16 = jnp.bfloat16
NT_DIMS = (((1,), (1,)), ((), ()))
TN_DIMS = (((0,), (0,)), ((), ()))


def _silu(x):
    return x * (1.0 / (1.0 + jnp.exp(-x)))


def _gelu_tanh(x):
    c = math.sqrt(2.0 / math.pi)
    return 0.5 * x * (1.0 + jnp.tanh(c * (x + 0.044715 * (x * x * x))))


def _rms_unit(x):
    return x * lax.rsqrt(jnp.mean(x * x, axis=-1, keepdims=True) + EPS)


def _center_unit(x):
    xc = x - jnp.mean(x, axis=-1, keepdims=True)
    return xc * lax.rsqrt(jnp.mean(xc * xc, axis=-1, keepdims=True) + EPS)


def _rope(x, cos_full, sin_signed):
    return x * cos_full + pltpu.roll(x, HEAD_W // 2, 1) * sin_signed


def _dot(a, b):
    return jnp.dot(a, b, preferred_element_type=F32)


def _adaln_body(c_ref, w_ref, b_ref, o_ref):
    c = _silu(c_ref[...]).astype(BF16)
    o_ref[...] = _dot(c, w_ref[...].astype(BF16)) + b_ref[...]


def _adaln(c_all, w_ada, b_ada):
    depth = w_ada.shape[0]
    rows = c_all.shape[0]
    n_out = w_ada.shape[2]
    bn = 1536
    return pl.pallas_call(
        _adaln_body,
        grid=(depth, n_out // bn),
        in_specs=[
            pl.BlockSpec((rows, D_MODEL), lambda l, j: (0, 0)),
            pl.BlockSpec((None, D_MODEL, bn), lambda l, j: (l, 0, j)),
            pl.BlockSpec((None, 1, bn), lambda l, j: (l, 0, j)),
        ],
        out_specs=pl.BlockSpec((None, rows, bn), lambda l, j: (l, 0, j)),
        out_shape=jax.ShapeDtypeStruct((depth, rows, n_out), F32),
        compiler_params=pltpu.CompilerParams(
            dimension_semantics=("arbitrary", "arbitrary"), vmem_limit_bytes=V7X_VMEM_LIMIT_BYTES),
        name="adaln",
    )(c_all, w_ada, b_ada.reshape(depth, 1, n_out))


def _prompt_layer_body(x_ref, modf_ref, modb_ref, gmix_ref, gffn_ref, win_ref, wout_ref, wup_ref, wdown_ref,
                       cos_ref, sin_ref, dmask_ref, qdec_ref, kdec_ref, cdec_ref, gn_ref, ln_ref,
                       ws_ref, bs_ref, convw_ref, convb_ref, gfin_ref,
                       y_ref, ret_ref, conv_ref,
                       h_ref, p_ref, mix_ref, mo_ref, x1_ref, h2_ref, a_ref, f_ref, mo2_ref,
                       *, tm, tiles_per_seq, n_tiles, final_norm):
    s = pl.program_id(0)
    front_on = s < n_tiles
    t_front = lax.rem(jnp.minimum(s, n_tiles - 1), tiles_per_seq)
    t_back = lax.rem(jnp.maximum(s - 1, 0), tiles_per_seq)

    @pl.when(s == 0)
    def _():
        x1_ref[...] = jnp.zeros_like(x1_ref)
        h2_ref[...] = jnp.zeros_like(h2_ref)

    @pl.when(jnp.logical_and(front_on, t_front == 0))
    def _():
        ret_ref[...] = jnp.zeros_like(ret_ref)

    @pl.when(t_back == 0)
    def _():
        conv_ref[...] = jnp.zeros_like(conv_ref)

    sh1, sc1, gt1 = modf_ref[0:1, :], modf_ref[1:2, :], modf_ref[2:3, :]
    sh2, sc2 = modf_ref[3:4, :], modf_ref[4:5, :]
    gt2 = modb_ref[5:6, :]
    row_blocks = [slice(r * ROW_BLOCK, (r + 1) * ROW_BLOCK) for r in range(tm // ROW_BLOCK)]

    y_ref[...] = x1_ref[...]
    a_ref[CONV_PAD - (CONV_W - 1):CONV_PAD, :] = conv_ref[...]
    a_ref[CONV_PAD:CONV_PAD + tm, :] = _dot(h2_ref[...], wup_ref[...])
    conv_ref[...] = a_ref[CONV_PAD + tm - (CONV_W - 1):CONV_PAD + tm, :]
    cb = 512
    for j in range(D_FF // cb):
        for r, rows in enumerate(row_blocks):
            halves = []
            for c0 in (j * cb, D_FF + j * cb):
                acc = convb_ref[:, c0:c0 + cb]
                for tap in range(CONV_W):
                    r0 = CONV_PAD - (CONV_W - 1) + tap + r * ROW_BLOCK
                    acc = acc + a_ref[r0:r0 + ROW_BLOCK, c0:c0 + cb] * convw_ref[tap:tap + 1, c0:c0 + cb]
                halves.append(acc)
            f_ref[rows, j * cb:(j + 1) * cb] = (_silu(halves[0]) * halves[1]).astype(BF16)
    mo2_ref[...] = _dot(f_ref[...], wdown_ref[...])
    for rows in row_blocks:
        x2 = y_ref[rows, :] + gt2 * mo2_ref[rows, :]
        if final_norm:
            x2 = _rms_unit(x2) * gfin_ref[...]
        y_ref[rows, :] = x2

    for rows in row_blocks:
        h = _rms_unit(x_ref[rows, :]) * gmix_ref[...]
        h_ref[rows, :] = (h * (1.0 + sc1) + sh1).astype(BF16)
    p_ref[...] = _dot(h_ref[...], win_ref[...])

    for c in range(tm // CHUNK):
        rows = slice(c * CHUNK, (c + 1) * CHUNK)
        cos_full, sin_signed = cos_ref[rows, :], sin_ref[rows, :]
        for hh in range(N_HEADS):
            cs = slice(hh * HEAD_W, (hh + 1) * HEAD_W)
            q = _rope(p_ref[rows, hh * HEAD_W:(hh + 1) * HEAD_W], cos_full, sin_signed)
            k = _rope(p_ref[rows, RET_W + hh * HEAD_W:RET_W + (hh + 1) * HEAD_W], cos_full, sin_signed)
            k = k * (HEAD_W ** -0.5)
            vb = p_ref[rows, 2 * RET_W + hh * HEAD_W:2 * RET_W + (hh + 1) * HEAD_W].astype(BF16)
            g = p_ref[rows, 3 * RET_W + hh * HEAD_W:3 * RET_W + (hh + 1) * HEAD_W]
            qb = q.astype(BF16)
            qdb = (q * qdec_ref[:, cs]).astype(BF16)
            kb = k.astype(BF16)
            kdb = (k * kdec_ref[:, cs]).astype(BF16)
            scores = lax.dot_general(qb, kb, NT_DIMS, preferred_element_type=F32)
            att = (scores * dmask_ref[hh]).astype(BF16)
            state = ret_ref[hh]
            o = _dot(att, vb) + _dot(qdb, state.astype(BF16))
            new_state = cdec_ref[hh] * state + lax.dot_general(kdb, vb, TN_DIMS, preferred_element_type=F32)
            ret_ref[hh] = jnp.where(front_on, new_state, state)
            on = _center_unit(o) * gn_ref[:, cs]
            mix_ref[rows, cs] = (_silu(g) * on).astype(BF16)
        for gg in range(N_HEADS):
            cs = slice(gg * HEAD_W, (gg + 1) * HEAD_W)
            u = _gelu_tanh(p_ref[rows, 4 * RET_W + gg * HEAD_W:4 * RET_W + (gg + 1) * HEAD_W])
            vv = _gelu_tanh(p_ref[rows, 4 * RET_W + GM_W + gg * HEAD_W:4 * RET_W + GM_W + (gg + 1) * HEAD_W])
            vn = _center_unit(vv) * ln_ref[:, cs]
            sg = _dot(ws_ref[gg], vn.astype(BF16)) + bs_ref[:, cs]
            mix_ref[rows, RET_W + gg * HEAD_W:RET_W + (gg + 1) * HEAD_W] = (u * sg).astype(BF16)

    mo_ref[...] = _dot(mix_ref[...], wout_ref[...])
    for rows in row_blocks:
        x1 = x_ref[rows, :] + gt1 * mo_ref[rows, :]
        x1_ref[rows, :] = x1
        h = _rms_unit(x1) * gffn_ref[...]
        h2_ref[rows, :] = (h * (1.0 + sc2) + sh2).astype(BF16)


def _const_spec(shape):
    zeros = (0,) * len(shape)
    return pl.BlockSpec(shape, lambda s: zeros, pipeline_mode=pl.Buffered(1))


def _prompt_layer(layer, x, mod, lw, tabs, g_final, final_norm):
    batch, seq, _ = x.shape

    def layer_spec(name):
        shape = lw[name].shape
        zeros = (0,) * (len(shape) - 1)
        return pl.BlockSpec((None,) + tuple(shape[1:]), lambda s: (layer,) + zeros, pipeline_mode=pl.Buffered(1))

    tm = PROMPT_TILE
    tiles_per_seq = seq // tm
    n_tiles = batch * tiles_per_seq
    body = functools.partial(_prompt_layer_body, tm=tm, tiles_per_seq=tiles_per_seq, n_tiles=n_tiles,
                             final_norm=final_norm)

    def front(s):
        i = jnp.minimum(s, n_tiles - 1)
        return i // tiles_per_seq, lax.rem(i, tiles_per_seq)

    def back(s):
        i = jnp.maximum(s - 1, 0)
        return i // tiles_per_seq, lax.rem(i, tiles_per_seq)

    in_specs = [
        pl.BlockSpec((None, tm, D_MODEL), lambda s: (*front(s), 0)),
        pl.BlockSpec((None, None, N_MOD, D_MODEL), lambda s: (layer, front(s)[0], 0, 0)),
        pl.BlockSpec((None, None, N_MOD, D_MODEL), lambda s: (layer, back(s)[0], 0, 0)),
        layer_spec("g_mix"), layer_spec("g_ffn"),
        layer_spec("w_in"), layer_spec("w_out"), layer_spec("w_up"), layer_spec("w_down"),
        pl.BlockSpec((tm, HEAD_W), lambda s: (front(s)[1], 0)),
        pl.BlockSpec((tm, HEAD_W), lambda s: (front(s)[1], 0)),
        _const_spec((N_HEADS, CHUNK, CHUNK)), _const_spec((CHUNK, RET_W)), _const_spec((CHUNK, RET_W)),
        _const_spec((N_HEADS, 1, HEAD_W)), layer_spec("gn"), layer_spec("ln"),
        layer_spec("ws_tril"), layer_spec("bs_full"),
        layer_spec("conv_w"), layer_spec("conv_b"), _const_spec((1, D_MODEL)),
    ]
    out_specs = [
        pl.BlockSpec((None, tm, D_MODEL), lambda s: (*back(s), 0)),
        pl.BlockSpec((None, N_HEADS, HEAD_W, HEAD_W), lambda s: (front(s)[0], 0, 0, 0)),
        pl.BlockSpec((None, CONV_W - 1, 2 * D_FF), lambda s: (back(s)[0], 0, 0)),
    ]
    out_shape = [
        jax.ShapeDtypeStruct((batch, seq, D_MODEL), F32),
        jax.ShapeDtypeStruct((batch, N_HEADS, HEAD_W, HEAD_W), F32),
        jax.ShapeDtypeStruct((batch, CONV_W - 1, 2 * D_FF), F32),
    ]
    scratch = [
        pltpu.VMEM((tm, D_MODEL), BF16),
        pltpu.VMEM((tm, IN_W), F32),
        pltpu.VMEM((tm, RET_W + GM_W), BF16),
        pltpu.VMEM((tm, D_MODEL), F32),
        pltpu.VMEM((tm, D_MODEL), F32),
        pltpu.VMEM((tm, D_MODEL), BF16),
        pltpu.VMEM((CONV_PAD + tm, 2 * D_FF), F32),
        pltpu.VMEM((tm, D_FF), BF16),
        pltpu.VMEM((tm, D_MODEL), F32),
    ]
    return pl.pallas_call(
        body,
        grid=(n_tiles + 1,),
        in_specs=in_specs,
        out_specs=out_specs,
        out_shape=out_shape,
        scratch_shapes=scratch,
        compiler_params=pltpu.CompilerParams(
            dimension_semantics=("arbitrary",), vmem_limit_bytes=V7X_VMEM_LIMIT_BYTES),
        name="prompt_layer",
    )(x, mod, mod, lw["g_mix"], lw["g_ffn"], lw["w_in"], lw["w_out"], lw["w_up"], lw["w_down"],
      tabs["cos"], tabs["sin"], tabs["dmask"], tabs["qdec"], tabs["kdec"], tabs["cdec"],
      lw["gn"], lw["ln"], lw["ws_tril"], lw["bs_full"], lw["conv_w"], lw["conv_b"], g_final)


def _sample_in_body(x_ref, mod_ref, gmix_ref, win_ref, cos_ref, sin_ref, ln_ref, ws0_ref, bs0_ref,
                    q_ref, k_ref, v_ref, sg_ref, ogm_ref, vn_ref):
    sh1, sc1 = mod_ref[:, 0:D_MODEL], mod_ref[:, D_MODEL:2 * D_MODEL]
    h = _rms_unit(x_ref[...]) * gmix_ref[...]
    h = (h * (1.0 + sc1) + sh1).astype(BF16)
    p = _dot(h, win_ref[...])
    for hh in range(N_HEADS):
        cs = slice(hh * HEAD_W, (hh + 1) * HEAD_W)
        q_ref[:, cs] = _rope(p[:, hh * HEAD_W:(hh + 1) * HEAD_W], cos_ref[...], sin_ref[...])
        k_ref[:, cs] = _rope(p[:, RET_W + hh * HEAD_W:RET_W + (hh + 1) * HEAD_W],
                             cos_ref[...], sin_ref[...]) * (HEAD_W ** -0.5)
        u = _gelu_tanh(p[:, 4 * RET_W + hh * HEAD_W:4 * RET_W + (hh + 1) * HEAD_W])
        vv = _gelu_tanh(p[:, 4 * RET_W + GM_W + hh * HEAD_W:4 * RET_W + GM_W + (hh + 1) * HEAD_W])
        vn = _center_unit(vv) * ln_ref[:, cs]
        vn_ref[:, cs] = vn
        ogm_ref[:, cs] = u * (ws0_ref[:, cs] * vn + bs0_ref[:, cs])
    v_ref[...] = p[:, 2 * RET_W:3 * RET_W]
    sg_ref[...] = _silu(p[:, 3 * RET_W:4 * RET_W])


def _sample_ret_body(q_ref, k_ref, v_ref, qt_ref, kt_ref, s_ref, gam_ref, *rest, sb, layer, first):
    o_ref, snew_ref = rest[-2:]
    if first:
        for other in range(snew_ref.shape[0]):
            if other != layer:
                snew_ref[other] = jnp.zeros(snew_ref.shape[1:], F32)
        snew_ref = snew_ref.at[layer]
    for hh in range(N_HEADS):
        cs = slice(hh * HEAD_W, (hh + 1) * HEAD_W)
        gam = gam_ref[:, cs]
        v = v_ref[:, cs]
        qt = qt_ref[hh]
        kt = kt_ref[hh]
        for s in range(sb):
            state = s_ref[s, hh]
            snew_ref[s, hh] = gam * state + kt[:, s:s + 1] * v[s:s + 1, :]
            o_ref[s:s + 1, cs] = gam * jnp.sum(qt[:, s:s + 1] * state, axis=0, keepdims=True)
        qk = jnp.sum(q_ref[:, cs] * k_ref[:, cs], axis=-1, keepdims=True)
        o_ref[:, cs] = o_ref[:, cs] + qk * v


def _sample_out_body(x_ref, mod_ref, o_ref, sg_ref, ogm_ref, gn_ref, wout_ref, gffn_ref, wup_ref,
                     cbuf_ref, convw_ref, convb_ref, wdown_ref, gfin_ref, acc_ref, y_ref, cnew_ref,
                     *, final_norm):
    del acc_ref
    gt1 = mod_ref[:, 2 * D_MODEL:3 * D_MODEL]
    sh2, sc2 = mod_ref[:, 3 * D_MODEL:4 * D_MODEL], mod_ref[:, 4 * D_MODEL:5 * D_MODEL]
    gt2 = mod_ref[:, 5 * D_MODEL:6 * D_MODEL]
    parts = []
    for hh in range(N_HEADS):
        cs = slice(hh * HEAD_W, (hh + 1) * HEAD_W)
        on = _center_unit(o_ref[:, cs]) * gn_ref[:, cs]
        parts.append((sg_ref[:, cs] * on).astype(BF16))
    parts.append(ogm_ref[...].astype(BF16))
    mix = _dot(jnp.concatenate(parts, axis=1), wout_ref[...])
    x1 = x_ref[...] + gt1 * mix
    h = _rms_unit(x1) * gffn_ref[...]
    h = (h * (1.0 + sc2) + sh2).astype(BF16)
    a = _dot(h, wup_ref[...])
    w = 2 * D_FF
    buf0, buf1 = cbuf_ref[:, 0:w], cbuf_ref[:, w:2 * w]
    conv = convb_ref[...] + buf0 * convw_ref[0:1, :] + buf1 * convw_ref[1:2, :] + a * convw_ref[2:3, :]
    cnew_ref[:, 0:w] = buf1
    cnew_ref[:, w:2 * w] = a
    f = (_silu(conv[:, 0:D_FF]) * conv[:, D_FF:w]).astype(BF16)
    x2 = x1 + gt2 * _dot(f, wdown_ref[...])
    if final_norm:
        x2 = _rms_unit(x2) * gfin_ref[...]
    y_ref[...] = x2


def _whole(shape):
    zeros = (0,) * len(shape)
    return pl.BlockSpec(shape, lambda *_: zeros)


def _layer_slab(shape, layer):
    zeros = (0,) * (len(shape) - 1)
    return pl.BlockSpec((None,) + tuple(shape[1:]), lambda *_: (layer,) + zeros)


def _sample_layer(layer, x, mod, state_ret, state_conv, ret_acc, conv_acc, lw, tabs, g_final, final_norm):
    n = x.shape[0]
    params = pltpu.CompilerParams(dimension_semantics=("arbitrary",), vmem_limit_bytes=V7X_VMEM_LIMIT_BYTES)
    act = jax.ShapeDtypeStruct((n, RET_W), F32)
    in_args = (x, mod, lw["g_mix"], lw["w_in"], tabs["cos_s"], tabs["sin_s"], lw["ln"], lw["ws0"], lw["bs0"])
    in_specs = [_whole(a.shape) for a in in_args]
    in_specs[1] = pl.BlockSpec((None, n, N_MOD * D_MODEL), lambda i: (layer, 0, 0))
    for i in (2, 3, 6, 7, 8):
        in_specs[i] = _layer_slab(in_args[i].shape, layer)
    q, k, v, sg, ogm, vn = pl.pallas_call(
        _sample_in_body,
        grid=(1,),
        in_specs=in_specs,
        out_specs=[_whole(act.shape)] * 6,
        out_shape=[act] * 6,
        compiler_params=params,
        name="sample_in",
    )(*in_args)

    sb = SAMPLE_SEQ_BLOCK
    row_spec = pl.BlockSpec((sb, RET_W), lambda i: (i, 0))
    state_spec = pl.BlockSpec((None, sb, N_HEADS, HEAD_W, HEAD_W), lambda i: (layer, i, 0, 0, 0))
    col_spec = pl.BlockSpec((None, N_HEADS, HEAD_W, sb), lambda i: (i, 0, 0, 0))
    carried = [] if ret_acc is None else [ret_acc]

    def head_columns(a):
        return a.reshape(n // sb, sb, N_HEADS, HEAD_W).transpose(0, 2, 3, 1)

    depth = state_ret.shape[0]
    first = ret_acc is None
    all_layers_spec = pl.BlockSpec((depth, sb, N_HEADS, HEAD_W, HEAD_W), lambda i: (0, i, 0, 0, 0))
    o, ret_acc = pl.pallas_call(
        functools.partial(_sample_ret_body, sb=sb, layer=layer, first=first),
        grid=(n // sb,),
        in_specs=[row_spec, row_spec, row_spec, col_spec, col_spec, state_spec, _whole((1, RET_W))]
        + [pl.BlockSpec(memory_space=pl.ANY)] * len(carried),
        out_specs=[row_spec, all_layers_spec if first else state_spec],
        out_shape=[act, jax.ShapeDtypeStruct(state_ret.shape, F32)],
        input_output_aliases={7: 1} if carried else {},
        compiler_params=params,
        name="sample_ret",
    )(q, k, v, head_columns(q), head_columns(k), state_ret, tabs["gamma_s"], *carried)

    cbuf = state_conv.reshape(depth, n, (CONV_W - 1) * 2 * D_FF)
    if conv_acc is None:
        conv_acc = jnp.zeros(cbuf.shape, F32)
    out_args = (x, mod, o, sg, ogm, lw["gn"], lw["w_out"], lw["g_ffn"], lw["w_up"], cbuf,
                lw["conv_w"], lw["conv_b"], lw["w_down"], g_final)
    in_specs = [_whole(a.shape) for a in out_args]
    in_specs[1] = pl.BlockSpec((None, n, N_MOD * D_MODEL), lambda i: (layer, 0, 0))
    for i in (5, 6, 7, 8, 9, 10, 11, 12):
        in_specs[i] = _layer_slab(out_args[i].shape, layer)

    y, conv_acc = pl.pallas_call(
        functools.partial(_sample_out_body, final_norm=final_norm),
        grid=(1,),
        in_specs=in_specs + [pl.BlockSpec(memory_space=pl.ANY)],
        out_specs=[_whole((n, D_MODEL)), _layer_slab(cbuf.shape, layer)],
        out_shape=[jax.ShapeDtypeStruct((n, D_MODEL), F32), jax.ShapeDtypeStruct(cbuf.shape, F32)],
        input_output_aliases={len(out_args): 1},
        compiler_params=params,
        name="sample_out",
    )(*out_args, conv_acc)
    return y, ret_acc, conv_acc, vn


def _rope_tables(pos):
    half = HEAD_W // 2
    freqs = np.exp(-math.log(ROPE_BASE) * np.arange(half, dtype=np.float64) / half)
    ang = np.asarray(pos, dtype=np.float64)[:, None] * freqs[None, :]
    cos, sin = np.cos(ang), np.sin(ang)
    return (np.concatenate([cos, cos], axis=-1).astype(np.float32),
            np.concatenate([-sin, sin], axis=-1).astype(np.float32))


def _decay_tables(chunk):
    lg = np.log1p(-np.exp2(-5.0 - np.arange(N_HEADS, dtype=np.float64)))
    i = np.arange(chunk, dtype=np.float64)
    diff = i[:, None] - i[None, :]
    dmask = np.where(diff[None] >= 0.0, np.exp(np.maximum(diff, 0.0)[None] * lg[:, None, None]), 0.0)
    q_dec = np.exp((i[:, None] + 1.0) * lg[None, :])
    k_dec = np.exp((chunk - 1.0 - i)[:, None] * lg[None, :])
    chunk_dec = np.exp(chunk * lg)
    return tuple(a.astype(np.float32) for a in (dmask, q_dec, k_dec, chunk_dec))


def _per_head_lanes(a):
    return a.repeat(HEAD_W, axis=-1)


def kernel(x_prompt, x_sample, state_ret, state_conv, c_prompt, c_sample, w_ada, b_ada, g_mix, w_in,
           ret_gn_gain, gmlp_ln_gain, w_s, b_s, w_out, g_ffn, w_up, conv_w, conv_b, w_down, g_final):
    depth = w_in.shape[0]
    batch, seq, _ = x_prompt.shape
    n_dec, dec_seq, _ = x_sample.shape
    assert dec_seq == 1 and seq % PROMPT_TILE == 0 and PROMPT_TILE % CHUNK == 0

    cos_p, sin_p = _rope_tables(np.arange(seq))
    cos_s, sin_s = _rope_tables(PAST_LEN + np.arange(dec_seq))
    dmask, q_dec, k_dec, chunk_dec = _decay_tables(CHUNK)
    _, _, _, gamma_s = _decay_tables(dec_seq)
    tabs = {
        "cos": cos_p, "sin": sin_p, "cos_s": cos_s, "sin_s": sin_s,
        "dmask": dmask, "qdec": _per_head_lanes(q_dec), "kdec": _per_head_lanes(k_dec),
        "cdec": np.broadcast_to(chunk_dec[:, None, None], (N_HEADS, 1, HEAD_W)),
        "gamma_s": _per_head_lanes(gamma_s[None, :]),
    }
    tabs = {name: jnp.asarray(a) for name, a in tabs.items()}

    mod = _adaln(jnp.concatenate([c_sample, c_prompt], axis=0), w_ada, b_ada)
    mod_p = mod[:, n_dec:].reshape(depth, batch, N_MOD, D_MODEL)
    tril = np.tril(np.ones((CHUNK, CHUNK), dtype=bool))
    lw = {
        "g_mix": g_mix.reshape(depth, 1, D_MODEL), "g_ffn": g_ffn.reshape(depth, 1, D_MODEL),
        "w_in": w_in.astype(BF16), "w_out": w_out.astype(BF16),
        "w_up": w_up.astype(BF16), "w_down": w_down.astype(BF16),
        "gn": ret_gn_gain.reshape(depth, 1, RET_W), "ln": gmlp_ln_gain.reshape(depth, 1, GM_W),
        "ws_tril": jnp.where(tril, w_s, 0.0).astype(BF16),
        "bs_full": _per_head_lanes(b_s.transpose(0, 2, 1)),
        "ws0": _per_head_lanes(w_s[:, None, :, 0, 0]), "bs0": _per_head_lanes(b_s[:, None, :, 0]),
        "conv_w": conv_w, "conv_b": conv_b.reshape(depth, 1, 2 * D_FF),
    }
    g_fin = g_final.reshape(1, D_MODEL)

    xp, xs = x_prompt, x_sample.reshape(n_dec, D_MODEL)
    ret_p, conv_p, v_s = [], [], []
    ret_s = conv_s = None
    for l in range(depth):
        last = l == depth - 1
        xp, sp, bp = _prompt_layer(l, xp, mod_p, lw, tabs, g_fin, last)
        xs, ret_s, conv_s, vs = _sample_layer(l, xs, mod, state_ret, state_conv, ret_s, conv_s, lw, tabs,
                                              g_fin, last)
        ret_p.append(sp); conv_p.append(bp)
        v_s.append(vs.reshape(n_dec, dec_seq, GM_W))
    return (xp, xs.reshape(n_dec, dec_seq, D_MODEL), jnp.stack(ret_p), jnp.stack(conv_p),
            ret_s, conv_s.reshape(state_conv.shape), jnp.stack(v_s))
```

```python
import functools
import math

import jax
import jax.numpy as jnp
import numpy as np
from jax import lax
from jax.experimental import pallas as pl
from jax.experimental.pallas import tpu as pltpu

D_MODEL = 1024
N_HEADS = 4
HEAD_W = 128
RET_W = N_HEADS * HEAD_W
GM_W = N_HEADS * HEAD_W
IN_W = 4 * RET_W + 2 * GM_W
D_FF = 2048
CONV_W = 3
CHUNK = 128
ROPE_BASE = 10000.0
PAST_LEN = 16384
EPS = 1e-6
N_MOD = 6

V7X_VMEM_LIMIT_BYTES = 56 * 1024 * 1024
PROMPT_TILE = 256
ROW_BLOCK = 32
CONV_PAD = 8
GATE_GROUP_W = 512
SAMPLE_SEQ_BLOCK = 8

F32 = jnp.float32
BF16 = jnp.bfloat16
NT_DIMS = (((1,), (1,)), ((), ()))
TN_DIMS = (((0,), (0,)), ((), ()))


def _silu(x):
    return x * (1.0 / (1.0 + jnp.exp(-x)))


def _gelu_tanh(x):
    c = math.sqrt(2.0 / math.pi)
    return 0.5 * x * (1.0 + jnp.tanh(c * (x + 0.044715 * (x * x * x))))


def _rms_unit(x):
    return x * lax.rsqrt(jnp.mean(x * x, axis=-1, keepdims=True) + EPS)


def _center_unit(x):
    xc = x - jnp.mean(x, axis=-1, keepdims=True)
    return xc * lax.rsqrt(jnp.mean(xc * xc, axis=-1, keepdims=True) + EPS)


def _rope(x, cos_full, sin_signed):
    return x * cos_full + pltpu.roll(x, HEAD_W // 2, 1) * sin_signed


def _dot(a, b):
    return jnp.dot(a, b, preferred_element_type=F32)


NORM_BUNDLES = 60
RESID_BUNDLES = 30
CONV_BUNDLES = 170
HEAD_BUNDLES = 160
GROUP_BUNDLES = 130


def _two_unit_order(pieces, orders):
    unit_free = {unit: 0 for unit in orders}
    head = {unit: 0 for unit in orders}
    finish = {}
    order = []
    while any(head[unit] < len(names) for unit, names in orders.items()):
        ready = []
        for unit, names in orders.items():
            if head[unit] < len(names):
                name = names[head[unit]]
                deps = pieces[name][2]
                if all(d in finish for d in deps):
                    ready.append((max([unit_free[unit]] + [finish[d] for d in deps]), unit, name))
        assert ready, "piece lists contradict the dependencies"
        start, unit, name = min(ready)
        finish[name] = unit_free[unit] = start + pieces[name][1]
        head[unit] += 1
        order.append(name)
    return order


def _adaln_body(c_ref, w_ref, b_ref, o_ref):
    c = _silu(c_ref[...]).astype(BF16)
    o_ref[...] = _dot(c, w_ref[...].astype(BF16)) + b_ref[...]


def _adaln(c_all, w_ada, b_ada):
    depth = w_ada.shape[0]
    rows = c_all.shape[0]
    n_out = w_ada.shape[2]
    bn = 1536
    return pl.pallas_call(
        _adaln_body,
        grid=(depth, n_out // bn),
        in_specs=[
            pl.BlockSpec((rows, D_MODEL), lambda l, j: (0, 0)),
            pl.BlockSpec((None, D_MODEL, bn), lambda l, j: (l, 0, j)),
            pl.BlockSpec((None, 1, bn), lambda l, j: (l, 0, j)),
        ],
        out_specs=pl.BlockSpec((None, rows, bn), lambda l, j: (l, 0, j)),
        out_shape=jax.ShapeDtypeStruct((depth, rows, n_out), F32),
        compiler_params=pltpu.CompilerParams(
            dimension_semantics=("arbitrary", "arbitrary"), vmem_limit_bytes=V7X_VMEM_LIMIT_BYTES),
        name="adaln",
    )(c_all, w_ada, b_ada.reshape(depth, 1, n_out))


def _prompt_layer_body(x_ref, modf_ref, modb_ref, gmix_ref, gffn_ref, win_ref, wout_ref, wup_ref, wdown_ref,
                       cos_ref, sin_ref, dmask_ref, qdec_ref, kdec_ref, cdec_ref, gn_ref, ln_ref,
                       ws_ref, bs_ref, convw_ref, convb_ref, gfin_ref,
                       y_ref, ret_ref, conv_ref,
                       h_ref, p_ref, mix_ref, mo_ref, x1_ref, h2_ref, f_ref, mo2_ref, *a_refs,
                       tm, tiles_per_seq, n_tiles, final_norm):
    s = pl.program_id(0)
    front_on = s < n_tiles
    t_front = lax.rem(jnp.minimum(s, n_tiles - 1), tiles_per_seq)
    t_back = lax.rem(jnp.maximum(s - 1, 0), tiles_per_seq)

    @pl.when(s == 0)
    def _():
        x1_ref[...] = jnp.zeros_like(x1_ref)
        h2_ref[...] = jnp.zeros_like(h2_ref)

    @pl.when(jnp.logical_and(front_on, t_front == 0))
    def _():
        ret_ref[...] = jnp.zeros_like(ret_ref)

    @pl.when(t_back == 0)
    def _():
        conv_ref[...] = jnp.zeros_like(conv_ref)

    sh1, sc1, gt1 = modf_ref[0:1, :], modf_ref[1:2, :], modf_ref[2:3, :]
    sh2, sc2 = modf_ref[3:4, :], modf_ref[4:5, :]
    gt2 = modb_ref[5:6, :]
    row_blocks = [slice(r * ROW_BLOCK, (r + 1) * ROW_BLOCK) for r in range(tm // ROW_BLOCK)]

    cb = GATE_GROUP_W
    tail = slice(CONV_PAD - (CONV_W - 1), CONV_PAD)

    pieces = {}
    n_rb = len(row_blocks)
    n_grp = D_FF // cb
    n_chunks = tm // CHUNK

    def matmul_bundles(k, n):
        return (tm // 16) * (k // 256) * (n // 256) * 8 // 2

    def up_group(j):
        for half in range(2):
            c0 = half * D_FF + j * cb
            a_refs[j][tail, half * cb:(half + 1) * cb] = conv_ref[:, c0:c0 + cb]
        a_refs[j][CONV_PAD:CONV_PAD + tm, :] = _dot(h2_ref[...], wup_ref[:, 2 * j * cb:2 * (j + 1) * cb])
        for half in range(2):
            c0 = half * D_FF + j * cb
            conv_ref[:, c0:c0 + cb] = a_refs[j][CONV_PAD + tm - (CONV_W - 1):CONV_PAD + tm,
                                                half * cb:(half + 1) * cb]

    def conv_rows(j, r):
        halves = []
        for half in range(2):
            c0 = half * D_FF + j * cb
            src = slice(half * cb, (half + 1) * cb)
            window = a_refs[j][r * ROW_BLOCK:CONV_PAD + (r + 1) * ROW_BLOCK, src]
            acc = convb_ref[:, c0:c0 + cb]
            for tap in range(CONV_W):
                back = CONV_W - 1 - tap
                rows_back = pltpu.roll(window, back, 0) if back else window
                acc = acc + rows_back[CONV_PAD:, :] * convw_ref[tap:tap + 1, c0:c0 + cb]
            halves.append(acc)
        f_ref[row_blocks[r], j * cb:(j + 1) * cb] = (_silu(halves[0]) * halves[1]).astype(BF16)

    def down_proj():
        mo2_ref[...] = _dot(f_ref[...], wdown_ref[...])

    def resid_out(r):
        rows = row_blocks[r]
        x2 = y_ref[rows, :] + gt2 * mo2_ref[rows, :]
        if final_norm:
            x2 = _rms_unit(x2) * gfin_ref[...]
        y_ref[rows, :] = x2

    for j in range(n_grp):
        pieces[f"up{j}"] = ("mxu", matmul_bundles(D_MODEL, 2 * cb), (), functools.partial(up_group, j))
        for r in range(n_rb):
            pieces[f"conv{j}.{r}"] = ("valu", CONV_BUNDLES, (f"up{j}",), functools.partial(conv_rows, j, r))
    all_conv = tuple(f"conv{j}.{r}" for j in range(n_grp) for r in range(n_rb))
    pieces["down"] = ("mxu", matmul_bundles(D_FF, D_MODEL), all_conv, down_proj)
    for r in range(n_rb):
        pieces[f"y{r}"] = ("valu", RESID_BUNDLES, ("down",), functools.partial(resid_out, r))

    def norm_rows(r):
        rows = row_blocks[r]
        h = _rms_unit(x_ref[rows, :]) * gmix_ref[...]
        h_ref[rows, :] = (h * (1.0 + sc1) + sh1).astype(BF16)

    def in_cols(c0, c1):
        p_ref[:, c0:c1] = _dot(h_ref[...], win_ref[:, c0:c1])

    all_norm = tuple(f"norm{r}" for r in range(n_rb))
    for r in range(n_rb):
        pieces[f"norm{r}"] = ("valu", NORM_BUNDLES, (), functools.partial(norm_rows, r))
    pieces["in_ret"] = ("mxu", matmul_bundles(D_MODEL, 4 * RET_W), all_norm, functools.partial(in_cols, 0, 4 * RET_W))
    pieces["in_gm"] = ("mxu", matmul_bundles(D_MODEL, 2 * GM_W), all_norm,
                       functools.partial(in_cols, 4 * RET_W, IN_W))

    def mix_head(c, hh):
        rows = slice(c * CHUNK, (c + 1) * CHUNK)
        cs = slice(hh * HEAD_W, (hh + 1) * HEAD_W)
        cos_full, sin_signed = cos_ref[rows, :], sin_ref[rows, :]
        q = _rope(p_ref[rows, hh * HEAD_W:(hh + 1) * HEAD_W], cos_full, sin_signed)
        k = _rope(p_ref[rows, RET_W + hh * HEAD_W:RET_W + (hh + 1) * HEAD_W], cos_full, sin_signed)
        k = k * (HEAD_W ** -0.5)
        vb = p_ref[rows, 2 * RET_W + hh * HEAD_W:2 * RET_W + (hh + 1) * HEAD_W].astype(BF16)
        g = p_ref[rows, 3 * RET_W + hh * HEAD_W:3 * RET_W + (hh + 1) * HEAD_W]
        qb = q.astype(BF16)
        qdb = (q * qdec_ref[:, cs]).astype(BF16)
        kb = k.astype(BF16)
        kdb = (k * kdec_ref[:, cs]).astype(BF16)
        scores = lax.dot_general(qb, kb, NT_DIMS, preferred_element_type=F32)
        att = (scores * dmask_ref[hh]).astype(BF16)
        state = ret_ref[hh]
        o = _dot(att, vb) + _dot(qdb, state.astype(BF16))
        new_state = cdec_ref[hh] * state + lax.dot_general(kdb, vb, TN_DIMS, preferred_element_type=F32)
        ret_ref[hh] = jnp.where(front_on, new_state, state)
        on = _center_unit(o) * gn_ref[:, cs]
        mix_ref[rows, cs] = (_silu(g) * on).astype(BF16)

    def mix_group(c, gg):
        rows = slice(c * CHUNK, (c + 1) * CHUNK)
        cs = slice(gg * HEAD_W, (gg + 1) * HEAD_W)
        u = _gelu_tanh(p_ref[rows, 4 * RET_W + gg * HEAD_W:4 * RET_W + (gg + 1) * HEAD_W])
        vv = _gelu_tanh(p_ref[rows, 4 * RET_W + GM_W + gg * HEAD_W:4 * RET_W + GM_W + (gg + 1) * HEAD_W])
        vn = _center_unit(vv) * ln_ref[:, cs]
        sg = _dot(ws_ref[gg], vn.astype(BF16)) + bs_ref[:, cs]
        mix_ref[rows, RET_W + gg * HEAD_W:RET_W + (gg + 1) * HEAD_W] = (u * sg).astype(BF16)

    def out_proj():
        mo_ref[...] = _dot(mix_ref[...], wout_ref[...])

    def resid_mid(r):
        rows = row_blocks[r]
        x1 = x_ref[rows, :] + gt1 * mo_ref[rows, :]
        x1_ref[rows, :] = x1
        h = _rms_unit(x1) * gffn_ref[...]
        h2_ref[rows, :] = (h * (1.0 + sc2) + sh2).astype(BF16)

    all_mix = []
    for c in range(n_chunks):
        for hh in range(N_HEADS):
            pieces[f"head{c}.{hh}"] = ("valu", HEAD_BUNDLES, ("in_ret",), functools.partial(mix_head, c, hh))
            pieces[f"group{c}.{hh}"] = ("valu", GROUP_BUNDLES, ("in_gm",), functools.partial(mix_group, c, hh))
            all_mix += [f"head{c}.{hh}", f"group{c}.{hh}"]
    pieces["out"] = ("mxu", matmul_bundles(RET_W + GM_W, D_MODEL), tuple(all_mix), out_proj)
    all_up = tuple(f"up{j}" for j in range(n_grp))
    for r in range(n_rb):
        pieces[f"x1{r}"] = ("valu", RESID_BUNDLES + NORM_BUNDLES, ("out",) + all_up, functools.partial(resid_mid, r))

    orders = {
        "mxu": list(all_up) + ["in_ret", "in_gm", "down", "out"],
        "valu": list(all_norm) + list(all_conv)
        + [f"{kind}{c}.{i}" for c in range(n_chunks) for kind in ("head", "group") for i in range(N_HEADS)]
        + [f"y{r}" for r in range(n_rb)] + [f"x1{r}" for r in range(n_rb)],
    }
    y_ref[...] = x1_ref[...]
    for name in _two_unit_order(pieces, orders):
        pieces[name][3]()


def _const_spec(shape):
    zeros = (0,) * len(shape)
    return pl.BlockSpec(shape, lambda s: zeros, pipeline_mode=pl.Buffered(1))


def _prompt_layer(layer, x, mod, lw, tabs, g_final, final_norm):
    batch, seq, _ = x.shape

    def layer_spec(name):
        shape = lw[name].shape
        zeros = (0,) * (len(shape) - 1)
        return pl.BlockSpec((None,) + tuple(shape[1:]), lambda s: (layer,) + zeros, pipeline_mode=pl.Buffered(1))

    tm = PROMPT_TILE
    tiles_per_seq = seq // tm
    n_tiles = batch * tiles_per_seq
    body = functools.partial(_prompt_layer_body, tm=tm, tiles_per_seq=tiles_per_seq, n_tiles=n_tiles,
                             final_norm=final_norm)

    def front(s):
        i = jnp.minimum(s, n_tiles - 1)
        return i // tiles_per_seq, lax.rem(i, tiles_per_seq)

    def back(s):
        i = jnp.maximum(s - 1, 0)
        return i // tiles_per_seq, lax.rem(i, tiles_per_seq)

    in_specs = [
        pl.BlockSpec((None, tm, D_MODEL), lambda s: (*front(s), 0)),
        pl.BlockSpec((None, None, N_MOD, D_MODEL), lambda s: (layer, front(s)[0], 0, 0)),
        pl.BlockSpec((None, None, N_MOD, D_MODEL), lambda s: (layer, back(s)[0], 0, 0)),
        layer_spec("g_mix"), layer_spec("g_ffn"),
        layer_spec("w_in"), layer_spec("w_out"), layer_spec("w_up"), layer_spec("w_down"),
        pl.BlockSpec((tm, HEAD_W), lambda s: (front(s)[1], 0)),
        pl.BlockSpec((tm, HEAD_W), lambda s: (front(s)[1], 0)),
        _const_spec((N_HEADS, CHUNK, CHUNK)), _const_spec((CHUNK, RET_W)), _const_spec((CHUNK, RET_W)),
        _const_spec((N_HEADS, 1, HEAD_W)), layer_spec("gn"), layer_spec("ln"),
        layer_spec("ws_tril"), layer_spec("bs_full"),
        layer_spec("conv_w"), layer_spec("conv_b"), _const_spec((1, D_MODEL)),
    ]
    out_specs = [
        pl.BlockSpec((None, tm, D_MODEL), lambda s: (*back(s), 0)),
        pl.BlockSpec((None, N_HEADS, HEAD_W, HEAD_W), lambda s: (front(s)[0], 0, 0, 0)),
        pl.BlockSpec((None, CONV_W - 1, 2 * D_FF), lambda s: (back(s)[0], 0, 0)),
    ]
    out_shape = [
        jax.ShapeDtypeStruct((batch, seq, D_MODEL), F32),
        jax.ShapeDtypeStruct((batch, N_HEADS, HEAD_W, HEAD_W), F32),
        jax.ShapeDtypeStruct((batch, CONV_W - 1, 2 * D_FF), F32),
    ]
    scratch = [
        pltpu.VMEM((tm, D_MODEL), BF16),
        pltpu.VMEM((tm, IN_W), F32),
        pltpu.VMEM((tm, RET_W + GM_W), BF16),
        pltpu.VMEM((tm, D_MODEL), F32),
        pltpu.VMEM((tm, D_MODEL), F32),
        pltpu.VMEM((tm, D_MODEL), BF16),
        pltpu.VMEM((tm, D_FF), BF16),
        pltpu.VMEM((tm, D_MODEL), F32),
    ] + [pltpu.VMEM((CONV_PAD + tm, 2 * 512), F32)] * (D_FF // 512)
    return pl.pallas_call(
        body,
        grid=(n_tiles + 1,),
        in_specs=in_specs,
        out_specs=out_specs,
        out_shape=out_shape,
        scratch_shapes=scratch,
        compiler_params=pltpu.CompilerParams(
            dimension_semantics=("arbitrary",), vmem_limit_bytes=V7X_VMEM_LIMIT_BYTES),
        name="prompt_layer",
    )(x, mod, mod, lw["g_mix"], lw["g_ffn"], lw["w_in"], lw["w_out"], lw["w_up"], lw["w_down"],
      tabs["cos"], tabs["sin"], tabs["dmask"], tabs["qdec"], tabs["kdec"], tabs["cdec"],
      lw["gn"], lw["ln"], lw["ws_tril"], lw["bs_full"], lw["conv_w"], lw["conv_b"], g_final)


def _sample_in_body(x_ref, mod_ref, gmix_ref, win_ref, cos_ref, sin_ref, ln_ref, ws0_ref, bs0_ref,
                    q_ref, k_ref, v_ref, sg_ref, ogm_ref, vn_ref):
    sh1, sc1 = mod_ref[:, 0:D_MODEL], mod_ref[:, D_MODEL:2 * D_MODEL]
    h = _rms_unit(x_ref[...]) * gmix_ref[...]
    h = (h * (1.0 + sc1) + sh1).astype(BF16)
    p = _dot(h, win_ref[...])
    for hh in range(N_HEADS):
        cs = slice(hh * HEAD_W, (hh + 1) * HEAD_W)
        q_ref[:, cs] = _rope(p[:, hh * HEAD_W:(hh + 1) * HEAD_W], cos_ref[...], sin_ref[...])
        k_ref[:, cs] = _rope(p[:, RET_W + hh * HEAD_W:RET_W + (hh + 1) * HEAD_W],
                             cos_ref[...], sin_ref[...]) * (HEAD_W ** -0.5)
        u = _gelu_tanh(p[:, 4 * RET_W + hh * HEAD_W:4 * RET_W + (hh + 1) * HEAD_W])
        vv = _gelu_tanh(p[:, 4 * RET_W + GM_W + hh * HEAD_W:4 * RET_W + GM_W + (hh + 1) * HEAD_W])
        vn = _center_unit(vv) * ln_ref[:, cs]
        vn_ref[:, cs] = vn
        ogm_ref[:, cs] = u * (ws0_ref[:, cs] * vn + bs0_ref[:, cs])
    v_ref[...] = p[:, 2 * RET_W:3 * RET_W]
    sg_ref[...] = _silu(p[:, 3 * RET_W:4 * RET_W])


def _sample_ret_body(q_ref, k_ref, v_ref, qt_ref, kt_ref, s_ref, gam_ref, *rest, sb, layer, first):
    o_ref, snew_ref = rest[-2:]
    if first:
        for other in range(snew_ref.shape[0]):
            if other != layer:
                snew_ref[other] = jnp.zeros(snew_ref.shape[1:], F32)
        snew_ref = snew_ref.at[layer]
    for hh in range(N_HEADS):
        cs = slice(hh * HEAD_W, (hh + 1) * HEAD_W)
        gam = gam_ref[:, cs]
        v = v_ref[:, cs]
        qt = qt_ref[hh]
        kt = kt_ref[hh]
        for s in range(sb):
            state = s_ref[s, hh]
            snew_ref[s, hh] = gam * state + kt[:, s:s + 1] * v[s:s + 1, :]
            o_ref[s:s + 1, cs] = gam * jnp.sum(qt[:, s:s + 1] * state, axis=0, keepdims=True)
        qk = jnp.sum(q_ref[:, cs] * k_ref[:, cs], axis=-1, keepdims=True)
        o_ref[:, cs] = o_ref[:, cs] + qk * v


def _sample_out_body(x_ref, mod_ref, o_ref, sg_ref, ogm_ref, gn_ref, wout_ref, gffn_ref, wup_ref,
                     cbuf_ref, convw_ref, convb_ref, wdown_ref, gfin_ref, acc_ref, y_ref, cnew_ref,
                     *, final_norm):
    del acc_ref
    gt1 = mod_ref[:, 2 * D_MODEL:3 * D_MODEL]
    sh2, sc2 = mod_ref[:, 3 * D_MODEL:4 * D_MODEL], mod_ref[:, 4 * D_MODEL:5 * D_MODEL]
    gt2 = mod_ref[:, 5 * D_MODEL:6 * D_MODEL]
    parts = []
    for hh in range(N_HEADS):
        cs = slice(hh * HEAD_W, (hh + 1) * HEAD_W)
        on = _center_unit(o_ref[:, cs]) * gn_ref[:, cs]
        parts.append((sg_ref[:, cs] * on).astype(BF16))
    parts.append(ogm_ref[...].astype(BF16))
    mix = _dot(jnp.concatenate(parts, axis=1), wout_ref[...])
    x1 = x_ref[...] + gt1 * mix
    h = _rms_unit(x1) * gffn_ref[...]
    h = (h * (1.0 + sc2) + sh2).astype(BF16)
    a = _dot(h, wup_ref[...])
    w = 2 * D_FF
    gw = GATE_GROUP_W
    f_parts = []
    for j in range(D_FF // gw):
        halves = []
        for half in range(2):
            c0 = half * D_FF + j * gw
            a_blk = a[:, (2 * j + half) * gw:(2 * j + half + 1) * gw]
            buf0, buf1 = cbuf_ref[:, c0:c0 + gw], cbuf_ref[:, w + c0:w + c0 + gw]
            halves.append(convb_ref[:, c0:c0 + gw] + buf0 * convw_ref[0:1, c0:c0 + gw]
                          + buf1 * convw_ref[1:2, c0:c0 + gw] + a_blk * convw_ref[2:3, c0:c0 + gw])
            cnew_ref[:, c0:c0 + gw] = buf1
            cnew_ref[:, w + c0:w + c0 + gw] = a_blk
        f_parts.append((_silu(halves[0]) * halves[1]).astype(BF16))
    f = jnp.concatenate(f_parts, axis=1)
    x2 = x1 + gt2 * _dot(f, wdown_ref[...])
    if final_norm:
        x2 = _rms_unit(x2) * gfin_ref[...]
    y_ref[...] = x2


def _whole(shape):
    zeros = (0,) * len(shape)
    return pl.BlockSpec(shape, lambda *_: zeros)


def _layer_slab(shape, layer):
    zeros = (0,) * (len(shape) - 1)
    return pl.BlockSpec((None,) + tuple(shape[1:]), lambda *_: (layer,) + zeros)


def _sample_layer(layer, x, mod, state_ret, state_conv, ret_acc, conv_acc, lw, tabs, g_final, final_norm):
    n = x.shape[0]
    params = pltpu.CompilerParams(dimension_semantics=("arbitrary",), vmem_limit_bytes=V7X_VMEM_LIMIT_BYTES)
    act = jax.ShapeDtypeStruct((n, RET_W), F32)
    in_args = (x, mod, lw["g_mix"], lw["w_in"], tabs["cos_s"], tabs["sin_s"], lw["ln"], lw["ws0"], lw["bs0"])
    in_specs = [_whole(a.shape) for a in in_args]
    in_specs[1] = pl.BlockSpec((None, n, N_MOD * D_MODEL), lambda i: (layer, 0, 0))
    for i in (2, 3, 6, 7, 8):
        in_specs[i] = _layer_slab(in_args[i].shape, layer)
    q, k, v, sg, ogm, vn = pl.pallas_call(
        _sample_in_body,
        grid=(1,),
        in_specs=in_specs,
        out_specs=[_whole(act.shape)] * 6,
        out_shape=[act] * 6,
        compiler_params=params,
        name="sample_in",
    )(*in_args)

    sb = SAMPLE_SEQ_BLOCK
    row_spec = pl.BlockSpec((sb, RET_W), lambda i: (i, 0))
    state_spec = pl.BlockSpec((None, sb, N_HEADS, HEAD_W, HEAD_W), lambda i: (layer, i, 0, 0, 0))
    col_spec = pl.BlockSpec((None, N_HEADS, HEAD_W, sb), lambda i: (i, 0, 0, 0))
    carried = [] if ret_acc is None else [ret_acc]

    def head_columns(a):
        return a.reshape(n // sb, sb, N_HEADS, HEAD_W).transpose(0, 2, 3, 1)

    depth = state_ret.shape[0]
    first = ret_acc is None
    all_layers_spec = pl.BlockSpec((depth, sb, N_HEADS, HEAD_W, HEAD_W), lambda i: (0, i, 0, 0, 0))
    o, ret_acc = pl.pallas_call(
        functools.partial(_sample_ret_body, sb=sb, layer=layer, first=first),
        grid=(n // sb,),
        in_specs=[row_spec, row_spec, row_spec, col_spec, col_spec, state_spec, _whole((1, RET_W))]
        + [pl.BlockSpec(memory_space=pl.ANY)] * len(carried),
        out_specs=[row_spec, all_layers_spec if first else state_spec],
        out_shape=[act, jax.ShapeDtypeStruct(state_ret.shape, F32)],
        input_output_aliases={7: 1} if carried else {},
        compiler_params=params,
        name="sample_ret",
    )(q, k, v, head_columns(q), head_columns(k), state_ret, tabs["gamma_s"], *carried)

    cbuf = state_conv.reshape(depth, n, (CONV_W - 1) * 2 * D_FF)
    if conv_acc is None:
        conv_acc = jnp.zeros(cbuf.shape, F32)
    out_args = (x, mod, o, sg, ogm, lw["gn"], lw["w_out"], lw["g_ffn"], lw["w_up"], cbuf,
                lw["conv_w"], lw["conv_b"], lw["w_down"], g_final)
    in_specs = [_whole(a.shape) for a in out_args]
    in_specs[1] = pl.BlockSpec((None, n, N_MOD * D_MODEL), lambda i: (layer, 0, 0))
    for i in (5, 6, 7, 8, 9, 10, 11, 12):
        in_specs[i] = _layer_slab(out_args[i].shape, layer)

    y, conv_acc = pl.pallas_call(
        functools.partial(_sample_out_body, final_norm=final_norm),
        grid=(1,),
        in_specs=in_specs + [pl.BlockSpec(memory_space=pl.ANY)],
        out_specs=[_whole((n, D_MODEL)), _layer_slab(cbuf.shape, layer)],
        out_shape=[jax.ShapeDtypeStruct((n, D_MODEL), F32), jax.ShapeDtypeStruct(cbuf.shape, F32)],
        input_output_aliases={len(out_args): 1},
        compiler_params=params,
        name="sample_out",
    )(*out_args, conv_acc)
    return y, ret_acc, conv_acc, vn


def _rope_tables(pos):
    half = HEAD_W // 2
    freqs = np.exp(-math.log(ROPE_BASE) * np.arange(half, dtype=np.float64) / half)
    ang = np.asarray(pos, dtype=np.float64)[:, None] * freqs[None, :]
    cos, sin = np.cos(ang), np.sin(ang)
    return (np.concatenate([cos, cos], axis=-1).astype(np.float32),
            np.concatenate([-sin, sin], axis=-1).astype(np.float32))


def _decay_tables(chunk):
    lg = np.log1p(-np.exp2(-5.0 - np.arange(N_HEADS, dtype=np.float64)))
    i = np.arange(chunk, dtype=np.float64)
    diff = i[:, None] - i[None, :]
    dmask = np.where(diff[None] >= 0.0, np.exp(np.maximum(diff, 0.0)[None] * lg[:, None, None]), 0.0)
    q_dec = np.exp((i[:, None] + 1.0) * lg[None, :])
    k_dec = np.exp((chunk - 1.0 - i)[:, None] * lg[None, :])
    chunk_dec = np.exp(chunk * lg)
    return tuple(a.astype(np.float32) for a in (dmask, q_dec, k_dec, chunk_dec))


def _group_gate_columns(w):
    lead = w.shape[:-1]
    n_grp = D_FF // GATE_GROUP_W
    w = w.reshape(lead + (2, n_grp, GATE_GROUP_W))
    return jnp.swapaxes(w, -3, -2).reshape(lead + (2 * D_FF,))


def _per_head_lanes(a):
    return a.repeat(HEAD_W, axis=-1)


def kernel(x_prompt, x_sample, state_ret, state_conv, c_prompt, c_sample, w_ada, b_ada, g_mix, w_in,
           ret_gn_gain, gmlp_ln_gain, w_s, b_s, w_out, g_ffn, w_up, conv_w, conv_b, w_down, g_final):
    depth = w_in.shape[0]
    batch, seq, _ = x_prompt.shape
    n_dec, dec_seq, _ = x_sample.shape
    assert dec_seq == 1 and seq % PROMPT_TILE == 0 and PROMPT_TILE % CHUNK == 0

    cos_p, sin_p = _rope_tables(np.arange(seq))
    cos_s, sin_s = _rope_tables(PAST_LEN + np.arange(dec_seq))
    dmask, q_dec, k_dec, chunk_dec = _decay_tables(CHUNK)
    _, _, _, gamma_s = _decay_tables(dec_seq)
    tabs = {
        "cos": cos_p, "sin": sin_p, "cos_s": cos_s, "sin_s": sin_s,
        "dmask": dmask, "qdec": _per_head_lanes(q_dec), "kdec": _per_head_lanes(k_dec),
        "cdec": np.broadcast_to(chunk_dec[:, None, None], (N_HEADS, 1, HEAD_W)),
        "gamma_s": _per_head_lanes(gamma_s[None, :]),
    }
    tabs = {name: jnp.asarray(a) for name, a in tabs.items()}

    mod = _adaln(jnp.concatenate([c_sample, c_prompt], axis=0), w_ada, b_ada)
    mod_p = mod[:, n_dec:].reshape(depth, batch, N_MOD, D_MODEL)
    tril = np.tril(np.ones((CHUNK, CHUNK), dtype=bool))
    lw = {
        "g_mix": g_mix.reshape(depth, 1, D_MODEL), "g_ffn": g_ffn.reshape(depth, 1, D_MODEL),
        "w_in": w_in.astype(BF16), "w_out": w_out.astype(BF16),
        "w_up": _group_gate_columns(w_up).astype(BF16), "w_down": w_down.astype(BF16),
        "gn": ret_gn_gain.reshape(depth, 1, RET_W), "ln": gmlp_ln_gain.reshape(depth, 1, GM_W),
        "ws_tril": jnp.where(tril, w_s, 0.0).astype(BF16),
        "bs_full": _per_head_lanes(b_s.transpose(0, 2, 1)),
        "ws0": _per_head_lanes(w_s[:, None, :, 0, 0]), "bs0": _per_head_lanes(b_s[:, None, :, 0]),
        "conv_w": conv_w, "conv_b": conv_b.reshape(depth, 1, 2 * D_FF),
    }
    g_fin = g_final.reshape(1, D_MODEL)

    xp, xs = x_prompt, x_sample.reshape(n_dec, D_MODEL)
    ret_p, conv_p, v_s = [], [], []
    ret_s = conv_s = None
    for l in range(depth):
        last = l == depth - 1
        xp, sp, bp = _prompt_layer(l, xp, mod_p, lw, tabs, g_fin, last)
        xs, ret_s, conv_s, vs = _sample_layer(l, xs, mod, state_ret, state_conv, ret_s, conv_s, lw, tabs,
                                              g_fin, last)
        ret_p.append(sp); conv_p.append(bp)
        v_s.append(vs.reshape(n_dec, dec_seq, GM_W))
    return (xp, xs.reshape(n_dec, dec_seq, D_MODEL), jnp.stack(ret_p), jnp.stack(conv_p),
            ret_s, conv_s.reshape(state_conv.shape), jnp.stack(v_s))
```

```python
import functools
import math

import jax
import jax.numpy as jnp
import numpy as np
from jax import lax
from jax.experimental import pallas as pl
from jax.experimental.pallas import tpu as pltpu

D_MODEL = 1024
N_HEADS = 4
HEAD_W = 128
RET_W = N_HEADS * HEAD_W
GM_W = N_HEADS * HEAD_W
IN_W = 4 * RET_W + 2 * GM_W
D_FF = 2048
CONV_W = 3
CHUNK = 128
ROPE_BASE = 10000.0
PAST_LEN = 16384
EPS = 1e-6
N_MOD = 6

V7X_VMEM_LIMIT_BYTES = 56 * 1024 * 1024
PROMPT_TILE = 256
ROW_BLOCK = 32
CONV_PAD = 8
GATE_GROUP_W = 512
SAMPLE_SEQ_BLOCK = 8

F32 = jnp.float32
BF16 = jnp.bfloat16
NT_DIMS = (((1,), (1,)), ((), ()))
TN_DIMS = (((0,), (0,)), ((), ()))


def _silu(x):
    return x * (1.0 / (1.0 + jnp.exp(-x)))


def _gelu_tanh(x):
    c = math.sqrt(2.0 / math.pi)
    return 0.5 * x * (1.0 + jnp.tanh(c * (x + 0.044715 * (x * x * x))))


def _rms_unit(x):
    return x * lax.rsqrt(jnp.mean(x * x, axis=-1, keepdims=True) + EPS)


def _center_unit(x):
    xc = x - jnp.mean(x, axis=-1, keepdims=True)
    return xc * lax.rsqrt(jnp.mean(xc * xc, axis=-1, keepdims=True) + EPS)


def _rope(x, cos_full, sin_signed):
    return x * cos_full + pltpu.roll(x, HEAD_W // 2, 1) * sin_signed


def _dot(a, b):
    return jnp.dot(a, b, preferred_element_type=F32)


NORM_BUNDLES = 60
RESID_BUNDLES = 30
CONV_BUNDLES = 170
HEAD_BUNDLES = 160
GROUP_BUNDLES = 130


def _two_unit_order(pieces, orders):
    unit_free = {unit: 0 for unit in orders}
    head = {unit: 0 for unit in orders}
    finish = {}
    order = []
    while any(head[unit] < len(names) for unit, names in orders.items()):
        ready = []
        for unit, names in orders.items():
            if head[unit] < len(names):
                name = names[head[unit]]
                deps = pieces[name][2]
                if all(d in finish for d in deps):
                    ready.append((max([unit_free[unit]] + [finish[d] for d in deps]), unit, name))
        assert ready, "piece lists contradict the dependencies"
        start, unit, name = min(ready)
        finish[name] = unit_free[unit] = start + pieces[name][1]
        head[unit] += 1
        order.append(name)
    return order


def _cast_body(w_ref, o_ref):
    o_ref[...] = w_ref[...].astype(BF16)


def _to_bf16(w, group_gate_columns=False):
    depth, k, n = w.shape
    bn = GATE_GROUP_W
    n_grp = n // (2 * bn)

    def src_block(c):
        return (c % 2) * n_grp + c // 2 if group_gate_columns else c

    return pl.pallas_call(
        _cast_body,
        grid=(depth, n // bn),
        in_specs=[pl.BlockSpec((None, k, bn), lambda l, c: (l, 0, src_block(c)))],
        out_specs=pl.BlockSpec((None, k, bn), lambda l, c: (l, 0, c)),
        out_shape=jax.ShapeDtypeStruct(w.shape, BF16),
        compiler_params=pltpu.CompilerParams(
            dimension_semantics=("arbitrary", "arbitrary"), vmem_limit_bytes=V7X_VMEM_LIMIT_BYTES),
        name="to_bf16",
    )(w)


def _adaln_body(c_ref, w_ref, b_ref, o_ref):
    c = _silu(c_ref[...]).astype(BF16)
    o_ref[...] = _dot(c, w_ref[...].astype(BF16)) + b_ref[...]


def _adaln(c_all, w_ada, b_ada):
    depth = w_ada.shape[0]
    rows = c_all.shape[0]
    n_out = w_ada.shape[2]
    bn = 1536
    return pl.pallas_call(
        _adaln_body,
        grid=(depth, n_out // bn),
        in_specs=[
            pl.BlockSpec((rows, D_MODEL), lambda l, j: (0, 0)),
            pl.BlockSpec((None, D_MODEL, bn), lambda l, j: (l, 0, j)),
            pl.BlockSpec((None, 1, bn), lambda l, j: (l, 0, j)),
        ],
        out_specs=pl.BlockSpec((None, rows, bn), lambda l, j: (l, 0, j)),
        out_shape=jax.ShapeDtypeStruct((depth, rows, n_out), F32),
        compiler_params=pltpu.CompilerParams(
            dimension_semantics=("arbitrary", "arbitrary"), vmem_limit_bytes=V7X_VMEM_LIMIT_BYTES),
        name="adaln",
    )(c_all, w_ada, b_ada.reshape(depth, 1, n_out))


def _prompt_layer_body(x_ref, modf_ref, modb_ref, gmix_ref, gffn_ref, win_ref, wout_ref, wup_ref, wdown_ref,
                       cos_ref, sin_ref, dmask_ref, qdec_ref, kdec_ref, cdec_ref, gn_ref, ln_ref,
                       ws_ref, bs_ref, convw_ref, convb_ref, gfin_ref,
                       y_ref, ret_ref, conv_ref,
                       h_ref, p_ref, mix_ref, mo_ref, x1_ref, h2_ref, f_ref, mo2_ref, *a_refs,
                       tm, tiles_per_seq, n_tiles, final_norm):
    s = pl.program_id(0)
    front_on = s < n_tiles
    t_front = lax.rem(jnp.minimum(s, n_tiles - 1), tiles_per_seq)
    t_back = lax.rem(jnp.maximum(s - 1, 0), tiles_per_seq)

    @pl.when(s == 0)
    def _():
        x1_ref[...] = jnp.zeros_like(x1_ref)
        h2_ref[...] = jnp.zeros_like(h2_ref)

    @pl.when(jnp.logical_and(front_on, t_front == 0))
    def _():
        ret_ref[...] = jnp.zeros_like(ret_ref)

    @pl.when(t_back == 0)
    def _():
        conv_ref[...] = jnp.zeros_like(conv_ref)

    sh1, sc1, gt1 = modf_ref[0:1, :], modf_ref[1:2, :], modf_ref[2:3, :]
    sh2, sc2 = modf_ref[3:4, :], modf_ref[4:5, :]
    gt2 = modb_ref[5:6, :]
    row_blocks = [slice(r * ROW_BLOCK, (r + 1) * ROW_BLOCK) for r in range(tm // ROW_BLOCK)]

    cb = GATE_GROUP_W
    tail = slice(CONV_PAD - (CONV_W - 1), CONV_PAD)

    pieces = {}
    n_rb = len(row_blocks)
    n_grp = D_FF // cb
    n_chunks = tm // CHUNK

    def matmul_bundles(k, n):
        return (tm // 16) * (k // 256) * (n // 256) * 8 // 2

    def up_group(j):
        for half in range(2):
            c0 = half * D_FF + j * cb
            a_refs[j][tail, half * cb:(half + 1) * cb] = conv_ref[:, c0:c0 + cb]
        a_refs[j][CONV_PAD:CONV_PAD + tm, :] = _dot(h2_ref[...], wup_ref[:, 2 * j * cb:2 * (j + 1) * cb])
        for half in range(2):
            c0 = half * D_FF + j * cb
            conv_ref[:, c0:c0 + cb] = a_refs[j][CONV_PAD + tm - (CONV_W - 1):CONV_PAD + tm,
                                                half * cb:(half + 1) * cb]

    def conv_rows(j, r):
        halves = []
        for half in range(2):
            c0 = half * D_FF + j * cb
            src = slice(half * cb, (half + 1) * cb)
            window = a_refs[j][r * ROW_BLOCK:CONV_PAD + (r + 1) * ROW_BLOCK, src]
            acc = convb_ref[:, c0:c0 + cb]
            for tap in range(CONV_W):
                back = CONV_W - 1 - tap
                rows_back = pltpu.roll(window, back, 0) if back else window
                acc = acc + rows_back[CONV_PAD:, :] * convw_ref[tap:tap + 1, c0:c0 + cb]
            halves.append(acc)
        f_ref[row_blocks[r], j * cb:(j + 1) * cb] = (_silu(halves[0]) * halves[1]).astype(BF16)

    def down_proj():
        mo2_ref[...] = _dot(f_ref[...], wdown_ref[...])

    def resid_out(r):
        rows = row_blocks[r]
        x2 = y_ref[rows, :] + gt2 * mo2_ref[rows, :]
        if final_norm:
            x2 = _rms_unit(x2) * gfin_ref[...]
        y_ref[rows, :] = x2

    for j in range(n_grp):
        pieces[f"up{j}"] = ("mxu", matmul_bundles(D_MODEL, 2 * cb), (), functools.partial(up_group, j))
        for r in range(n_rb):
            pieces[f"conv{j}.{r}"] = ("valu", CONV_BUNDLES, (f"up{j}",), functools.partial(conv_rows, j, r))
    all_conv = tuple(f"conv{j}.{r}" for j in range(n_grp) for r in range(n_rb))
    pieces["down"] = ("mxu", matmul_bundles(D_FF, D_MODEL), all_conv, down_proj)
    for r in range(n_rb):
        pieces[f"y{r}"] = ("valu", RESID_BUNDLES, ("down",), functools.partial(resid_out, r))

    def norm_rows(r):
        rows = row_blocks[r]
        h = _rms_unit(x_ref[rows, :]) * gmix_ref[...]
        h_ref[rows, :] = (h * (1.0 + sc1) + sh1).astype(BF16)

    def in_cols(c0, c1):
        p_ref[:, c0:c1] = _dot(h_ref[...], win_ref[:, c0:c1])

    all_norm = tuple(f"norm{r}" for r in range(n_rb))
    for r in range(n_rb):
        pieces[f"norm{r}"] = ("valu", NORM_BUNDLES, (), functools.partial(norm_rows, r))
    pieces["in_ret"] = ("mxu", matmul_bundles(D_MODEL, 4 * RET_W), all_norm, functools.partial(in_cols, 0, 4 * RET_W))
    pieces["in_gm"] = ("mxu", matmul_bundles(D_MODEL, 2 * GM_W), all_norm,
                       functools.partial(in_cols, 4 * RET_W, IN_W))

    def mix_head(c, hh):
        rows = slice(c * CHUNK, (c + 1) * CHUNK)
        cs = slice(hh * HEAD_W, (hh + 1) * HEAD_W)
        cos_full, sin_signed = cos_ref[rows, :], sin_ref[rows, :]
        q = _rope(p_ref[rows, hh * HEAD_W:(hh + 1) * HEAD_W], cos_full, sin_signed)
        k = _rope(p_ref[rows, RET_W + hh * HEAD_W:RET_W + (hh + 1) * HEAD_W], cos_full, sin_signed)
        k = k * (HEAD_W ** -0.5)
        vb = p_ref[rows, 2 * RET_W + hh * HEAD_W:2 * RET_W + (hh + 1) * HEAD_W].astype(BF16)
        g = p_ref[rows, 3 * RET_W + hh * HEAD_W:3 * RET_W + (hh + 1) * HEAD_W]
        qb = q.astype(BF16)
        qdb = (q * qdec_ref[:, cs]).astype(BF16)
        kb = k.astype(BF16)
        kdb = (k * kdec_ref[:, cs]).astype(BF16)
        scores = lax.dot_general(qb, kb, NT_DIMS, preferred_element_type=F32)
        att = (scores * dmask_ref[hh]).astype(BF16)
        state = ret_ref[hh]
        o = _dot(att, vb) + _dot(qdb, state.astype(BF16))
        new_state = cdec_ref[hh] * state + lax.dot_general(kdb, vb, TN_DIMS, preferred_element_type=F32)
        ret_ref[hh] = jnp.where(front_on, new_state, state)
        on = _center_unit(o) * gn_ref[:, cs]
        mix_ref[rows, cs] = (_silu(g) * on).astype(BF16)

    def mix_group(c, gg):
        rows = slice(c * CHUNK, (c + 1) * CHUNK)
        cs = slice(gg * HEAD_W, (gg + 1) * HEAD_W)
        u = _gelu_tanh(p_ref[rows, 4 * RET_W + gg * HEAD_W:4 * RET_W + (gg + 1) * HEAD_W])
        vv = _gelu_tanh(p_ref[rows, 4 * RET_W + GM_W + gg * HEAD_W:4 * RET_W + GM_W + (gg + 1) * HEAD_W])
        vn = _center_unit(vv) * ln_ref[:, cs]
        sg = _dot(ws_ref[gg], vn.astype(BF16)) + bs_ref[:, cs]
        mix_ref[rows, RET_W + gg * HEAD_W:RET_W + (gg + 1) * HEAD_W] = (u * sg).astype(BF16)

    def out_proj():
        mo_ref[...] = _dot(mix_ref[...], wout_ref[...])

    def resid_mid(r):
        rows = row_blocks[r]
        x1 = x_ref[rows, :] + gt1 * mo_ref[rows, :]
        x1_ref[rows, :] = x1
        h = _rms_unit(x1) * gffn_ref[...]
        h2_ref[rows, :] = (h * (1.0 + sc2) + sh2).astype(BF16)

    all_mix = []
    for c in range(n_chunks):
        for hh in range(N_HEADS):
            pieces[f"head{c}.{hh}"] = ("valu", HEAD_BUNDLES, ("in_ret",), functools.partial(mix_head, c, hh))
            pieces[f"group{c}.{hh}"] = ("valu", GROUP_BUNDLES, ("in_gm",), functools.partial(mix_group, c, hh))
            all_mix += [f"head{c}.{hh}", f"group{c}.{hh}"]
    pieces["out"] = ("mxu", matmul_bundles(RET_W + GM_W, D_MODEL), tuple(all_mix), out_proj)
    all_up = tuple(f"up{j}" for j in range(n_grp))
    for r in range(n_rb):
        pieces[f"x1{r}"] = ("valu", RESID_BUNDLES + NORM_BUNDLES, ("out",) + all_up, functools.partial(resid_mid, r))

    orders = {
        "mxu": list(all_up) + ["in_ret", "in_gm", "down", "out"],
        "valu": list(all_norm) + list(all_conv)
        + [f"{kind}{c}.{i}" for c in range(n_chunks) for kind in ("head", "group") for i in range(N_HEADS)]
        + [f"y{r}" for r in range(n_rb)] + [f"x1{r}" for r in range(n_rb)],
    }
    y_ref[...] = x1_ref[...]
    for name in _two_unit_order(pieces, orders):
        pieces[name][3]()


def _const_spec(shape):
    zeros = (0,) * len(shape)
    return pl.BlockSpec(shape, lambda s: zeros, pipeline_mode=pl.Buffered(1))


def _prompt_layer(layer, x, mod, lw, tabs, g_final, final_norm):
    batch, seq, _ = x.shape

    def layer_spec(name):
        shape = lw[name].shape
        zeros = (0,) * (len(shape) - 1)
        return pl.BlockSpec((None,) + tuple(shape[1:]), lambda s: (layer,) + zeros, pipeline_mode=pl.Buffered(1))

    tm = PROMPT_TILE
    tiles_per_seq = seq // tm
    n_tiles = batch * tiles_per_seq
    body = functools.partial(_prompt_layer_body, tm=tm, tiles_per_seq=tiles_per_seq, n_tiles=n_tiles,
                             final_norm=final_norm)

    def front(s):
        i = jnp.minimum(s, n_tiles - 1)
        return i // tiles_per_seq, lax.rem(i, tiles_per_seq)

    def back(s):
        i = jnp.maximum(s - 1, 0)
        return i // tiles_per_seq, lax.rem(i, tiles_per_seq)

    in_specs = [
        pl.BlockSpec((None, tm, D_MODEL), lambda s: (*front(s), 0)),
        pl.BlockSpec((None, None, N_MOD, D_MODEL), lambda s: (layer, front(s)[0], 0, 0)),
        pl.BlockSpec((None, None, N_MOD, D_MODEL), lambda s: (layer, back(s)[0], 0, 0)),
        layer_spec("g_mix"), layer_spec("g_ffn"),
        layer_spec("w_in"), layer_spec("w_out"), layer_spec("w_up"), layer_spec("w_down"),
        pl.BlockSpec((tm, HEAD_W), lambda s: (front(s)[1], 0)),
        pl.BlockSpec((tm, HEAD_W), lambda s: (front(s)[1], 0)),
        _const_spec((N_HEADS, CHUNK, CHUNK)), _const_spec((CHUNK, RET_W)), _const_spec((CHUNK, RET_W)),
        _const_spec((N_HEADS, 1, HEAD_W)), layer_spec("gn"), layer_spec("ln"),
        layer_spec("ws_tril"), layer_spec("bs_full"),
        layer_spec("conv_w"), layer_spec("conv_b"), _const_spec((1, D_MODEL)),
    ]
    out_specs = [
        pl.BlockSpec((None, tm, D_MODEL), lambda s: (*back(s), 0)),
        pl.BlockSpec((None, N_HEADS, HEAD_W, HEAD_W), lambda s: (front(s)[0], 0, 0, 0)),
        pl.BlockSpec((None, CONV_W - 1, 2 * D_FF), lambda s: (back(s)[0], 0, 0)),
    ]
    out_shape = [
        jax.ShapeDtypeStruct((batch, seq, D_MODEL), F32),
        jax.ShapeDtypeStruct((batch, N_HEADS, HEAD_W, HEAD_W), F32),
        jax.ShapeDtypeStruct((batch, CONV_W - 1, 2 * D_FF), F32),
    ]
    scratch = [
        pltpu.VMEM((tm, D_MODEL), BF16),
        pltpu.VMEM((tm, IN_W), F32),
        pltpu.VMEM((tm, RET_W + GM_W), BF16),
        pltpu.VMEM((tm, D_MODEL), F32),
        pltpu.VMEM((tm, D_MODEL), F32),
        pltpu.VMEM((tm, D_MODEL), BF16),
        pltpu.VMEM((tm, D_FF), BF16),
        pltpu.VMEM((tm, D_MODEL), F32),
    ] + [pltpu.VMEM((CONV_PAD + tm, 2 * 512), F32)] * (D_FF // 512)
    return pl.pallas_call(
        body,
        grid=(n_tiles + 1,),
        in_specs=in_specs,
        out_specs=out_specs,
        out_shape=out_shape,
        scratch_shapes=scratch,
        compiler_params=pltpu.CompilerParams(
            dimension_semantics=("arbitrary",), vmem_limit_bytes=V7X_VMEM_LIMIT_BYTES),
        name="prompt_layer",
    )(x, mod, mod, lw["g_mix"], lw["g_ffn"], lw["w_in"], lw["w_out"], lw["w_up"], lw["w_down"],
      tabs["cos"], tabs["sin"], tabs["dmask"], tabs["qdec"], tabs["kdec"], tabs["cdec"],
      lw["gn"], lw["ln"], lw["ws_tril"], lw["bs_full"], lw["conv_w"], lw["conv_b"], g_final)


def _sample_in_body(x_ref, mod_ref, gmix_ref, win_ref, cos_ref, sin_ref, ln_ref, ws0_ref, bs0_ref,
                    q_ref, k_ref, v_ref, sg_ref, ogm_ref, vn_ref):
    sh1, sc1 = mod_ref[:, 0:D_MODEL], mod_ref[:, D_MODEL:2 * D_MODEL]
    h = _rms_unit(x_ref[...]) * gmix_ref[...]
    h = (h * (1.0 + sc1) + sh1).astype(BF16)
    p = _dot(h, win_ref[...])
    for hh in range(N_HEADS):
        cs = slice(hh * HEAD_W, (hh + 1) * HEAD_W)
        q_ref[:, cs] = _rope(p[:, hh * HEAD_W:(hh + 1) * HEAD_W], cos_ref[...], sin_ref[...])
        k_ref[:, cs] = _rope(p[:, RET_W + hh * HEAD_W:RET_W + (hh + 1) * HEAD_W],
                             cos_ref[...], sin_ref[...]) * (HEAD_W ** -0.5)
        u = _gelu_tanh(p[:, 4 * RET_W + hh * HEAD_W:4 * RET_W + (hh + 1) * HEAD_W])
        vv = _gelu_tanh(p[:, 4 * RET_W + GM_W + hh * HEAD_W:4 * RET_W + GM_W + (hh + 1) * HEAD_W])
        vn = _center_unit(vv) * ln_ref[:, cs]
        vn_ref[:, cs] = vn
        ogm_ref[:, cs] = u * (ws0_ref[:, cs] * vn + bs0_ref[:, cs])
    v_ref[...] = p[:, 2 * RET_W:3 * RET_W]
    sg_ref[...] = _silu(p[:, 3 * RET_W:4 * RET_W])


def _sample_ret_body(q_ref, k_ref, v_ref, s_ref, gam_ref, *rest, sb, layer, first):
    o_ref, snew_ref = rest[-2:]
    if first:
        for other in range(snew_ref.shape[0]):
            if other != layer:
                snew_ref[other] = jnp.zeros(snew_ref.shape[1:], F32)
        snew_ref = snew_ref.at[layer]
    for hh in range(N_HEADS):
        cs = slice(hh * HEAD_W, (hh + 1) * HEAD_W)
        gam = gam_ref[:, cs]
        v = v_ref[:, cs]
        k = k_ref[:, cs]
        qb = q_ref[:, cs].astype(BF16)
        kt = jnp.concatenate([k, jnp.zeros((HEAD_W - sb, HEAD_W), F32)], axis=0).T
        for s in range(sb):
            state = s_ref[s, hh]
            snew_ref[s, hh] = gam * state + kt[:, s:s + 1] * v[s:s + 1, :]
            o_ref[s:s + 1, cs] = gam * _dot(qb, state.astype(BF16))[s:s + 1, :]
        qk = jnp.sum(q_ref[:, cs] * k, axis=-1, keepdims=True)
        o_ref[:, cs] = o_ref[:, cs] + qk * v


def _sample_out_body(x_ref, mod_ref, o_ref, sg_ref, ogm_ref, gn_ref, wout_ref, gffn_ref, wup_ref,
                     cbuf_ref, convw_ref, convb_ref, wdown_ref, gfin_ref, acc_ref, y_ref, cnew_ref,
                     *, final_norm):
    del acc_ref
    gt1 = mod_ref[:, 2 * D_MODEL:3 * D_MODEL]
    sh2, sc2 = mod_ref[:, 3 * D_MODEL:4 * D_MODEL], mod_ref[:, 4 * D_MODEL:5 * D_MODEL]
    gt2 = mod_ref[:, 5 * D_MODEL:6 * D_MODEL]
    parts = []
    for hh in range(N_HEADS):
        cs = slice(hh * HEAD_W, (hh + 1) * HEAD_W)
        on = _center_unit(o_ref[:, cs]) * gn_ref[:, cs]
        parts.append((sg_ref[:, cs] * on).astype(BF16))
    parts.append(ogm_ref[...].astype(BF16))
    mix = _dot(jnp.concatenate(parts, axis=1), wout_ref[...])
    x1 = x_ref[...] + gt1 * mix
    h = _rms_unit(x1) * gffn_ref[...]
    h = (h * (1.0 + sc2) + sh2).astype(BF16)
    a = _dot(h, wup_ref[...])
    w = 2 * D_FF
    gw = GATE_GROUP_W
    f_parts = []
    for j in range(D_FF // gw):
        halves = []
        for half in range(2):
            c0 = half * D_FF + j * gw
            a_blk = a[:, (2 * j + half) * gw:(2 * j + half + 1) * gw]
            buf0, buf1 = cbuf_ref[:, c0:c0 + gw], cbuf_ref[:, w + c0:w + c0 + gw]
            halves.append(convb_ref[:, c0:c0 + gw] + buf0 * convw_ref[0:1, c0:c0 + gw]
                          + buf1 * convw_ref[1:2, c0:c0 + gw] + a_blk * convw_ref[2:3, c0:c0 + gw])
            cnew_ref[:, c0:c0 + gw] = buf1
            cnew_ref[:, w + c0:w + c0 + gw] = a_blk
        f_parts.append((_silu(halves[0]) * halves[1]).astype(BF16))
    f = jnp.concatenate(f_parts, axis=1)
    x2 = x1 + gt2 * _dot(f, wdown_ref[...])
    if final_norm:
        x2 = _rms_unit(x2) * gfin_ref[...]
    y_ref[...] = x2


def _whole(shape):
    zeros = (0,) * len(shape)
    return pl.BlockSpec(shape, lambda *_: zeros)


def _layer_slab(shape, layer):
    zeros = (0,) * (len(shape) - 1)
    return pl.BlockSpec((None,) + tuple(shape[1:]), lambda *_: (layer,) + zeros)


def _sample_layer(layer, x, mod, state_ret, state_conv, ret_acc, conv_acc, lw, tabs, g_final, final_norm):
    n = x.shape[0]
    params = pltpu.CompilerParams(dimension_semantics=("arbitrary",), vmem_limit_bytes=V7X_VMEM_LIMIT_BYTES)
    act = jax.ShapeDtypeStruct((n, RET_W), F32)
    in_args = (x, mod, lw["g_mix"], lw["w_in"], tabs["cos_s"], tabs["sin_s"], lw["ln"], lw["ws0"], lw["bs0"])
    in_specs = [_whole(a.shape) for a in in_args]
    in_specs[1] = pl.BlockSpec((None, n, N_MOD * D_MODEL), lambda i: (layer, 0, 0))
    for i in (2, 3, 6, 7, 8):
        in_specs[i] = _layer_slab(in_args[i].shape, layer)
    q, k, v, sg, ogm, vn = pl.pallas_call(
        _sample_in_body,
        grid=(1,),
        in_specs=in_specs,
        out_specs=[_whole(act.shape)] * 6,
        out_shape=[act] * 6,
        compiler_params=params,
        name="sample_in",
    )(*in_args)

    sb = SAMPLE_SEQ_BLOCK
    row_spec = pl.BlockSpec((sb, RET_W), lambda i: (i, 0))
    state_spec = pl.BlockSpec((None, sb, N_HEADS, HEAD_W, HEAD_W), lambda i: (layer, i, 0, 0, 0))
    carried = [] if ret_acc is None else [ret_acc]
    depth = state_ret.shape[0]
    first = ret_acc is None
    all_layers_spec = pl.BlockSpec((depth, sb, N_HEADS, HEAD_W, HEAD_W), lambda i: (0, i, 0, 0, 0))
    o, ret_acc = pl.pallas_call(
        functools.partial(_sample_ret_body, sb=sb, layer=layer, first=first),
        grid=(n // sb,),
        in_specs=[row_spec, row_spec, row_spec, state_spec, _whole((1, RET_W))]
        + [pl.BlockSpec(memory_space=pl.ANY)] * len(carried),
        out_specs=[row_spec, all_layers_spec if first else state_spec],
        out_shape=[act, jax.ShapeDtypeStruct(state_ret.shape, F32)],
        input_output_aliases={5: 1} if carried else {},
        compiler_params=params,
        name="sample_ret",
    )(q, k, v, state_ret, tabs["gamma_s"], *carried)

    cbuf = state_conv.reshape(depth, n, (CONV_W - 1) * 2 * D_FF)
    if conv_acc is None:
        conv_acc = jnp.zeros(cbuf.shape, F32)
    out_args = (x, mod, o, sg, ogm, lw["gn"], lw["w_out"], lw["g_ffn"], lw["w_up"], cbuf,
                lw["conv_w"], lw["conv_b"], lw["w_down"], g_final)
    in_specs = [_whole(a.shape) for a in out_args]
    in_specs[1] = pl.BlockSpec((None, n, N_MOD * D_MODEL), lambda i: (layer, 0, 0))
    for i in (5, 6, 7, 8, 9, 10, 11, 12):
        in_specs[i] = _layer_slab(out_args[i].shape, layer)

    y, conv_acc = pl.pallas_call(
        functools.partial(_sample_out_body, final_norm=final_norm),
        grid=(1,),
        in_specs=in_specs + [pl.BlockSpec(memory_space=pl.ANY)],
        out_specs=[_whole((n, D_MODEL)), _layer_slab(cbuf.shape, layer)],
        out_shape=[jax.ShapeDtypeStruct((n, D_MODEL), F32), jax.ShapeDtypeStruct(cbuf.shape, F32)],
        input_output_aliases={len(out_args): 1},
        compiler_params=params,
        name="sample_out",
    )(*out_args, conv_acc)
    return y, ret_acc, conv_acc, vn


def _rope_tables(pos):
    half = HEAD_W // 2
    freqs = np.exp(-math.log(ROPE_BASE) * np.arange(half, dtype=np.float64) / half)
    ang = np.asarray(pos, dtype=np.float64)[:, None] * freqs[None, :]
    cos, sin = np.cos(ang), np.sin(ang)
    return (np.concatenate([cos, cos], axis=-1).astype(np.float32),
            np.concatenate([-sin, sin], axis=-1).astype(np.float32))


def _decay_tables(chunk):
    lg = np.log1p(-np.exp2(-5.0 - np.arange(N_HEADS, dtype=np.float64)))
    i = np.arange(chunk, dtype=np.float64)
    diff = i[:, None] - i[None, :]
    dmask = np.where(diff[None] >= 0.0, np.exp(np.maximum(diff, 0.0)[None] * lg[:, None, None]), 0.0)
    q_dec = np.exp((i[:, None] + 1.0) * lg[None, :])
    k_dec = np.exp((chunk - 1.0 - i)[:, None] * lg[None, :])
    chunk_dec = np.exp(chunk * lg)
    return tuple(a.astype(np.float32) for a in (dmask, q_dec, k_dec, chunk_dec))


def _per_head_lanes(a):
    return a.repeat(HEAD_W, axis=-1)


def kernel(x_prompt, x_sample, state_ret, state_conv, c_prompt, c_sample, w_ada, b_ada, g_mix, w_in,
           ret_gn_gain, gmlp_ln_gain, w_s, b_s, w_out, g_ffn, w_up, conv_w, conv_b, w_down, g_final):
    depth = w_in.shape[0]
    batch, seq, _ = x_prompt.shape
    n_dec, dec_seq, _ = x_sample.shape
    assert dec_seq == 1 and seq % PROMPT_TILE == 0 and PROMPT_TILE % CHUNK == 0

    cos_p, sin_p = _rope_tables(np.arange(seq))
    cos_s, sin_s = _rope_tables(PAST_LEN + np.arange(dec_seq))
    dmask, q_dec, k_dec, chunk_dec = _decay_tables(CHUNK)
    _, _, _, gamma_s = _decay_tables(dec_seq)
    tabs = {
        "cos": cos_p, "sin": sin_p, "cos_s": cos_s, "sin_s": sin_s,
        "dmask": dmask, "qdec": _per_head_lanes(q_dec), "kdec": _per_head_lanes(k_dec),
        "cdec": np.broadcast_to(chunk_dec[:, None, None], (N_HEADS, 1, HEAD_W)),
        "gamma_s": _per_head_lanes(gamma_s[None, :]),
    }
    tabs = {name: jnp.asarray(a) for name, a in tabs.items()}

    mod = _adaln(jnp.concatenate([c_sample, c_prompt], axis=0), w_ada, b_ada)
    mod_p = mod[:, n_dec:].reshape(depth, batch, N_MOD, D_MODEL)
    tril = np.tril(np.ones((CHUNK, CHUNK), dtype=bool))
    lw = {
        "g_mix": g_mix.reshape(depth, 1, D_MODEL), "g_ffn": g_ffn.reshape(depth, 1, D_MODEL),
        "w_in": _to_bf16(w_in), "w_out": _to_bf16(w_out),
        "w_up": _to_bf16(w_up, group_gate_columns=True), "w_down": _to_bf16(w_down),
        "gn": ret_gn_gain.reshape(depth, 1, RET_W), "ln": gmlp_ln_gain.reshape(depth, 1, GM_W),
        "ws_tril": jnp.where(tril, w_s, 0.0).astype(BF16),
        "bs_full": _per_head_lanes(b_s.transpose(0, 2, 1)),
        "ws0": _per_head_lanes(w_s[:, None, :, 0, 0]), "bs0": _per_head_lanes(b_s[:, None, :, 0]),
        "conv_w": conv_w, "conv_b": conv_b.reshape(depth, 1, 2 * D_FF),
    }
    g_fin = g_final.reshape(1, D_MODEL)

    xp, xs = x_prompt, x_sample.reshape(n_dec, D_MODEL)
    ret_p, conv_p, v_s = [], [], []
    ret_s = conv_s = None
    for l in range(depth):
        last = l == depth - 1
        xp, sp, bp = _prompt_layer(l, xp, mod_p, lw, tabs, g_fin, last)
        xs, ret_s, conv_s, vs = _sample_layer(l, xs, mod, state_ret, state_conv, ret_s, conv_s, lw, tabs,
                                              g_fin, last)
        ret_p.append(sp); conv_p.append(bp)
        v_s.append(vs.reshape(n_dec, dec_seq, GM_W))
    return (xp, xs.reshape(n_dec, dec_seq, D_MODEL), jnp.stack(ret_p), jnp.stack(conv_p),
            ret_s, conv_s.reshape(state_conv.shape), jnp.stack(v_s))
```

```python
import functools
import math

import jax
import jax.numpy as jnp
import numpy as np
from jax import lax
from jax.experimental import pallas as pl
from jax.experimental.pallas import tpu as pltpu

D_MODEL = 1024
N_HEADS = 4
HEAD_W = 128
RET_W = N_HEADS * HEAD_W
GM_W = N_HEADS * HEAD_W
IN_W = 4 * RET_W + 2 * GM_W
D_FF = 2048
CONV_W = 3
CHUNK = 128
ROPE_BASE = 10000.0
PAST_LEN = 16384
EPS = 1e-6
N_MOD = 6

V7X_VMEM_LIMIT_BYTES = 60 * 1024 * 1024
PROMPT_TILE = 512
ROW_BLOCK = 32
CONV_PAD = 8
GATE_GROUP_W = 512
SAMPLE_SEQ_BLOCK = 8

F32 = jnp.float32
BF16 = jnp.bfloat16
NT_DIMS = (((1,), (1,)), ((), ()))
TN_DIMS = (((0,), (0,)), ((), ()))


def _silu(x):
    return x * (1.0 / (1.0 + jnp.exp(-x)))


def _gelu_tanh(x):
    c = math.sqrt(2.0 / math.pi)
    return 0.5 * x * (1.0 + jnp.tanh(c * (x + 0.044715 * (x * x * x))))


def _rms_unit(x):
    return x * lax.rsqrt(jnp.mean(x * x, axis=-1, keepdims=True) + EPS)


def _center_unit(x):
    xc = x - jnp.mean(x, axis=-1, keepdims=True)
    return xc * lax.rsqrt(jnp.mean(xc * xc, axis=-1, keepdims=True) + EPS)


def _rope(x, cos_full, sin_signed):
    return x * cos_full + pltpu.roll(x, HEAD_W // 2, 1) * sin_signed


def _dot(a, b):
    return jnp.dot(a, b, preferred_element_type=F32)


NORM_BUNDLES = 60
RESID_BUNDLES = 30
CONV_BUNDLES = 170
HEAD_BUNDLES = 160
GROUP_BUNDLES = 130


def _two_unit_order(pieces, orders):
    unit_free = {unit: 0 for unit in orders}
    head = {unit: 0 for unit in orders}
    finish = {}
    order = []
    while any(head[unit] < len(names) for unit, names in orders.items()):
        ready = []
        for unit, names in orders.items():
            if head[unit] < len(names):
                name = names[head[unit]]
                deps = pieces[name][2]
                if all(d in finish for d in deps):
                    ready.append((max([unit_free[unit]] + [finish[d] for d in deps]), unit, name))
        assert ready, "piece lists contradict the dependencies"
        start, unit, name = min(ready)
        finish[name] = unit_free[unit] = start + pieces[name][1]
        head[unit] += 1
        order.append(name)
    return order


def _cast_body(w_ref, o_ref):
    o_ref[...] = w_ref[...].astype(BF16)


def _to_bf16(w, group_gate_columns=False):
    depth, k, n = w.shape
    bn = GATE_GROUP_W
    n_grp = n // (2 * bn)

    def src_block(c):
        return (c % 2) * n_grp + c // 2 if group_gate_columns else c

    return pl.pallas_call(
        _cast_body,
        grid=(depth, n // bn),
        in_specs=[pl.BlockSpec((None, k, bn), lambda l, c: (l, 0, src_block(c)))],
        out_specs=pl.BlockSpec((None, k, bn), lambda l, c: (l, 0, c)),
        out_shape=jax.ShapeDtypeStruct(w.shape, BF16),
        compiler_params=pltpu.CompilerParams(
            dimension_semantics=("arbitrary", "arbitrary"), vmem_limit_bytes=V7X_VMEM_LIMIT_BYTES),
        name="to_bf16",
    )(w)


def _adaln_body(c_ref, w_ref, b_ref, o_ref):
    c = _silu(c_ref[...]).astype(BF16)
    o_ref[...] = _dot(c, w_ref[...].astype(BF16)) + b_ref[...]


def _adaln(c_all, w_ada, b_ada):
    depth = w_ada.shape[0]
    rows = c_all.shape[0]
    n_out = w_ada.shape[2]
    bn = 1536
    return pl.pallas_call(
        _adaln_body,
        grid=(depth, n_out // bn),
        in_specs=[
            pl.BlockSpec((rows, D_MODEL), lambda l, j: (0, 0)),
            pl.BlockSpec((None, D_MODEL, bn), lambda l, j: (l, 0, j)),
            pl.BlockSpec((None, 1, bn), lambda l, j: (l, 0, j)),
        ],
        out_specs=pl.BlockSpec((None, rows, bn), lambda l, j: (l, 0, j)),
        out_shape=jax.ShapeDtypeStruct((depth, rows, n_out), F32),
        compiler_params=pltpu.CompilerParams(
            dimension_semantics=("arbitrary", "arbitrary"), vmem_limit_bytes=V7X_VMEM_LIMIT_BYTES),
        name="adaln",
    )(c_all, w_ada, b_ada.reshape(depth, 1, n_out))


def _prompt_layer_body(x_ref, modf_ref, modb_ref, gmix_ref, gffn_ref, win_ref, wout_ref, wup_ref, wdown_ref,
                       cos_ref, sin_ref, dmask_ref, qdec_ref, kdec_ref, cdec_ref, gn_ref, ln_ref,
                       ws_ref, bs_ref, convw_ref, convb_ref, gfin_ref,
                       y_ref, ret_ref, conv_ref,
                       h_ref, p_ref, mix_ref, mo_ref, x1_ref, h2_ref, f_ref, mo2_ref, *a_refs,
                       tm, tiles_per_seq, n_tiles, final_norm):
    s = pl.program_id(0)
    front_on = s < n_tiles
    t_front = lax.rem(jnp.minimum(s, n_tiles - 1), tiles_per_seq)
    t_back = lax.rem(jnp.maximum(s - 1, 0), tiles_per_seq)

    @pl.when(s == 0)
    def _():
        x1_ref[...] = jnp.zeros_like(x1_ref)
        h2_ref[...] = jnp.zeros_like(h2_ref)

    @pl.when(jnp.logical_and(front_on, t_front == 0))
    def _():
        ret_ref[...] = jnp.zeros_like(ret_ref)

    @pl.when(t_back == 0)
    def _():
        conv_ref[...] = jnp.zeros_like(conv_ref)

    sh1, sc1, gt1 = modf_ref[0:1, :], modf_ref[1:2, :], modf_ref[2:3, :]
    sh2, sc2 = modf_ref[3:4, :], modf_ref[4:5, :]
    gt2 = modb_ref[5:6, :]
    row_blocks = [slice(r * ROW_BLOCK, (r + 1) * ROW_BLOCK) for r in range(tm // ROW_BLOCK)]

    cb = GATE_GROUP_W
    tail = slice(CONV_PAD - (CONV_W - 1), CONV_PAD)

    pieces = {}
    n_rb = len(row_blocks)
    n_grp = D_FF // cb
    n_chunks = tm // CHUNK

    def matmul_bundles(k, n):
        return (tm // 16) * (k // 256) * (n // 256) * 8 // 2

    def up_group(j):
        for half in range(2):
            c0 = half * D_FF + j * cb
            a_refs[j][tail, half * cb:(half + 1) * cb] = conv_ref[:, c0:c0 + cb]
        a_refs[j][CONV_PAD:CONV_PAD + tm, :] = _dot(h2_ref[...], wup_ref[:, 2 * j * cb:2 * (j + 1) * cb])
        for half in range(2):
            c0 = half * D_FF + j * cb
            conv_ref[:, c0:c0 + cb] = a_refs[j][CONV_PAD + tm - (CONV_W - 1):CONV_PAD + tm,
                                                half * cb:(half + 1) * cb]

    def conv_rows(j, r):
        halves = []
        for half in range(2):
            c0 = half * D_FF + j * cb
            src = slice(half * cb, (half + 1) * cb)
            window = a_refs[j][r * ROW_BLOCK:CONV_PAD + (r + 1) * ROW_BLOCK, src]
            acc = convb_ref[:, c0:c0 + cb]
            for tap in range(CONV_W):
                back = CONV_W - 1 - tap
                rows_back = pltpu.roll(window, back, 0) if back else window
                acc = acc + rows_back[CONV_PAD:, :] * convw_ref[tap:tap + 1, c0:c0 + cb]
            halves.append(acc)
        f_ref[row_blocks[r], j * cb:(j + 1) * cb] = (_silu(halves[0]) * halves[1]).astype(BF16)

    def down_proj():
        mo2_ref[...] = _dot(f_ref[...], wdown_ref[...])

    def resid_out(r):
        rows = row_blocks[r]
        x2 = y_ref[rows, :] + gt2 * mo2_ref[rows, :]
        if final_norm:
            x2 = _rms_unit(x2) * gfin_ref[...]
        y_ref[rows, :] = x2

    for j in range(n_grp):
        pieces[f"up{j}"] = ("mxu", matmul_bundles(D_MODEL, 2 * cb), (), functools.partial(up_group, j))
        for r in range(n_rb):
            pieces[f"conv{j}.{r}"] = ("valu", CONV_BUNDLES, (f"up{j}",), functools.partial(conv_rows, j, r))
    all_conv = tuple(f"conv{j}.{r}" for j in range(n_grp) for r in range(n_rb))
    pieces["down"] = ("mxu", matmul_bundles(D_FF, D_MODEL), all_conv, down_proj)
    for r in range(n_rb):
        pieces[f"y{r}"] = ("valu", RESID_BUNDLES, ("down",), functools.partial(resid_out, r))

    def norm_rows(r):
        rows = row_blocks[r]
        h = _rms_unit(x_ref[rows, :]) * gmix_ref[...]
        h_ref[rows, :] = (h * (1.0 + sc1) + sh1).astype(BF16)

    def in_cols(c0, c1):
        p_ref[:, c0:c1] = _dot(h_ref[...], win_ref[:, c0:c1])

    all_norm = tuple(f"norm{r}" for r in range(n_rb))
    for r in range(n_rb):
        pieces[f"norm{r}"] = ("valu", NORM_BUNDLES, (), functools.partial(norm_rows, r))
    pieces["in_ret"] = ("mxu", matmul_bundles(D_MODEL, 4 * RET_W), all_norm, functools.partial(in_cols, 0, 4 * RET_W))
    pieces["in_gm"] = ("mxu", matmul_bundles(D_MODEL, 2 * GM_W), all_norm,
                       functools.partial(in_cols, 4 * RET_W, IN_W))

    def mix_head(c, hh):
        rows = slice(c * CHUNK, (c + 1) * CHUNK)
        cs = slice(hh * HEAD_W, (hh + 1) * HEAD_W)
        cos_full, sin_signed = cos_ref[rows, :], sin_ref[rows, :]
        q = _rope(p_ref[rows, hh * HEAD_W:(hh + 1) * HEAD_W], cos_full, sin_signed)
        k = _rope(p_ref[rows, RET_W + hh * HEAD_W:RET_W + (hh + 1) * HEAD_W], cos_full, sin_signed)
        k = k * (HEAD_W ** -0.5)
        vb = p_ref[rows, 2 * RET_W + hh * HEAD_W:2 * RET_W + (hh + 1) * HEAD_W].astype(BF16)
        g = p_ref[rows, 3 * RET_W + hh * HEAD_W:3 * RET_W + (hh + 1) * HEAD_W]
        qb = q.astype(BF16)
        qdb = (q * qdec_ref[:, cs]).astype(BF16)
        kb = k.astype(BF16)
        kdb = (k * kdec_ref[:, cs]).astype(BF16)
        scores = lax.dot_general(qb, kb, NT_DIMS, preferred_element_type=F32)
        att = (scores * dmask_ref[hh]).astype(BF16)
        state = ret_ref[hh]
        o = _dot(att, vb) + _dot(qdb, state.astype(BF16))
        new_state = cdec_ref[hh] * state + lax.dot_general(kdb, vb, TN_DIMS, preferred_element_type=F32)
        ret_ref[hh] = jnp.where(front_on, new_state, state)
        on = _center_unit(o) * gn_ref[:, cs]
        mix_ref[rows, cs] = (_silu(g) * on).astype(BF16)

    def mix_group(c, gg):
        rows = slice(c * CHUNK, (c + 1) * CHUNK)
        cs = slice(gg * HEAD_W, (gg + 1) * HEAD_W)
        u = _gelu_tanh(p_ref[rows, 4 * RET_W + gg * HEAD_W:4 * RET_W + (gg + 1) * HEAD_W])
        vv = _gelu_tanh(p_ref[rows, 4 * RET_W + GM_W + gg * HEAD_W:4 * RET_W + GM_W + (gg + 1) * HEAD_W])
        vn = _center_unit(vv) * ln_ref[:, cs]
        sg = _dot(ws_ref[gg], vn.astype(BF16)) + bs_ref[:, cs]
        mix_ref[rows, RET_W + gg * HEAD_W:RET_W + (gg + 1) * HEAD_W] = (u * sg).astype(BF16)

    def out_proj():
        mo_ref[...] = _dot(mix_ref[...], wout_ref[...])

    def resid_mid(r):
        rows = row_blocks[r]
        x1 = x_ref[rows, :] + gt1 * mo_ref[rows, :]
        x1_ref[rows, :] = x1
        h = _rms_unit(x1) * gffn_ref[...]
        h2_ref[rows, :] = (h * (1.0 + sc2) + sh2).astype(BF16)

    all_mix = []
    for c in range(n_chunks):
        for hh in range(N_HEADS):
            pieces[f"head{c}.{hh}"] = ("valu", HEAD_BUNDLES, ("in_ret",), functools.partial(mix_head, c, hh))
            pieces[f"group{c}.{hh}"] = ("valu", GROUP_BUNDLES, ("in_gm",), functools.partial(mix_group, c, hh))
            all_mix += [f"head{c}.{hh}", f"group{c}.{hh}"]
    pieces["out"] = ("mxu", matmul_bundles(RET_W + GM_W, D_MODEL), tuple(all_mix), out_proj)
    all_up = tuple(f"up{j}" for j in range(n_grp))
    for r in range(n_rb):
        pieces[f"x1{r}"] = ("valu", RESID_BUNDLES + NORM_BUNDLES, ("out",) + all_up, functools.partial(resid_mid, r))

    orders = {
        "mxu": list(all_up) + ["in_ret", "in_gm", "down", "out"],
        "valu": list(all_norm) + list(all_conv)
        + [f"{kind}{c}.{i}" for c in range(n_chunks) for kind in ("head", "group") for i in range(N_HEADS)]
        + [f"y{r}" for r in range(n_rb)] + [f"x1{r}" for r in range(n_rb)],
    }
    y_ref[...] = x1_ref[...]
    for name in _two_unit_order(pieces, orders):
        pieces[name][3]()


def _const_spec(shape):
    zeros = (0,) * len(shape)
    return pl.BlockSpec(shape, lambda s: zeros, pipeline_mode=pl.Buffered(1))


def _prompt_layer(layer, x, mod, lw, tabs, g_final, final_norm):
    batch, seq, _ = x.shape

    def layer_spec(name):
        shape = lw[name].shape
        zeros = (0,) * (len(shape) - 1)
        return pl.BlockSpec((None,) + tuple(shape[1:]), lambda s: (layer,) + zeros, pipeline_mode=pl.Buffered(1))

    tm = PROMPT_TILE
    tiles_per_seq = seq // tm
    n_tiles = batch * tiles_per_seq
    body = functools.partial(_prompt_layer_body, tm=tm, tiles_per_seq=tiles_per_seq, n_tiles=n_tiles,
                             final_norm=final_norm)

    def front(s):
        i = jnp.minimum(s, n_tiles - 1)
        return i // tiles_per_seq, lax.rem(i, tiles_per_seq)

    def back(s):
        i = jnp.maximum(s - 1, 0)
        return i // tiles_per_seq, lax.rem(i, tiles_per_seq)

    in_specs = [
        pl.BlockSpec((None, tm, D_MODEL), lambda s: (*front(s), 0)),
        pl.BlockSpec((None, None, N_MOD, D_MODEL), lambda s: (layer, front(s)[0], 0, 0)),
        pl.BlockSpec((None, None, N_MOD, D_MODEL), lambda s: (layer, back(s)[0], 0, 0)),
        layer_spec("g_mix"), layer_spec("g_ffn"),
        layer_spec("w_in"), layer_spec("w_out"), layer_spec("w_up"), layer_spec("w_down"),
        pl.BlockSpec((tm, HEAD_W), lambda s: (front(s)[1], 0)),
        pl.BlockSpec((tm, HEAD_W), lambda s: (front(s)[1], 0)),
        _const_spec((N_HEADS, CHUNK, CHUNK)), _const_spec((CHUNK, RET_W)), _const_spec((CHUNK, RET_W)),
        _const_spec((N_HEADS, 1, HEAD_W)), layer_spec("gn"), layer_spec("ln"),
        layer_spec("ws_tril"), layer_spec("bs_full"),
        layer_spec("conv_w"), layer_spec("conv_b"), _const_spec((1, D_MODEL)),
    ]
    out_specs = [
        pl.BlockSpec((None, tm, D_MODEL), lambda s: (*back(s), 0)),
        pl.BlockSpec((None, N_HEADS, HEAD_W, HEAD_W), lambda s: (front(s)[0], 0, 0, 0)),
        pl.BlockSpec((None, CONV_W - 1, 2 * D_FF), lambda s: (back(s)[0], 0, 0)),
    ]
    out_shape = [
        jax.ShapeDtypeStruct((batch, seq, D_MODEL), F32),
        jax.ShapeDtypeStruct((batch, N_HEADS, HEAD_W, HEAD_W), F32),
        jax.ShapeDtypeStruct((batch, CONV_W - 1, 2 * D_FF), F32),
    ]
    scratch = [
        pltpu.VMEM((tm, D_MODEL), BF16),
        pltpu.VMEM((tm, IN_W), F32),
        pltpu.VMEM((tm, RET_W + GM_W), BF16),
        pltpu.VMEM((tm, D_MODEL), F32),
        pltpu.VMEM((tm, D_MODEL), F32),
        pltpu.VMEM((tm, D_MODEL), BF16),
        pltpu.VMEM((tm, D_FF), BF16),
        pltpu.VMEM((tm, D_MODEL), F32),
    ] + [pltpu.VMEM((CONV_PAD + tm, 2 * 512), F32)] * (D_FF // 512)
    return pl.pallas_call(
        body,
        grid=(n_tiles + 1,),
        in_specs=in_specs,
        out_specs=out_specs,
        out_shape=out_shape,
        scratch_shapes=scratch,
        compiler_params=pltpu.CompilerParams(
            dimension_semantics=("arbitrary",), vmem_limit_bytes=V7X_VMEM_LIMIT_BYTES),
        name="prompt_layer",
    )(x, mod, mod, lw["g_mix"], lw["g_ffn"], lw["w_in"], lw["w_out"], lw["w_up"], lw["w_down"],
      tabs["cos"], tabs["sin"], tabs["dmask"], tabs["qdec"], tabs["kdec"], tabs["cdec"],
      lw["gn"], lw["ln"], lw["ws_tril"], lw["bs_full"], lw["conv_w"], lw["conv_b"], g_final)


def _sample_in_body(x_ref, mod_ref, gmix_ref, win_ref, cos_ref, sin_ref, ln_ref, ws0_ref, bs0_ref,
                    q_ref, k_ref, v_ref, sg_ref, ogm_ref, vn_ref):
    sh1, sc1 = mod_ref[:, 0:D_MODEL], mod_ref[:, D_MODEL:2 * D_MODEL]
    h = _rms_unit(x_ref[...]) * gmix_ref[...]
    h = (h * (1.0 + sc1) + sh1).astype(BF16)
    p = _dot(h, win_ref[...])
    for hh in range(N_HEADS):
        cs = slice(hh * HEAD_W, (hh + 1) * HEAD_W)
        q_ref[:, cs] = _rope(p[:, hh * HEAD_W:(hh + 1) * HEAD_W], cos_ref[...], sin_ref[...])
        k_ref[:, cs] = _rope(p[:, RET_W + hh * HEAD_W:RET_W + (hh + 1) * HEAD_W],
                             cos_ref[...], sin_ref[...]) * (HEAD_W ** -0.5)
        u = _gelu_tanh(p[:, 4 * RET_W + hh * HEAD_W:4 * RET_W + (hh + 1) * HEAD_W])
        vv = _gelu_tanh(p[:, 4 * RET_W + GM_W + hh * HEAD_W:4 * RET_W + GM_W + (hh + 1) * HEAD_W])
        vn = _center_unit(vv) * ln_ref[:, cs]
        vn_ref[:, cs] = vn
        ogm_ref[:, cs] = u * (ws0_ref[:, cs] * vn + bs0_ref[:, cs])
    v_ref[...] = p[:, 2 * RET_W:3 * RET_W]
    sg_ref[...] = _silu(p[:, 3 * RET_W:4 * RET_W])


def _sample_ret_body(q_ref, k_ref, v_ref, s_ref, gam_ref, *rest, sb, layer, first):
    o_ref, snew_ref = rest[-2:]
    if first:
        for other in range(snew_ref.shape[0]):
            if other != layer:
                snew_ref[other] = jnp.zeros(snew_ref.shape[1:], F32)
        snew_ref = snew_ref.at[layer]
    for hh in range(N_HEADS):
        cs = slice(hh * HEAD_W, (hh + 1) * HEAD_W)
        gam = gam_ref[:, cs]
        v = v_ref[:, cs]
        k = k_ref[:, cs]
        qb = q_ref[:, cs].astype(BF16)
        kt = jnp.concatenate([k, jnp.zeros((HEAD_W - sb, HEAD_W), F32)], axis=0).T
        for s in range(sb):
            state = s_ref[s, hh]
            snew_ref[s, hh] = gam * state + kt[:, s:s + 1] * v[s:s + 1, :]
            o_ref[s:s + 1, cs] = gam * _dot(qb, state.astype(BF16))[s:s + 1, :]
        qk = jnp.sum(q_ref[:, cs] * k, axis=-1, keepdims=True)
        o_ref[:, cs] = o_ref[:, cs] + qk * v


def _sample_out_body(x_ref, mod_ref, o_ref, sg_ref, ogm_ref, gn_ref, wout_ref, gffn_ref, wup_ref,
                     cbuf_ref, convw_ref, convb_ref, wdown_ref, gfin_ref, acc_ref, y_ref, cnew_ref,
                     *, final_norm):
    del acc_ref
    gt1 = mod_ref[:, 2 * D_MODEL:3 * D_MODEL]
    sh2, sc2 = mod_ref[:, 3 * D_MODEL:4 * D_MODEL], mod_ref[:, 4 * D_MODEL:5 * D_MODEL]
    gt2 = mod_ref[:, 5 * D_MODEL:6 * D_MODEL]
    parts = []
    for hh in range(N_HEADS):
        cs = slice(hh * HEAD_W, (hh + 1) * HEAD_W)
        on = _center_unit(o_ref[:, cs]) * gn_ref[:, cs]
        parts.append((sg_ref[:, cs] * on).astype(BF16))
    parts.append(ogm_ref[...].astype(BF16))
    mix = _dot(jnp.concatenate(parts, axis=1), wout_ref[...])
    x1 = x_ref[...] + gt1 * mix
    h = _rms_unit(x1) * gffn_ref[...]
    h = (h * (1.0 + sc2) + sh2).astype(BF16)
    a = _dot(h, wup_ref[...])
    w = 2 * D_FF
    gw = GATE_GROUP_W
    f_parts = []
    for j in range(D_FF // gw):
        halves = []
        for half in range(2):
            c0 = half * D_FF + j * gw
            a_blk = a[:, (2 * j + half) * gw:(2 * j + half + 1) * gw]
            buf0, buf1 = cbuf_ref[:, c0:c0 + gw], cbuf_ref[:, w + c0:w + c0 + gw]
            halves.append(convb_ref[:, c0:c0 + gw] + buf0 * convw_ref[0:1, c0:c0 + gw]
                          + buf1 * convw_ref[1:2, c0:c0 + gw] + a_blk * convw_ref[2:3, c0:c0 + gw])
            cnew_ref[:, c0:c0 + gw] = buf1
            cnew_ref[:, w + c0:w + c0 + gw] = a_blk
        f_parts.append((_silu(halves[0]) * halves[1]).astype(BF16))
    f = jnp.concatenate(f_parts, axis=1)
    x2 = x1 + gt2 * _dot(f, wdown_ref[...])
    if final_norm:
        x2 = _rms_unit(x2) * gfin_ref[...]
    y_ref[...] = x2


def _whole(shape):
    zeros = (0,) * len(shape)
    return pl.BlockSpec(shape, lambda *_: zeros)


def _layer_slab(shape, layer):
    zeros = (0,) * (len(shape) - 1)
    return pl.BlockSpec((None,) + tuple(shape[1:]), lambda *_: (layer,) + zeros)


def _sample_layer(layer, x, mod, state_ret, state_conv, ret_acc, conv_acc, lw, tabs, g_final, final_norm):
    n = x.shape[0]
    params = pltpu.CompilerParams(dimension_semantics=("arbitrary",), vmem_limit_bytes=V7X_VMEM_LIMIT_BYTES)
    act = jax.ShapeDtypeStruct((n, RET_W), F32)
    in_args = (x, mod, lw["g_mix"], lw["w_in"], tabs["cos_s"], tabs["sin_s"], lw["ln"], lw["ws0"], lw["bs0"])
    in_specs = [_whole(a.shape) for a in in_args]
    in_specs[1] = pl.BlockSpec((None, n, N_MOD * D_MODEL), lambda i: (layer, 0, 0))
    for i in (2, 3, 6, 7, 8):
        in_specs[i] = _layer_slab(in_args[i].shape, layer)
    q, k, v, sg, ogm, vn = pl.pallas_call(
        _sample_in_body,
        grid=(1,),
        in_specs=in_specs,
        out_specs=[_whole(act.shape)] * 6,
        out_shape=[act] * 6,
        compiler_params=params,
        name="sample_in",
    )(*in_args)

    sb = SAMPLE_SEQ_BLOCK
    row_spec = pl.BlockSpec((sb, RET_W), lambda i: (i, 0))
    state_spec = pl.BlockSpec((None, sb, N_HEADS, HEAD_W, HEAD_W), lambda i: (layer, i, 0, 0, 0))
    carried = [] if ret_acc is None else [ret_acc]
    depth = state_ret.shape[0]
    first = ret_acc is None
    all_layers_spec = pl.BlockSpec((depth, sb, N_HEADS, HEAD_W, HEAD_W), lambda i: (0, i, 0, 0, 0))
    o, ret_acc = pl.pallas_call(
        functools.partial(_sample_ret_body, sb=sb, layer=layer, first=first),
        grid=(n // sb,),
        in_specs=[row_spec, row_spec, row_spec, state_spec, _whole((1, RET_W))]
        + [pl.BlockSpec(memory_space=pl.ANY)] * len(carried),
        out_specs=[row_spec, all_layers_spec if first else state_spec],
        out_shape=[act, jax.ShapeDtypeStruct(state_ret.shape, F32)],
        input_output_aliases={5: 1} if carried else {},
        compiler_params=params,
        name="sample_ret",
    )(q, k, v, state_ret, tabs["gamma_s"], *carried)

    cbuf = state_conv.reshape(depth, n, (CONV_W - 1) * 2 * D_FF)
    if conv_acc is None:
        conv_acc = jnp.zeros(cbuf.shape, F32)
    out_args = (x, mod, o, sg, ogm, lw["gn"], lw["w_out"], lw["g_ffn"], lw["w_up"], cbuf,
                lw["conv_w"], lw["conv_b"], lw["w_down"], g_final)
    in_specs = [_whole(a.shape) for a in out_args]
    in_specs[1] = pl.BlockSpec((None, n, N_MOD * D_MODEL), lambda i: (layer, 0, 0))
    for i in (5, 6, 7, 8, 9, 10, 11, 12):
        in_specs[i] = _layer_slab(out_args[i].shape, layer)

    y, conv_acc = pl.pallas_call(
        functools.partial(_sample_out_body, final_norm=final_norm),
        grid=(1,),
        in_specs=in_specs + [pl.BlockSpec(memory_space=pl.ANY)],
        out_specs=[_whole((n, D_MODEL)), _layer_slab(cbuf.shape, layer)],
        out_shape=[jax.ShapeDtypeStruct((n, D_MODEL), F32), jax.ShapeDtypeStruct(cbuf.shape, F32)],
        input_output_aliases={len(out_args): 1},
        compiler_params=params,
        name="sample_out",
    )(*out_args, conv_acc)
    return y, ret_acc, conv_acc, vn


def _rope_tables(pos):
    half = HEAD_W // 2
    freqs = np.exp(-math.log(ROPE_BASE) * np.arange(half, dtype=np.float64) / half)
    ang = np.asarray(pos, dtype=np.float64)[:, None] * freqs[None, :]
    cos, sin = np.cos(ang), np.sin(ang)
    return (np.concatenate([cos, cos], axis=-1).astype(np.float32),
            np.concatenate([-sin, sin], axis=-1).astype(np.float32))


def _decay_tables(chunk):
    lg = np.log1p(-np.exp2(-5.0 - np.arange(N_HEADS, dtype=np.float64)))
    i = np.arange(chunk, dtype=np.float64)
    diff = i[:, None] - i[None, :]
    dmask = np.where(diff[None] >= 0.0, np.exp(np.maximum(diff, 0.0)[None] * lg[:, None, None]), 0.0)
    q_dec = np.exp((i[:, None] + 1.0) * lg[None, :])
    k_dec = np.exp((chunk - 1.0 - i)[:, None] * lg[None, :])
    chunk_dec = np.exp(chunk * lg)
    return tuple(a.astype(np.float32) for a in (dmask, q_dec, k_dec, chunk_dec))


def _per_head_lanes(a):
    return a.repeat(HEAD_W, axis=-1)


def kernel(x_prompt, x_sample, state_ret, state_conv, c_prompt, c_sample, w_ada, b_ada, g_mix, w_in,
           ret_gn_gain, gmlp_ln_gain, w_s, b_s, w_out, g_ffn, w_up, conv_w, conv_b, w_down, g_final):
    depth = w_in.shape[0]
    batch, seq, _ = x_prompt.shape
    n_dec, dec_seq, _ = x_sample.shape
    assert dec_seq == 1 and seq % PROMPT_TILE == 0 and PROMPT_TILE % CHUNK == 0

    cos_p, sin_p = _rope_tables(np.arange(seq))
    cos_s, sin_s = _rope_tables(PAST_LEN + np.arange(dec_seq))
    dmask, q_dec, k_dec, chunk_dec = _decay_tables(CHUNK)
    _, _, _, gamma_s = _decay_tables(dec_seq)
    tabs = {
        "cos": cos_p, "sin": sin_p, "cos_s": cos_s, "sin_s": sin_s,
        "dmask": dmask, "qdec": _per_head_lanes(q_dec), "kdec": _per_head_lanes(k_dec),
        "cdec": np.broadcast_to(chunk_dec[:, None, None], (N_HEADS, 1, HEAD_W)),
        "gamma_s": _per_head_lanes(gamma_s[None, :]),
    }
    tabs = {name: jnp.asarray(a) for name, a in tabs.items()}

    mod = _adaln(jnp.concatenate([c_sample, c_prompt], axis=0), w_ada, b_ada)
    mod_p = mod[:, n_dec:].reshape(depth, batch, N_MOD, D_MODEL)
    tril = np.tril(np.ones((CHUNK, CHUNK), dtype=bool))
    lw = {
        "g_mix": g_mix.reshape(depth, 1, D_MODEL), "g_ffn": g_ffn.reshape(depth, 1, D_MODEL),
        "w_in": _to_bf16(w_in), "w_out": _to_bf16(w_out),
        "w_up": _to_bf16(w_up, group_gate_columns=True), "w_down": _to_bf16(w_down),
        "gn": ret_gn_gain.reshape(depth, 1, RET_W), "ln": gmlp_ln_gain.reshape(depth, 1, GM_W),
        "ws_tril": jnp.where(tril, w_s, 0.0).astype(BF16),
        "bs_full": _per_head_lanes(b_s.transpose(0, 2, 1)),
        "ws0": _per_head_lanes(w_s[:, None, :, 0, 0]), "bs0": _per_head_lanes(b_s[:, None, :, 0]),
        "conv_w": conv_w, "conv_b": conv_b.reshape(depth, 1, 2 * D_FF),
    }
    g_fin = g_final.reshape(1, D_MODEL)

    xp, xs = x_prompt, x_sample.reshape(n_dec, D_MODEL)
    ret_p, conv_p, v_s = [], [], []
    ret_s = conv_s = None
    for l in range(depth):
        last = l == depth - 1
        xp, sp, bp = _prompt_layer(l, xp, mod_p, lw, tabs, g_fin, last)
        xs, ret_s, conv_s, vs = _sample_layer(l, xs, mod, state_ret, state_conv, ret_s, conv_s, lw, tabs,
                                              g_fin, last)
        ret_p.append(sp); conv_p.append(bp)
        v_s.append(vs.reshape(n_dec, dec_seq, GM_W))
    return (xp, xs.reshape(n_dec, dec_seq, D_MODEL), jnp.stack(ret_p), jnp.stack(conv_p),
            ret_s, conv_s.reshape(state_conv.shape), jnp.stack(v_s))
```

```python
import functools
import math

import jax
import jax.numpy as jnp
import numpy as np
from jax import lax
from jax.experimental import pallas as pl
from jax.experimental.pallas import tpu as pltpu

D_MODEL = 1024
N_HEADS = 4
HEAD_W = 128
RET_W = N_HEADS * HEAD_W
GM_W = N_HEADS * HEAD_W
IN_W = 4 * RET_W + 2 * GM_W
D_FF = 2048
CONV_W = 3
CHUNK = 128
ROPE_BASE = 10000.0
PAST_LEN = 16384
EPS = 1e-6
N_MOD = 6

V7X_VMEM_LIMIT_BYTES = 60 * 1024 * 1024
PROMPT_TILE = 512
ROW_BLOCK = 32
CONV_PAD = 8
GATE_GROUP_W = 512
SAMPLE_SEQ_BLOCK = 16

F32 = jnp.float32
BF16 = jnp.bfloat16
NT_DIMS = (((1,), (1,)), ((), ()))
TN_DIMS = (((0,), (0,)), ((), ()))


def _silu(x):
    return x * (1.0 / (1.0 + jnp.exp(-x)))


def _gelu_tanh(x):
    c = math.sqrt(2.0 / math.pi)
    return 0.5 * x * (1.0 + jnp.tanh(c * (x + 0.044715 * (x * x * x))))


def _rms_unit(x):
    return x * lax.rsqrt(jnp.mean(x * x, axis=-1, keepdims=True) + EPS)


def _center_unit(x):
    xc = x - jnp.mean(x, axis=-1, keepdims=True)
    return xc * lax.rsqrt(jnp.mean(xc * xc, axis=-1, keepdims=True) + EPS)


def _rope(x, cos_full, sin_signed):
    return x * cos_full + pltpu.roll(x, HEAD_W // 2, 1) * sin_signed


def _dot(a, b):
    return jnp.dot(a, b, preferred_element_type=F32)


NORM_BUNDLES = 60
RESID_BUNDLES = 30
CONV_BUNDLES = 170
HEAD_BUNDLES = 160
GROUP_BUNDLES = 130


def _two_unit_order(pieces, orders):
    unit_free = {unit: 0 for unit in orders}
    head = {unit: 0 for unit in orders}
    finish = {}
    order = []
    while any(head[unit] < len(names) for unit, names in orders.items()):
        ready = []
        for unit, names in orders.items():
            if head[unit] < len(names):
                name = names[head[unit]]
                deps = pieces[name][2]
                if all(d in finish for d in deps):
                    ready.append((max([unit_free[unit]] + [finish[d] for d in deps]), unit, name))
        assert ready, "piece lists contradict the dependencies"
        start, unit, name = min(ready)
        finish[name] = unit_free[unit] = start + pieces[name][1]
        head[unit] += 1
        order.append(name)
    return order


def _cast_body(w_ref, o_ref):
    o_ref[...] = w_ref[...].astype(BF16)


def _to_bf16(w, group_gate_columns=False):
    depth, k, n = w.shape
    bn = GATE_GROUP_W
    n_grp = n // (2 * bn)

    def src_block(c):
        return (c % 2) * n_grp + c // 2 if group_gate_columns else c

    return pl.pallas_call(
        _cast_body,
        grid=(depth, n // bn),
        in_specs=[pl.BlockSpec((None, k, bn), lambda l, c: (l, 0, src_block(c)))],
        out_specs=pl.BlockSpec((None, k, bn), lambda l, c: (l, 0, c)),
        out_shape=jax.ShapeDtypeStruct(w.shape, BF16),
        compiler_params=pltpu.CompilerParams(
            dimension_semantics=("arbitrary", "arbitrary"), vmem_limit_bytes=V7X_VMEM_LIMIT_BYTES),
        name="to_bf16",
    )(w)


def _adaln_body(c_ref, w_ref, b_ref, o_ref):
    c = _silu(c_ref[...]).astype(BF16)
    o_ref[...] = _dot(c, w_ref[...].astype(BF16)) + b_ref[...]


def _adaln(c_all, w_ada, b_ada):
    depth = w_ada.shape[0]
    rows = c_all.shape[0]
    n_out = w_ada.shape[2]
    bn = 1536
    return pl.pallas_call(
        _adaln_body,
        grid=(depth, n_out // bn),
        in_specs=[
            pl.BlockSpec((rows, D_MODEL), lambda l, j: (0, 0)),
            pl.BlockSpec((None, D_MODEL, bn), lambda l, j: (l, 0, j)),
            pl.BlockSpec((None, 1, bn), lambda l, j: (l, 0, j)),
        ],
        out_specs=pl.BlockSpec((None, rows, bn), lambda l, j: (l, 0, j)),
        out_shape=jax.ShapeDtypeStruct((depth, rows, n_out), F32),
        compiler_params=pltpu.CompilerParams(
            dimension_semantics=("arbitrary", "arbitrary"), vmem_limit_bytes=V7X_VMEM_LIMIT_BYTES),
        name="adaln",
    )(c_all, w_ada, b_ada.reshape(depth, 1, n_out))


def _prompt_layer_body(x_ref, modf_ref, modb_ref, gmix_ref, gffn_ref, win_ref, wout_ref, wup_ref, wdown_ref,
                       cos_ref, sin_ref, dmask_ref, qdec_ref, kdec_ref, cdec_ref, gn_ref, ln_ref,
                       ws_ref, bs_ref, convw_ref, convb_ref, gfin_ref,
                       y_ref, ret_ref, conv_ref,
                       h_ref, p_ref, mix_ref, mo_ref, x1_ref, h2_ref, f_ref, mo2_ref, *a_refs,
                       tm, tiles_per_seq, n_tiles, final_norm):
    s = pl.program_id(0)
    front_on = s < n_tiles
    t_front = lax.rem(jnp.minimum(s, n_tiles - 1), tiles_per_seq)
    t_back = lax.rem(jnp.maximum(s - 1, 0), tiles_per_seq)

    @pl.when(s == 0)
    def _():
        x1_ref[...] = jnp.zeros_like(x1_ref)
        h2_ref[...] = jnp.zeros_like(h2_ref)

    @pl.when(jnp.logical_and(front_on, t_front == 0))
    def _():
        ret_ref[...] = jnp.zeros_like(ret_ref)

    @pl.when(t_back == 0)
    def _():
        conv_ref[...] = jnp.zeros_like(conv_ref)

    sh1, sc1, gt1 = modf_ref[0:1, :], modf_ref[1:2, :], modf_ref[2:3, :]
    sh2, sc2 = modf_ref[3:4, :], modf_ref[4:5, :]
    gt2 = modb_ref[5:6, :]
    row_blocks = [slice(r * ROW_BLOCK, (r + 1) * ROW_BLOCK) for r in range(tm // ROW_BLOCK)]

    cb = GATE_GROUP_W
    tail = slice(CONV_PAD - (CONV_W - 1), CONV_PAD)

    pieces = {}
    n_rb = len(row_blocks)
    n_grp = D_FF // cb
    n_chunks = tm // CHUNK

    def matmul_bundles(k, n):
        return (tm // 16) * (k // 256) * (n // 256) * 8 // 2

    def up_group(j):
        for half in range(2):
            c0 = half * D_FF + j * cb
            a_refs[j][tail, half * cb:(half + 1) * cb] = conv_ref[:, c0:c0 + cb]
        a_refs[j][CONV_PAD:CONV_PAD + tm, :] = _dot(h2_ref[...], wup_ref[:, 2 * j * cb:2 * (j + 1) * cb])
        for half in range(2):
            c0 = half * D_FF + j * cb
            conv_ref[:, c0:c0 + cb] = a_refs[j][CONV_PAD + tm - (CONV_W - 1):CONV_PAD + tm,
                                                half * cb:(half + 1) * cb]

    def conv_rows(j, r):
        halves = []
        for half in range(2):
            c0 = half * D_FF + j * cb
            src = slice(half * cb, (half + 1) * cb)
            window = a_refs[j][r * ROW_BLOCK:CONV_PAD + (r + 1) * ROW_BLOCK, src]
            acc = convb_ref[:, c0:c0 + cb]
            for tap in range(CONV_W):
                back = CONV_W - 1 - tap
                rows_back = pltpu.roll(window, back, 0) if back else window
                acc = acc + rows_back[CONV_PAD:, :] * convw_ref[tap:tap + 1, c0:c0 + cb]
            halves.append(acc)
        f_ref[row_blocks[r], j * cb:(j + 1) * cb] = (_silu(halves[0]) * halves[1]).astype(BF16)

    def down_proj():
        mo2_ref[...] = _dot(f_ref[...], wdown_ref[...])

    def resid_out(r):
        rows = row_blocks[r]
        x2 = y_ref[rows, :] + gt2 * mo2_ref[rows, :]
        if final_norm:
            x2 = _rms_unit(x2) * gfin_ref[...]
        y_ref[rows, :] = x2

    for j in range(n_grp):
        pieces[f"up{j}"] = ("mxu", matmul_bundles(D_MODEL, 2 * cb), (), functools.partial(up_group, j))
        for r in range(n_rb):
            pieces[f"conv{j}.{r}"] = ("valu", CONV_BUNDLES, (f"up{j}",), functools.partial(conv_rows, j, r))
    all_conv = tuple(f"conv{j}.{r}" for j in range(n_grp) for r in range(n_rb))
    pieces["down"] = ("mxu", matmul_bundles(D_FF, D_MODEL), all_conv, down_proj)
    for r in range(n_rb):
        pieces[f"y{r}"] = ("valu", RESID_BUNDLES, ("down",), functools.partial(resid_out, r))

    def norm_rows(r):
        rows = row_blocks[r]
        h = _rms_unit(x_ref[rows, :]) * gmix_ref[...]
        h_ref[rows, :] = (h * (1.0 + sc1) + sh1).astype(BF16)

    def in_cols(c0, c1):
        p_ref[:, c0:c1] = _dot(h_ref[...], win_ref[:, c0:c1])

    all_norm = tuple(f"norm{r}" for r in range(n_rb))
    for r in range(n_rb):
        pieces[f"norm{r}"] = ("valu", NORM_BUNDLES, (), functools.partial(norm_rows, r))
    pieces["in_ret"] = ("mxu", matmul_bundles(D_MODEL, 4 * RET_W), all_norm, functools.partial(in_cols, 0, 4 * RET_W))
    pieces["in_gm"] = ("mxu", matmul_bundles(D_MODEL, 2 * GM_W), all_norm,
                       functools.partial(in_cols, 4 * RET_W, IN_W))

    def mix_head(c, hh):
        rows = slice(c * CHUNK, (c + 1) * CHUNK)
        cs = slice(hh * HEAD_W, (hh + 1) * HEAD_W)
        cos_full, sin_signed = cos_ref[rows, :], sin_ref[rows, :]
        q = _rope(p_ref[rows, hh * HEAD_W:(hh + 1) * HEAD_W], cos_full, sin_signed)
        k = _rope(p_ref[rows, RET_W + hh * HEAD_W:RET_W + (hh + 1) * HEAD_W], cos_full, sin_signed)
        k = k * (HEAD_W ** -0.5)
        vb = p_ref[rows, 2 * RET_W + hh * HEAD_W:2 * RET_W + (hh + 1) * HEAD_W].astype(BF16)
        g = p_ref[rows, 3 * RET_W + hh * HEAD_W:3 * RET_W + (hh + 1) * HEAD_W]
        qb = q.astype(BF16)
        qdb = (q * qdec_ref[:, cs]).astype(BF16)
        kb = k.astype(BF16)
        kdb = (k * kdec_ref[:, cs]).astype(BF16)
        scores = lax.dot_general(qb, kb, NT_DIMS, preferred_element_type=F32)
        att = (scores * dmask_ref[hh]).astype(BF16)
        state = ret_ref[hh]
        o = _dot(jnp.concatenate([att, qdb], axis=1), jnp.concatenate([vb, state.astype(BF16)], axis=0))
        new_state = cdec_ref[hh] * state + lax.dot_general(kdb, vb, TN_DIMS, preferred_element_type=F32)
        ret_ref[hh] = jnp.where(front_on, new_state, state)
        on = _center_unit(o) * gn_ref[:, cs]
        mix_ref[rows, cs] = (_silu(g) * on).astype(BF16)

    def mix_group(c, gg):
        rows = slice(c * CHUNK, (c + 1) * CHUNK)
        cs = slice(gg * HEAD_W, (gg + 1) * HEAD_W)
        u = _gelu_tanh(p_ref[rows, 4 * RET_W + gg * HEAD_W:4 * RET_W + (gg + 1) * HEAD_W])
        vv = _gelu_tanh(p_ref[rows, 4 * RET_W + GM_W + gg * HEAD_W:4 * RET_W + GM_W + (gg + 1) * HEAD_W])
        vn = _center_unit(vv) * ln_ref[:, cs]
        sg = _dot(ws_ref[gg], vn.astype(BF16)) + bs_ref[:, cs]
        mix_ref[rows, RET_W + gg * HEAD_W:RET_W + (gg + 1) * HEAD_W] = (u * sg).astype(BF16)

    def out_proj():
        mo_ref[...] = _dot(mix_ref[...], wout_ref[...])

    def resid_mid(r):
        rows = row_blocks[r]
        x1 = x_ref[rows, :] + gt1 * mo_ref[rows, :]
        x1_ref[rows, :] = x1
        h = _rms_unit(x1) * gffn_ref[...]
        h2_ref[rows, :] = (h * (1.0 + sc2) + sh2).astype(BF16)

    all_mix = []
    for c in range(n_chunks):
        for hh in range(N_HEADS):
            pieces[f"head{c}.{hh}"] = ("valu", HEAD_BUNDLES, ("in_ret",), functools.partial(mix_head, c, hh))
            pieces[f"group{c}.{hh}"] = ("valu", GROUP_BUNDLES, ("in_gm",), functools.partial(mix_group, c, hh))
            all_mix += [f"head{c}.{hh}", f"group{c}.{hh}"]
    pieces["out"] = ("mxu", matmul_bundles(RET_W + GM_W, D_MODEL), tuple(all_mix), out_proj)
    all_up = tuple(f"up{j}" for j in range(n_grp))
    for r in range(n_rb):
        pieces[f"x1{r}"] = ("valu", RESID_BUNDLES + NORM_BUNDLES, ("out",) + all_up, functools.partial(resid_mid, r))

    orders = {
        "mxu": list(all_up) + ["in_ret", "in_gm", "down", "out"],
        "valu": list(all_norm) + list(all_conv)
        + [f"{kind}{c}.{i}" for c in range(n_chunks) for kind in ("head", "group") for i in range(N_HEADS)]
        + [f"y{r}" for r in range(n_rb)] + [f"x1{r}" for r in range(n_rb)],
    }
    y_ref[...] = x1_ref[...]
    for name in _two_unit_order(pieces, orders):
        pieces[name][3]()


def _const_spec(shape):
    zeros = (0,) * len(shape)
    return pl.BlockSpec(shape, lambda s: zeros, pipeline_mode=pl.Buffered(1))


def _prompt_layer(layer, x, mod, lw, tabs, g_final, final_norm):
    batch, seq, _ = x.shape

    def layer_spec(name):
        shape = lw[name].shape
        zeros = (0,) * (len(shape) - 1)
        return pl.BlockSpec((None,) + tuple(shape[1:]), lambda s: (layer,) + zeros, pipeline_mode=pl.Buffered(1))

    tm = PROMPT_TILE
    tiles_per_seq = seq // tm
    n_tiles = batch * tiles_per_seq
    body = functools.partial(_prompt_layer_body, tm=tm, tiles_per_seq=tiles_per_seq, n_tiles=n_tiles,
                             final_norm=final_norm)

    def front(s):
        i = jnp.minimum(s, n_tiles - 1)
        return i // tiles_per_seq, lax.rem(i, tiles_per_seq)

    def back(s):
        i = jnp.maximum(s - 1, 0)
        return i // tiles_per_seq, lax.rem(i, tiles_per_seq)

    in_specs = [
        pl.BlockSpec((None, tm, D_MODEL), lambda s: (*front(s), 0)),
        pl.BlockSpec((None, None, N_MOD, D_MODEL), lambda s: (layer, front(s)[0], 0, 0)),
        pl.BlockSpec((None, None, N_MOD, D_MODEL), lambda s: (layer, back(s)[0], 0, 0)),
        layer_spec("g_mix"), layer_spec("g_ffn"),
        layer_spec("w_in"), layer_spec("w_out"), layer_spec("w_up"), layer_spec("w_down"),
        pl.BlockSpec((tm, HEAD_W), lambda s: (front(s)[1], 0)),
        pl.BlockSpec((tm, HEAD_W), lambda s: (front(s)[1], 0)),
        _const_spec((N_HEADS, CHUNK, CHUNK)), _const_spec((CHUNK, RET_W)), _const_spec((CHUNK, RET_W)),
        _const_spec((N_HEADS, 1, HEAD_W)), layer_spec("gn"), layer_spec("ln"),
        layer_spec("ws_tril"), layer_spec("bs_full"),
        layer_spec("conv_w"), layer_spec("conv_b"), _const_spec((1, D_MODEL)),
    ]
    out_specs = [
        pl.BlockSpec((None, tm, D_MODEL), lambda s: (*back(s), 0)),
        pl.BlockSpec((None, N_HEADS, HEAD_W, HEAD_W), lambda s: (front(s)[0], 0, 0, 0)),
        pl.BlockSpec((None, CONV_W - 1, 2 * D_FF), lambda s: (back(s)[0], 0, 0)),
    ]
    out_shape = [
        jax.ShapeDtypeStruct((batch, seq, D_MODEL), F32),
        jax.ShapeDtypeStruct((batch, N_HEADS, HEAD_W, HEAD_W), F32),
        jax.ShapeDtypeStruct((batch, CONV_W - 1, 2 * D_FF), F32),
    ]
    scratch = [
        pltpu.VMEM((tm, D_MODEL), BF16),
        pltpu.VMEM((tm, IN_W), F32),
        pltpu.VMEM((tm, RET_W + GM_W), BF16),
        pltpu.VMEM((tm, D_MODEL), F32),
        pltpu.VMEM((tm, D_MODEL), F32),
        pltpu.VMEM((tm, D_MODEL), BF16),
        pltpu.VMEM((tm, D_FF), BF16),
        pltpu.VMEM((tm, D_MODEL), F32),
    ] + [pltpu.VMEM((CONV_PAD + tm, 2 * 512), F32)] * (D_FF // 512)
    return pl.pallas_call(
        body,
        grid=(n_tiles + 1,),
        in_specs=in_specs,
        out_specs=out_specs,
        out_shape=out_shape,
        scratch_shapes=scratch,
        compiler_params=pltpu.CompilerParams(
            dimension_semantics=("arbitrary",), vmem_limit_bytes=V7X_VMEM_LIMIT_BYTES),
        name="prompt_layer",
    )(x, mod, mod, lw["g_mix"], lw["g_ffn"], lw["w_in"], lw["w_out"], lw["w_up"], lw["w_down"],
      tabs["cos"], tabs["sin"], tabs["dmask"], tabs["qdec"], tabs["kdec"], tabs["cdec"],
      lw["gn"], lw["ln"], lw["ws_tril"], lw["bs_full"], lw["conv_w"], lw["conv_b"], g_final)


def _sample_in_body(x_ref, mod_ref, gmix_ref, win_ref, cos_ref, sin_ref, ln_ref, ws0_ref, bs0_ref,
                    q_ref, k_ref, v_ref, sg_ref, ogm_ref, vn_ref):
    sh1, sc1 = mod_ref[:, 0:D_MODEL], mod_ref[:, D_MODEL:2 * D_MODEL]
    h = _rms_unit(x_ref[...]) * gmix_ref[...]
    h = (h * (1.0 + sc1) + sh1).astype(BF16)
    p = _dot(h, win_ref[...])
    for hh in range(N_HEADS):
        cs = slice(hh * HEAD_W, (hh + 1) * HEAD_W)
        q_ref[:, cs] = _rope(p[:, hh * HEAD_W:(hh + 1) * HEAD_W], cos_ref[...], sin_ref[...])
        k_ref[:, cs] = _rope(p[:, RET_W + hh * HEAD_W:RET_W + (hh + 1) * HEAD_W],
                             cos_ref[...], sin_ref[...]) * (HEAD_W ** -0.5)
        u = _gelu_tanh(p[:, 4 * RET_W + hh * HEAD_W:4 * RET_W + (hh + 1) * HEAD_W])
        vv = _gelu_tanh(p[:, 4 * RET_W + GM_W + hh * HEAD_W:4 * RET_W + GM_W + (hh + 1) * HEAD_W])
        vn = _center_unit(vv) * ln_ref[:, cs]
        vn_ref[:, cs] = vn
        ogm_ref[:, cs] = u * (ws0_ref[:, cs] * vn + bs0_ref[:, cs])
    v_ref[...] = p[:, 2 * RET_W:3 * RET_W]
    sg_ref[...] = _silu(p[:, 3 * RET_W:4 * RET_W])


def _sample_ret_body(q_ref, k_ref, v_ref, s_ref, gam_ref, *rest, sb, layer, first):
    o_ref, snew_ref = rest[-2:]
    if first:
        for other in range(snew_ref.shape[0]):
            if other != layer:
                snew_ref[other] = jnp.zeros(snew_ref.shape[1:], F32)
        snew_ref = snew_ref.at[layer]
    for hh in range(N_HEADS):
        cs = slice(hh * HEAD_W, (hh + 1) * HEAD_W)
        gam = gam_ref[:, cs]
        v = v_ref[:, cs]
        k = k_ref[:, cs]
        qb = q_ref[:, cs].astype(BF16)
        kt = jnp.concatenate([k, jnp.zeros((HEAD_W - sb, HEAD_W), F32)], axis=0).T
        for s in range(sb):
            state = s_ref[s, hh]
            snew_ref[s, hh] = gam * state + kt[:, s:s + 1] * v[s:s + 1, :]
            o_ref[s:s + 1, cs] = gam * _dot(qb, state.astype(BF16))[s:s + 1, :]
        qk = jnp.sum(q_ref[:, cs] * k, axis=-1, keepdims=True)
        o_ref[:, cs] = o_ref[:, cs] + qk * v


def _sample_out_body(x_ref, mod_ref, o_ref, sg_ref, ogm_ref, gn_ref, wout_ref, gffn_ref, wup_ref,
                     cbuf_ref, convw_ref, convb_ref, wdown_ref, gfin_ref, acc_ref, y_ref, cnew_ref,
                     *, final_norm):
    del acc_ref
    gt1 = mod_ref[:, 2 * D_MODEL:3 * D_MODEL]
    sh2, sc2 = mod_ref[:, 3 * D_MODEL:4 * D_MODEL], mod_ref[:, 4 * D_MODEL:5 * D_MODEL]
    gt2 = mod_ref[:, 5 * D_MODEL:6 * D_MODEL]
    parts = []
    for hh in range(N_HEADS):
        cs = slice(hh * HEAD_W, (hh + 1) * HEAD_W)
        on = _center_unit(o_ref[:, cs]) * gn_ref[:, cs]
        parts.append((sg_ref[:, cs] * on).astype(BF16))
    parts.append(ogm_ref[...].astype(BF16))
    mix = _dot(jnp.concatenate(parts, axis=1), wout_ref[...])
    x1 = x_ref[...] + gt1 * mix
    h = _rms_unit(x1) * gffn_ref[...]
    h = (h * (1.0 + sc2) + sh2).astype(BF16)
    a = _dot(h, wup_ref[...])
    gw = GATE_GROUP_W
    f_parts = []
    for j in range(D_FF // gw):
        halves = []
        for half in range(2):
            c0 = half * D_FF + j * gw
            a_blk = a[:, (2 * j + half) * gw:(2 * j + half + 1) * gw]
            buf0, buf1 = cbuf_ref[:, 0, c0:c0 + gw], cbuf_ref[:, 1, c0:c0 + gw]
            halves.append(convb_ref[:, c0:c0 + gw] + buf0 * convw_ref[0:1, c0:c0 + gw]
                          + buf1 * convw_ref[1:2, c0:c0 + gw] + a_blk * convw_ref[2:3, c0:c0 + gw])
            cnew_ref[:, 0, c0:c0 + gw] = buf1
            cnew_ref[:, 1, c0:c0 + gw] = a_blk
        f_parts.append((_silu(halves[0]) * halves[1]).astype(BF16))
    f = jnp.concatenate(f_parts, axis=1)
    x2 = x1 + gt2 * _dot(f, wdown_ref[...])
    if final_norm:
        x2 = _rms_unit(x2) * gfin_ref[...]
    y_ref[...] = x2


def _whole(shape, pipeline_mode=None):
    zeros = (0,) * len(shape)
    return pl.BlockSpec(shape, lambda *_: zeros, pipeline_mode=pipeline_mode)


def _layer_slab(shape, layer, pipeline_mode=None):
    zeros = (0,) * (len(shape) - 1)
    return pl.BlockSpec((None,) + tuple(shape[1:]), lambda *_: (layer,) + zeros, pipeline_mode=pipeline_mode)


def _sample_layer(layer, x, mod, state_ret, state_conv, ret_acc, conv_acc, lw, tabs, g_final, final_norm):
    n = x.shape[0]
    params = pltpu.CompilerParams(dimension_semantics=("arbitrary",), vmem_limit_bytes=V7X_VMEM_LIMIT_BYTES)
    act = jax.ShapeDtypeStruct((n, RET_W), F32)
    in_args = (x, mod, lw["g_mix"], lw["w_in"], tabs["cos_s"], tabs["sin_s"], lw["ln"], lw["ws0"], lw["bs0"])
    in_specs = [_whole(a.shape) for a in in_args]
    in_specs[1] = pl.BlockSpec((None, n, N_MOD * D_MODEL), lambda i: (layer, 0, 0))
    for i in (2, 3, 6, 7, 8):
        in_specs[i] = _layer_slab(in_args[i].shape, layer)
    q, k, v, sg, ogm, vn = pl.pallas_call(
        _sample_in_body,
        grid=(1,),
        in_specs=in_specs,
        out_specs=[_whole(act.shape)] * 6,
        out_shape=[act] * 6,
        compiler_params=params,
        name="sample_in",
    )(*in_args)

    sb = SAMPLE_SEQ_BLOCK
    row_spec = pl.BlockSpec((sb, RET_W), lambda i: (i, 0))
    state_spec = pl.BlockSpec((None, sb, N_HEADS, HEAD_W, HEAD_W), lambda i: (layer, i, 0, 0, 0))
    carried = [] if ret_acc is None else [ret_acc]
    depth = state_ret.shape[0]
    first = ret_acc is None
    all_layers_spec = pl.BlockSpec((depth, sb, N_HEADS, HEAD_W, HEAD_W), lambda i: (0, i, 0, 0, 0))
    o, ret_acc = pl.pallas_call(
        functools.partial(_sample_ret_body, sb=sb, layer=layer, first=first),
        grid=(n // sb,),
        in_specs=[row_spec, row_spec, row_spec, state_spec, _whole((1, RET_W))]
        + [pl.BlockSpec(memory_space=pl.ANY)] * len(carried),
        out_specs=[row_spec, all_layers_spec if first else state_spec],
        out_shape=[act, jax.ShapeDtypeStruct(state_ret.shape, F32)],
        input_output_aliases={5: 1} if carried else {},
        compiler_params=params,
        name="sample_ret",
    )(q, k, v, state_ret, tabs["gamma_s"], *carried)

    if conv_acc is None:
        conv_acc = jnp.zeros(state_conv.shape, F32)
    out_args = (x, mod, o, sg, ogm, lw["gn"], lw["w_out"], lw["g_ffn"], lw["w_up"], state_conv,
                lw["conv_w"], lw["conv_b"], lw["w_down"], g_final)
    once = pl.Buffered(1)
    in_specs = [_whole(a.shape, once) for a in out_args]
    in_specs[1] = pl.BlockSpec((None, n, N_MOD * D_MODEL), lambda i: (layer, 0, 0), pipeline_mode=once)
    for i in (5, 6, 7, 8, 9, 10, 11, 12):
        in_specs[i] = _layer_slab(out_args[i].shape, layer, once)

    y, conv_acc = pl.pallas_call(
        functools.partial(_sample_out_body, final_norm=final_norm),
        grid=(1,),
        in_specs=in_specs + [pl.BlockSpec(memory_space=pl.ANY)],
        out_specs=[_whole((n, D_MODEL)), _layer_slab(state_conv.shape, layer, once)],
        out_shape=[jax.ShapeDtypeStruct((n, D_MODEL), F32), jax.ShapeDtypeStruct(state_conv.shape, F32)],
        input_output_aliases={len(out_args): 1},
        compiler_params=params,
        name="sample_out",
    )(*out_args, conv_acc)
    return y, ret_acc, conv_acc, vn


def _rope_tables(pos):
    half = HEAD_W // 2
    freqs = np.exp(-math.log(ROPE_BASE) * np.arange(half, dtype=np.float64) / half)
    ang = np.asarray(pos, dtype=np.float64)[:, None] * freqs[None, :]
    cos, sin = np.cos(ang), np.sin(ang)
    return (np.concatenate([cos, cos], axis=-1).astype(np.float32),
            np.concatenate([-sin, sin], axis=-1).astype(np.float32))


def _decay_tables(chunk):
    lg = np.log1p(-np.exp2(-5.0 - np.arange(N_HEADS, dtype=np.float64)))
    i = np.arange(chunk, dtype=np.float64)
    diff = i[:, None] - i[None, :]
    dmask = np.where(diff[None] >= 0.0, np.exp(np.maximum(diff, 0.0)[None] * lg[:, None, None]), 0.0)
    q_dec = np.exp((i[:, None] + 1.0) * lg[None, :])
    k_dec = np.exp((chunk - 1.0 - i)[:, None] * lg[None, :])
    chunk_dec = np.exp(chunk * lg)
    return tuple(a.astype(np.float32) for a in (dmask, q_dec, k_dec, chunk_dec))


def _per_head_lanes(a):
    return a.repeat(HEAD_W, axis=-1)


def kernel(x_prompt, x_sample, state_ret, state_conv, c_prompt, c_sample, w_ada, b_ada, g_mix, w_in,
           ret_gn_gain, gmlp_ln_gain, w_s, b_s, w_out, g_ffn, w_up, conv_w, conv_b, w_down, g_final):
    depth = w_in.shape[0]
    batch, seq, _ = x_prompt.shape
    n_dec, dec_seq, _ = x_sample.shape
    assert dec_seq == 1 and seq % PROMPT_TILE == 0 and PROMPT_TILE % CHUNK == 0

    cos_p, sin_p = _rope_tables(np.arange(seq))
    cos_s, sin_s = _rope_tables(PAST_LEN + np.arange(dec_seq))
    dmask, q_dec, k_dec, chunk_dec = _decay_tables(CHUNK)
    _, _, _, gamma_s = _decay_tables(dec_seq)
    tabs = {
        "cos": cos_p, "sin": sin_p, "cos_s": cos_s, "sin_s": sin_s,
        "dmask": dmask, "qdec": _per_head_lanes(q_dec), "kdec": _per_head_lanes(k_dec),
        "cdec": np.broadcast_to(chunk_dec[:, None, None], (N_HEADS, 1, HEAD_W)),
        "gamma_s": _per_head_lanes(gamma_s[None, :]),
    }
    tabs = {name: jnp.asarray(a) for name, a in tabs.items()}

    mod = _adaln(jnp.concatenate([c_sample, c_prompt], axis=0), w_ada, b_ada)
    mod_p = mod[:, n_dec:].reshape(depth, batch, N_MOD, D_MODEL)
    tril = np.tril(np.ones((CHUNK, CHUNK), dtype=bool))
    lw = {
        "g_mix": g_mix.reshape(depth, 1, D_MODEL), "g_ffn": g_ffn.reshape(depth, 1, D_MODEL),
        "w_in": _to_bf16(w_in), "w_out": _to_bf16(w_out),
        "w_up": _to_bf16(w_up, group_gate_columns=True), "w_down": _to_bf16(w_down),
        "gn": ret_gn_gain.reshape(depth, 1, RET_W), "ln": gmlp_ln_gain.reshape(depth, 1, GM_W),
        "ws_tril": jnp.where(tril, w_s, 0.0).astype(BF16),
        "bs_full": _per_head_lanes(b_s.transpose(0, 2, 1)),
        "ws0": _per_head_lanes(w_s[:, None, :, 0, 0]), "bs0": _per_head_lanes(b_s[:, None, :, 0]),
        "conv_w": conv_w, "conv_b": conv_b.reshape(depth, 1, 2 * D_FF),
    }
    g_fin = g_final.reshape(1, D_MODEL)

    xp, xs = x_prompt, x_sample.reshape(n_dec, D_MODEL)
    ret_p, conv_p, v_s = [], [], []
    ret_s = conv_s = None
    for l in range(depth):
        last = l == depth - 1
        xp, sp, bp = _prompt_layer(l, xp, mod_p, lw, tabs, g_fin, last)
        xs, ret_s, conv_s, vs = _sample_layer(l, xs, mod, state_ret, state_conv, ret_s, conv_s, lw, tabs,
                                              g_fin, last)
        ret_p.append(sp); conv_p.append(bp)
        v_s.append(vs.reshape(n_dec, dec_seq, GM_W))
    return (xp, xs.reshape(n_dec, dec_seq, D_MODEL), jnp.stack(ret_p), jnp.stack(conv_p),
            ret_s, conv_s, jnp.stack(v_s))
```

```python
import functools
import math

import jax
import jax.numpy as jnp
import numpy as np
from jax import lax
from jax.experimental import pallas as pl
from jax.experimental.pallas import tpu as pltpu

D_MODEL = 1024
N_HEADS = 4
HEAD_W = 128
RET_W = N_HEADS * HEAD_W
GM_W = N_HEADS * HEAD_W
IN_W = 4 * RET_W + 2 * GM_W
D_FF = 2048
CONV_W = 3
CHUNK = 128
ROPE_BASE = 10000.0
PAST_LEN = 16384
EPS = 1e-6
N_MOD = 6

V7X_VMEM_LIMIT_BYTES = 60 * 1024 * 1024
PROMPT_TILE = 512
ROW_BLOCK = 32
CONV_PAD = 8
GATE_GROUP_W = 512
DOWN_SPLIT_GROUPS = 2
SAMPLE_SEQ_BLOCK = 16

F32 = jnp.float32
BF16 = jnp.bfloat16
NT_DIMS = (((1,), (1,)), ((), ()))
TN_DIMS = (((0,), (0,)), ((), ()))


def _silu(x):
    return x * (1.0 / (1.0 + jnp.exp(-x)))


def _gelu_tanh(x):
    c = math.sqrt(2.0 / math.pi)
    return 0.5 * x * (1.0 + jnp.tanh(c * (x + 0.044715 * (x * x * x))))


def _rms_unit(x):
    return x * lax.rsqrt(jnp.mean(x * x, axis=-1, keepdims=True) + EPS)


def _center_unit(x):
    xc = x - jnp.mean(x, axis=-1, keepdims=True)
    return xc * lax.rsqrt(jnp.mean(xc * xc, axis=-1, keepdims=True) + EPS)


def _rope(x, cos_full, sin_signed):
    return x * cos_full + pltpu.roll(x, HEAD_W // 2, 1) * sin_signed


def _dot(a, b):
    return jnp.dot(a, b, preferred_element_type=F32)


NORM_BUNDLES = 60
RESID_BUNDLES = 30
CONV_BUNDLES = 170
HEAD_BUNDLES = 160
GROUP_BUNDLES = 130


def _two_unit_order(pieces, orders):
    unit_free = {unit: 0 for unit in orders}
    head = {unit: 0 for unit in orders}
    finish = {}
    order = []
    while any(head[unit] < len(names) for unit, names in orders.items()):
        ready = []
        for unit, names in orders.items():
            if head[unit] < len(names):
                name = names[head[unit]]
                deps = pieces[name][2]
                if all(d in finish for d in deps):
                    ready.append((max([unit_free[unit]] + [finish[d] for d in deps]), unit, name))
        assert ready, "piece lists contradict the dependencies"
        start, unit, name = min(ready)
        finish[name] = unit_free[unit] = start + pieces[name][1]
        head[unit] += 1
        order.append(name)
    return order


def _cast_body(w_ref, o_ref):
    o_ref[...] = w_ref[...].astype(BF16)


def _to_bf16(w, group_gate_columns=False):
    depth, k, n = w.shape
    bn = GATE_GROUP_W
    n_grp = n // (2 * bn)

    def src_block(c):
        return (c % 2) * n_grp + c // 2 if group_gate_columns else c

    return pl.pallas_call(
        _cast_body,
        grid=(depth, n // bn),
        in_specs=[pl.BlockSpec((None, k, bn), lambda l, c: (l, 0, src_block(c)))],
        out_specs=pl.BlockSpec((None, k, bn), lambda l, c: (l, 0, c)),
        out_shape=jax.ShapeDtypeStruct(w.shape, BF16),
        compiler_params=pltpu.CompilerParams(
            dimension_semantics=("arbitrary", "arbitrary"), vmem_limit_bytes=V7X_VMEM_LIMIT_BYTES),
        name="to_bf16",
    )(w)


def _adaln_body(c_ref, w_ref, b_ref, o_ref):
    c = _silu(c_ref[...]).astype(BF16)
    o_ref[...] = _dot(c, w_ref[...].astype(BF16)) + b_ref[...]


def _adaln(c_all, w_ada, b_ada):
    depth = w_ada.shape[0]
    rows = c_all.shape[0]
    n_out = w_ada.shape[2]
    bn = 1536
    return pl.pallas_call(
        _adaln_body,
        grid=(depth, n_out // bn),
        in_specs=[
            pl.BlockSpec((rows, D_MODEL), lambda l, j: (0, 0)),
            pl.BlockSpec((None, D_MODEL, bn), lambda l, j: (l, 0, j)),
            pl.BlockSpec((None, 1, bn), lambda l, j: (l, 0, j)),
        ],
        out_specs=pl.BlockSpec((None, rows, bn), lambda l, j: (l, 0, j)),
        out_shape=jax.ShapeDtypeStruct((depth, rows, n_out), F32),
        compiler_params=pltpu.CompilerParams(
            dimension_semantics=("arbitrary", "arbitrary"), vmem_limit_bytes=V7X_VMEM_LIMIT_BYTES),
        name="adaln",
    )(c_all, w_ada, b_ada.reshape(depth, 1, n_out))


def _prompt_layer_body(x_ref, modf_ref, modb_ref, gmix_ref, gffn_ref, win_ref, wout_ref, wup_ref, wdown_ref,
                       cos_ref, sin_ref, dmask_ref, qdec_ref, kdec_ref, cdec_ref, gn_ref, ln_ref,
                       ws_ref, bs_ref, convw_ref, convb_ref, gfin_ref,
                       y_ref, ret_ref, conv_ref,
                       h_ref, p_ref, mix_ref, mo_ref, x1_ref, h2_ref, f_ref, mo2_ref, *a_refs,
                       tm, tiles_per_seq, n_tiles, final_norm):
    s = pl.program_id(0)
    front_on = s < n_tiles
    t_front = lax.rem(jnp.minimum(s, n_tiles - 1), tiles_per_seq)
    t_back = lax.rem(jnp.maximum(s - 1, 0), tiles_per_seq)

    @pl.when(s == 0)
    def _():
        x1_ref[...] = jnp.zeros_like(x1_ref)
        h2_ref[...] = jnp.zeros_like(h2_ref)

    @pl.when(jnp.logical_and(front_on, t_front == 0))
    def _():
        ret_ref[...] = jnp.zeros_like(ret_ref)

    @pl.when(t_back == 0)
    def _():
        conv_ref[...] = jnp.zeros_like(conv_ref)

    sh1, sc1, gt1 = modf_ref[0:1, :], modf_ref[1:2, :], modf_ref[2:3, :]
    sh2, sc2 = modf_ref[3:4, :], modf_ref[4:5, :]
    gt2 = modb_ref[5:6, :]
    row_blocks = [slice(r * ROW_BLOCK, (r + 1) * ROW_BLOCK) for r in range(tm // ROW_BLOCK)]

    cb = GATE_GROUP_W
    tail = slice(CONV_PAD - (CONV_W - 1), CONV_PAD)

    pieces = {}
    n_rb = len(row_blocks)
    n_grp = D_FF // cb
    n_chunks = tm // CHUNK

    def matmul_bundles(k, n):
        return (tm // 16) * (k // 256) * (n // 256) * 8 // 2

    def up_group(j):
        for half in range(2):
            c0 = half * D_FF + j * cb
            a_refs[j][tail, half * cb:(half + 1) * cb] = conv_ref[:, c0:c0 + cb]
        a_refs[j][CONV_PAD:CONV_PAD + tm, :] = _dot(h2_ref[...], wup_ref[:, 2 * j * cb:2 * (j + 1) * cb])
        for half in range(2):
            c0 = half * D_FF + j * cb
            conv_ref[:, c0:c0 + cb] = a_refs[j][CONV_PAD + tm - (CONV_W - 1):CONV_PAD + tm,
                                                half * cb:(half + 1) * cb]

    def conv_rows(j, r):
        halves = []
        for half in range(2):
            c0 = half * D_FF + j * cb
            src = slice(half * cb, (half + 1) * cb)
            window = a_refs[j][r * ROW_BLOCK:CONV_PAD + (r + 1) * ROW_BLOCK, src]
            acc = convb_ref[:, c0:c0 + cb]
            for tap in range(CONV_W):
                back = CONV_W - 1 - tap
                rows_back = pltpu.roll(window, back, 0) if back else window
                acc = acc + rows_back[CONV_PAD:, :] * convw_ref[tap:tap + 1, c0:c0 + cb]
            halves.append(acc)
        f_ref[row_blocks[r], j * cb:(j + 1) * cb] = (_silu(halves[0]) * halves[1]).astype(BF16)

    def down_proj(g0, g1):
        part = _dot(f_ref[:, g0 * cb:g1 * cb], wdown_ref[g0 * cb:g1 * cb, :])
        mo2_ref[...] = part if g0 == 0 else mo2_ref[...] + part

    def resid_out(r):
        rows = row_blocks[r]
        x2 = y_ref[rows, :] + gt2 * mo2_ref[rows, :]
        if final_norm:
            x2 = _rms_unit(x2) * gfin_ref[...]
        y_ref[rows, :] = x2

    for j in range(n_grp):
        pieces[f"up{j}"] = ("mxu", matmul_bundles(D_MODEL, 2 * cb), (), functools.partial(up_group, j))
        for r in range(n_rb):
            pieces[f"conv{j}.{r}"] = ("valu", CONV_BUNDLES, (f"up{j}",), functools.partial(conv_rows, j, r))
    all_conv = tuple(f"conv{j}.{r}" for j in range(n_grp) for r in range(n_rb))
    split = DOWN_SPLIT_GROUPS
    pieces["down_a"] = ("mxu", matmul_bundles(split * cb, D_MODEL), all_conv[:split * n_rb],
                        functools.partial(down_proj, 0, split))
    pieces["down"] = ("mxu", matmul_bundles((n_grp - split) * cb, D_MODEL), all_conv + ("down_a",),
                      functools.partial(down_proj, split, n_grp))
    for r in range(n_rb):
        pieces[f"y{r}"] = ("valu", RESID_BUNDLES, ("down",), functools.partial(resid_out, r))

    def norm_rows(r):
        rows = row_blocks[r]
        h = _rms_unit(x_ref[rows, :]) * gmix_ref[...]
        h_ref[rows, :] = (h * (1.0 + sc1) + sh1).astype(BF16)

    def in_cols(c0, c1):
        p_ref[:, c0:c1] = _dot(h_ref[...], win_ref[:, c0:c1])

    all_norm = tuple(f"norm{r}" for r in range(n_rb))
    for r in range(n_rb):
        pieces[f"norm{r}"] = ("valu", NORM_BUNDLES, (), functools.partial(norm_rows, r))
    pieces["in_ret"] = ("mxu", matmul_bundles(D_MODEL, 4 * RET_W), all_norm, functools.partial(in_cols, 0, 4 * RET_W))
    pieces["in_gm"] = ("mxu", matmul_bundles(D_MODEL, 2 * GM_W), all_norm,
                       functools.partial(in_cols, 4 * RET_W, IN_W))

    def mix_heads(c):
        rows = slice(c * CHUNK, (c + 1) * CHUNK)
        heads = [slice(hh * HEAD_W, (hh + 1) * HEAD_W) for hh in range(N_HEADS)]
        cos_full, sin_signed = cos_ref[rows, :], sin_ref[rows, :]
        q = [_rope(p_ref[rows, hh * HEAD_W:(hh + 1) * HEAD_W], cos_full, sin_signed) for hh in range(N_HEADS)]
        k = [_rope(p_ref[rows, RET_W + hh * HEAD_W:RET_W + (hh + 1) * HEAD_W], cos_full, sin_signed)
             * (HEAD_W ** -0.5) for hh in range(N_HEADS)]
        vb = [p_ref[rows, 2 * RET_W + hh * HEAD_W:2 * RET_W + (hh + 1) * HEAD_W].astype(BF16)
              for hh in range(N_HEADS)]
        scores = [lax.dot_general(q[hh].astype(BF16), k[hh].astype(BF16), NT_DIMS, preferred_element_type=F32)
                  for hh in range(N_HEADS)]
        update = [lax.dot_general((k[hh] * kdec_ref[:, heads[hh]]).astype(BF16), vb[hh], TN_DIMS,
                                  preferred_element_type=F32) for hh in range(N_HEADS)]
        state = [ret_ref[hh] for hh in range(N_HEADS)]
        o = [_dot(jnp.concatenate([(scores[hh] * dmask_ref[hh]).astype(BF16),
                                   (q[hh] * qdec_ref[:, heads[hh]]).astype(BF16)], axis=1),
                  jnp.concatenate([vb[hh], state[hh].astype(BF16)], axis=0)) for hh in range(N_HEADS)]
        for hh in range(N_HEADS):
            ret_ref[hh] = jnp.where(front_on, cdec_ref[hh] * state[hh] + update[hh], state[hh])
        for hh in range(N_HEADS):
            on = _center_unit(o[hh]) * gn_ref[:, heads[hh]]
            g = p_ref[rows, 3 * RET_W + hh * HEAD_W:3 * RET_W + (hh + 1) * HEAD_W]
            mix_ref[rows, heads[hh]] = (_silu(g) * on).astype(BF16)

    def mix_groups(c):
        rows = slice(c * CHUNK, (c + 1) * CHUNK)
        groups = [slice(gg * HEAD_W, (gg + 1) * HEAD_W) for gg in range(N_HEADS)]
        vn = [(_center_unit(_gelu_tanh(p_ref[rows, 4 * RET_W + GM_W + gg * HEAD_W:4 * RET_W + GM_W + (gg + 1) * HEAD_W]))
               * ln_ref[:, groups[gg]]).astype(BF16) for gg in range(N_HEADS)]
        gate = [_dot(ws_ref[gg], vn[gg]) for gg in range(N_HEADS)]
        for gg in range(N_HEADS):
            u = _gelu_tanh(p_ref[rows, 4 * RET_W + gg * HEAD_W:4 * RET_W + (gg + 1) * HEAD_W])
            mix_ref[rows, RET_W + gg * HEAD_W:RET_W + (gg + 1) * HEAD_W] = (
                u * (gate[gg] + bs_ref[:, groups[gg]])).astype(BF16)

    def out_proj():
        mo_ref[...] = _dot(mix_ref[...], wout_ref[...])

    def resid_mid(r):
        rows = row_blocks[r]
        x1 = x_ref[rows, :] + gt1 * mo_ref[rows, :]
        x1_ref[rows, :] = x1
        h = _rms_unit(x1) * gffn_ref[...]
        h2_ref[rows, :] = (h * (1.0 + sc2) + sh2).astype(BF16)

    all_mix = []
    for c in range(n_chunks):
        pieces[f"heads{c}"] = ("valu", N_HEADS * HEAD_BUNDLES, ("in_ret",), functools.partial(mix_heads, c))
        pieces[f"groups{c}"] = ("valu", N_HEADS * GROUP_BUNDLES, ("in_gm",), functools.partial(mix_groups, c))
        all_mix += [f"heads{c}", f"groups{c}"]
    pieces["out"] = ("mxu", matmul_bundles(RET_W + GM_W, D_MODEL), tuple(all_mix), out_proj)
    all_up = tuple(f"up{j}" for j in range(n_grp))
    for r in range(n_rb):
        pieces[f"x1{r}"] = ("valu", RESID_BUNDLES + NORM_BUNDLES, ("out",) + all_up, functools.partial(resid_mid, r))

    orders = {
        "mxu": list(all_up) + ["in_ret", "in_gm", "down_a", "down", "out"],
        "valu": list(all_norm) + list(all_conv) + all_mix
        + [f"y{r}" for r in range(n_rb)] + [f"x1{r}" for r in range(n_rb)],
    }
    y_ref[...] = x1_ref[...]
    for name in _two_unit_order(pieces, orders):
        pieces[name][3]()


def _const_spec(shape):
    zeros = (0,) * len(shape)
    return pl.BlockSpec(shape, lambda s: zeros, pipeline_mode=pl.Buffered(1))


def _prompt_layer(layer, x, mod, lw, tabs, g_final, final_norm):
    batch, seq, _ = x.shape

    def layer_spec(name):
        shape = lw[name].shape
        zeros = (0,) * (len(shape) - 1)
        return pl.BlockSpec((None,) + tuple(shape[1:]), lambda s: (layer,) + zeros, pipeline_mode=pl.Buffered(1))

    tm = PROMPT_TILE
    tiles_per_seq = seq // tm
    n_tiles = batch * tiles_per_seq
    body = functools.partial(_prompt_layer_body, tm=tm, tiles_per_seq=tiles_per_seq, n_tiles=n_tiles,
                             final_norm=final_norm)

    def front(s):
        i = jnp.minimum(s, n_tiles - 1)
        return i // tiles_per_seq, lax.rem(i, tiles_per_seq)

    def back(s):
        i = jnp.maximum(s - 1, 0)
        return i // tiles_per_seq, lax.rem(i, tiles_per_seq)

    in_specs = [
        pl.BlockSpec((None, tm, D_MODEL), lambda s: (*front(s), 0)),
        pl.BlockSpec((None, None, N_MOD, D_MODEL), lambda s: (layer, front(s)[0], 0, 0)),
        pl.BlockSpec((None, None, N_MOD, D_MODEL), lambda s: (layer, back(s)[0], 0, 0)),
        layer_spec("g_mix"), layer_spec("g_ffn"),
        layer_spec("w_in"), layer_spec("w_out"), layer_spec("w_up"), layer_spec("w_down"),
        pl.BlockSpec((tm, HEAD_W), lambda s: (front(s)[1], 0)),
        pl.BlockSpec((tm, HEAD_W), lambda s: (front(s)[1], 0)),
        _const_spec((N_HEADS, CHUNK, CHUNK)), _const_spec((CHUNK, RET_W)), _const_spec((CHUNK, RET_W)),
        _const_spec((N_HEADS, 1, HEAD_W)), layer_spec("gn"), layer_spec("ln"),
        layer_spec("ws_tril"), layer_spec("bs_full"),
        layer_spec("conv_w"), layer_spec("conv_b"), _const_spec((1, D_MODEL)),
    ]
    out_specs = [
        pl.BlockSpec((None, tm, D_MODEL), lambda s: (*back(s), 0)),
        pl.BlockSpec((None, N_HEADS, HEAD_W, HEAD_W), lambda s: (front(s)[0], 0, 0, 0)),
        pl.BlockSpec((None, CONV_W - 1, 2 * D_FF), lambda s: (back(s)[0], 0, 0)),
    ]
    out_shape = [
        jax.ShapeDtypeStruct((batch, seq, D_MODEL), F32),
        jax.ShapeDtypeStruct((batch, N_HEADS, HEAD_W, HEAD_W), F32),
        jax.ShapeDtypeStruct((batch, CONV_W - 1, 2 * D_FF), F32),
    ]
    scratch = [
        pltpu.VMEM((tm, D_MODEL), BF16),
        pltpu.VMEM((tm, IN_W), F32),
        pltpu.VMEM((tm, RET_W + GM_W), BF16),
        pltpu.VMEM((tm, D_MODEL), F32),
        pltpu.VMEM((tm, D_MODEL), F32),
        pltpu.VMEM((tm, D_MODEL), BF16),
        pltpu.VMEM((tm, D_FF), BF16),
        pltpu.VMEM((tm, D_MODEL), F32),
    ] + [pltpu.VMEM((CONV_PAD + tm, 2 * 512), F32)] * (D_FF // 512)
    return pl.pallas_call(
        body,
        grid=(n_tiles + 1,),
        in_specs=in_specs,
        out_specs=out_specs,
        out_shape=out_shape,
        scratch_shapes=scratch,
        compiler_params=pltpu.CompilerParams(
            dimension_semantics=("arbitrary",), vmem_limit_bytes=V7X_VMEM_LIMIT_BYTES),
        name="prompt_layer",
    )(x, mod, mod, lw["g_mix"], lw["g_ffn"], lw["w_in"], lw["w_out"], lw["w_up"], lw["w_down"],
      tabs["cos"], tabs["sin"], tabs["dmask"], tabs["qdec"], tabs["kdec"], tabs["cdec"],
      lw["gn"], lw["ln"], lw["ws_tril"], lw["bs_full"], lw["conv_w"], lw["conv_b"], g_final)


def _sample_in_body(x_ref, mod_ref, gmix_ref, win_ref, cos_ref, sin_ref, ln_ref, ws0_ref, bs0_ref,
                    q_ref, k_ref, v_ref, sg_ref, ogm_ref, vn_ref):
    sh1, sc1 = mod_ref[:, 0:D_MODEL], mod_ref[:, D_MODEL:2 * D_MODEL]
    h = _rms_unit(x_ref[...]) * gmix_ref[...]
    h = (h * (1.0 + sc1) + sh1).astype(BF16)
    p = _dot(h, win_ref[...])
    for hh in range(N_HEADS):
        cs = slice(hh * HEAD_W, (hh + 1) * HEAD_W)
        q_ref[:, cs] = _rope(p[:, hh * HEAD_W:(hh + 1) * HEAD_W], cos_ref[...], sin_ref[...])
        k_ref[:, cs] = _rope(p[:, RET_W + hh * HEAD_W:RET_W + (hh + 1) * HEAD_W],
                             cos_ref[...], sin_ref[...]) * (HEAD_W ** -0.5)
        u = _gelu_tanh(p[:, 4 * RET_W + hh * HEAD_W:4 * RET_W + (hh + 1) * HEAD_W])
        vv = _gelu_tanh(p[:, 4 * RET_W + GM_W + hh * HEAD_W:4 * RET_W + GM_W + (hh + 1) * HEAD_W])
        vn = _center_unit(vv) * ln_ref[:, cs]
        vn_ref[:, cs] = vn
        ogm_ref[:, cs] = u * (ws0_ref[:, cs] * vn + bs0_ref[:, cs])
    v_ref[...] = p[:, 2 * RET_W:3 * RET_W]
    sg_ref[...] = _silu(p[:, 3 * RET_W:4 * RET_W])


def _sample_ret_body(q_ref, k_ref, v_ref, s_ref, gam_ref, *rest, sb, layer, first):
    o_ref, snew_ref = rest[-2:]
    if first:
        for other in range(snew_ref.shape[0]):
            if other != layer:
                snew_ref[other] = jnp.zeros(snew_ref.shape[1:], F32)
        snew_ref = snew_ref.at[layer]
    for hh in range(N_HEADS):
        cs = slice(hh * HEAD_W, (hh + 1) * HEAD_W)
        gam = gam_ref[:, cs]
        v = v_ref[:, cs]
        k = k_ref[:, cs]
        qb = q_ref[:, cs].astype(BF16)
        kt = jnp.concatenate([k, jnp.zeros((HEAD_W - sb, HEAD_W), F32)], axis=0).T
        for s in range(sb):
            state = s_ref[s, hh]
            snew_ref[s, hh] = gam * state + kt[:, s:s + 1] * v[s:s + 1, :]
            o_ref[s:s + 1, cs] = gam * _dot(qb, state.astype(BF16))[s:s + 1, :]
        qk = jnp.sum(q_ref[:, cs] * k, axis=-1, keepdims=True)
        o_ref[:, cs] = o_ref[:, cs] + qk * v


def _sample_out_body(x_ref, mod_ref, o_ref, sg_ref, ogm_ref, gn_ref, wout_ref, gffn_ref, wup_ref,
                     cbuf_ref, convw_ref, convb_ref, wdown_ref, gfin_ref, acc_ref, y_ref, cnew_ref,
                     *, final_norm):
    del acc_ref
    gt1 = mod_ref[:, 2 * D_MODEL:3 * D_MODEL]
    sh2, sc2 = mod_ref[:, 3 * D_MODEL:4 * D_MODEL], mod_ref[:, 4 * D_MODEL:5 * D_MODEL]
    gt2 = mod_ref[:, 5 * D_MODEL:6 * D_MODEL]
    parts = []
    for hh in range(N_HEADS):
        cs = slice(hh * HEAD_W, (hh + 1) * HEAD_W)
        on = _center_unit(o_ref[:, cs]) * gn_ref[:, cs]
        parts.append((sg_ref[:, cs] * on).astype(BF16))
    parts.append(ogm_ref[...].astype(BF16))
    mix = _dot(jnp.concatenate(parts, axis=1), wout_ref[...])
    x1 = x_ref[...] + gt1 * mix
    h = _rms_unit(x1) * gffn_ref[...]
    h = (h * (1.0 + sc2) + sh2).astype(BF16)
    a = _dot(h, wup_ref[...])
    gw = GATE_GROUP_W
    f_parts = []
    for j in range(D_FF // gw):
        halves = []
        for half in range(2):
            c0 = half * D_FF + j * gw
            a_blk = a[:, (2 * j + half) * gw:(2 * j + half + 1) * gw]
            buf0, buf1 = cbuf_ref[:, 0, c0:c0 + gw], cbuf_ref[:, 1, c0:c0 + gw]
            halves.append(convb_ref[:, c0:c0 + gw] + buf0 * convw_ref[0:1, c0:c0 + gw]
                          + buf1 * convw_ref[1:2, c0:c0 + gw] + a_blk * convw_ref[2:3, c0:c0 + gw])
            cnew_ref[:, 0, c0:c0 + gw] = buf1
            cnew_ref[:, 1, c0:c0 + gw] = a_blk
        f_parts.append((_silu(halves[0]) * halves[1]).astype(BF16))
    f = jnp.concatenate(f_parts, axis=1)
    x2 = x1 + gt2 * _dot(f, wdown_ref[...])
    if final_norm:
        x2 = _rms_unit(x2) * gfin_ref[...]
    y_ref[...] = x2


def _whole(shape, pipeline_mode=None):
    zeros = (0,) * len(shape)
    return pl.BlockSpec(shape, lambda *_: zeros, pipeline_mode=pipeline_mode)


def _layer_slab(shape, layer, pipeline_mode=None):
    zeros = (0,) * (len(shape) - 1)
    return pl.BlockSpec((None,) + tuple(shape[1:]), lambda *_: (layer,) + zeros, pipeline_mode=pipeline_mode)


def _sample_layer(layer, x, mod, state_ret, state_conv, ret_acc, conv_acc, lw, tabs, g_final, final_norm):
    n = x.shape[0]
    params = pltpu.CompilerParams(dimension_semantics=("arbitrary",), vmem_limit_bytes=V7X_VMEM_LIMIT_BYTES)
    act = jax.ShapeDtypeStruct((n, RET_W), F32)
    in_args = (x, mod, lw["g_mix"], lw["w_in"], tabs["cos_s"], tabs["sin_s"], lw["ln"], lw["ws0"], lw["bs0"])
    in_specs = [_whole(a.shape) for a in in_args]
    in_specs[1] = pl.BlockSpec((None, n, N_MOD * D_MODEL), lambda i: (layer, 0, 0))
    for i in (2, 3, 6, 7, 8):
        in_specs[i] = _layer_slab(in_args[i].shape, layer)
    q, k, v, sg, ogm, vn = pl.pallas_call(
        _sample_in_body,
        grid=(1,),
        in_specs=in_specs,
        out_specs=[_whole(act.shape)] * 6,
        out_shape=[act] * 6,
        compiler_params=params,
        name="sample_in",
    )(*in_args)

    sb = SAMPLE_SEQ_BLOCK
    row_spec = pl.BlockSpec((sb, RET_W), lambda i: (i, 0))
    state_spec = pl.BlockSpec((None, sb, N_HEADS, HEAD_W, HEAD_W), lambda i: (layer, i, 0, 0, 0))
    carried = [] if ret_acc is None else [ret_acc]
    depth = state_ret.shape[0]
    first = ret_acc is None
    all_layers_spec = pl.BlockSpec((depth, sb, N_HEADS, HEAD_W, HEAD_W), lambda i: (0, i, 0, 0, 0))
    o, ret_acc = pl.pallas_call(
        functools.partial(_sample_ret_body, sb=sb, layer=layer, first=first),
        grid=(n // sb,),
        in_specs=[row_spec, row_spec, row_spec, state_spec, _whole((1, RET_W))]
        + [pl.BlockSpec(memory_space=pl.ANY)] * len(carried),
        out_specs=[row_spec, all_layers_spec if first else state_spec],
        out_shape=[act, jax.ShapeDtypeStruct(state_ret.shape, F32)],
        input_output_aliases={5: 1} if carried else {},
        compiler_params=params,
        name="sample_ret",
    )(q, k, v, state_ret, tabs["gamma_s"], *carried)

    if conv_acc is None:
        conv_acc = jnp.zeros(state_conv.shape, F32)
    out_args = (x, mod, o, sg, ogm, lw["gn"], lw["w_out"], lw["g_ffn"], lw["w_up"], state_conv,
                lw["conv_w"], lw["conv_b"], lw["w_down"], g_final)
    once = pl.Buffered(1)
    in_specs = [_whole(a.shape, once) for a in out_args]
    in_specs[1] = pl.BlockSpec((None, n, N_MOD * D_MODEL), lambda i: (layer, 0, 0), pipeline_mode=once)
    for i in (5, 6, 7, 8, 9, 10, 11, 12):
        in_specs[i] = _layer_slab(out_args[i].shape, layer, once)

    y, conv_acc = pl.pallas_call(
        functools.partial(_sample_out_body, final_norm=final_norm),
        grid=(1,),
        in_specs=in_specs + [pl.BlockSpec(memory_space=pl.ANY)],
        out_specs=[_whole((n, D_MODEL)), _layer_slab(state_conv.shape, layer, once)],
        out_shape=[jax.ShapeDtypeStruct((n, D_MODEL), F32), jax.ShapeDtypeStruct(state_conv.shape, F32)],
        input_output_aliases={len(out_args): 1},
        compiler_params=params,
        name="sample_out",
    )(*out_args, conv_acc)
    return y, ret_acc, conv_acc, vn


def _rope_tables(pos):
    half = HEAD_W // 2
    freqs = np.exp(-math.log(ROPE_BASE) * np.arange(half, dtype=np.float64) / half)
    ang = np.asarray(pos, dtype=np.float64)[:, None] * freqs[None, :]
    cos, sin = np.cos(ang), np.sin(ang)
    return (np.concatenate([cos, cos], axis=-1).astype(np.float32),
            np.concatenate([-sin, sin], axis=-1).astype(np.float32))


def _decay_tables(chunk):
    lg = np.log1p(-np.exp2(-5.0 - np.arange(N_HEADS, dtype=np.float64)))
    i = np.arange(chunk, dtype=np.float64)
    diff = i[:, None] - i[None, :]
    dmask = np.where(diff[None] >= 0.0, np.exp(np.maximum(diff, 0.0)[None] * lg[:, None, None]), 0.0)
    q_dec = np.exp((i[:, None] + 1.0) * lg[None, :])
    k_dec = np.exp((chunk - 1.0 - i)[:, None] * lg[None, :])
    chunk_dec = np.exp(chunk * lg)
    return tuple(a.astype(np.float32) for a in (dmask, q_dec, k_dec, chunk_dec))


def _per_head_lanes(a):
    return a.repeat(HEAD_W, axis=-1)


def kernel(x_prompt, x_sample, state_ret, state_conv, c_prompt, c_sample, w_ada, b_ada, g_mix, w_in,
           ret_gn_gain, gmlp_ln_gain, w_s, b_s, w_out, g_ffn, w_up, conv_w, conv_b, w_down, g_final):
    depth = w_in.shape[0]
    batch, seq, _ = x_prompt.shape
    n_dec, dec_seq, _ = x_sample.shape
    assert dec_seq == 1 and seq % PROMPT_TILE == 0 and PROMPT_TILE % CHUNK == 0

    cos_p, sin_p = _rope_tables(np.arange(seq))
    cos_s, sin_s = _rope_tables(PAST_LEN + np.arange(dec_seq))
    dmask, q_dec, k_dec, chunk_dec = _decay_tables(CHUNK)
    _, _, _, gamma_s = _decay_tables(dec_seq)
    tabs = {
        "cos": cos_p, "sin": sin_p, "cos_s": cos_s, "sin_s": sin_s,
        "dmask": dmask, "qdec": _per_head_lanes(q_dec), "kdec": _per_head_lanes(k_dec),
        "cdec": np.broadcast_to(chunk_dec[:, None, None], (N_HEADS, 1, HEAD_W)),
        "gamma_s": _per_head_lanes(gamma_s[None, :]),
    }
    tabs = {name: jnp.asarray(a) for name, a in tabs.items()}

    mod = _adaln(jnp.concatenate([c_sample, c_prompt], axis=0), w_ada, b_ada)
    mod_p = mod[:, n_dec:].reshape(depth, batch, N_MOD, D_MODEL)
    tril = np.tril(np.ones((CHUNK, CHUNK), dtype=bool))
    lw = {
        "g_mix": g_mix.reshape(depth, 1, D_MODEL), "g_ffn": g_ffn.reshape(depth, 1, D_MODEL),
        "w_in": _to_bf16(w_in), "w_out": _to_bf16(w_out),
        "w_up": _to_bf16(w_up, group_gate_columns=True), "w_down": _to_bf16(w_down),
        "gn": ret_gn_gain.reshape(depth, 1, RET_W), "ln": gmlp_ln_gain.reshape(depth, 1, GM_W),
        "ws_tril": jnp.where(tril, w_s, 0.0).astype(BF16),
        "bs_full": _per_head_lanes(b_s.transpose(0, 2, 1)),
        "ws0": _per_head_lanes(w_s[:, None, :, 0, 0]), "bs0": _per_head_lanes(b_s[:, None, :, 0]),
        "conv_w": conv_w, "conv_b": conv_b.reshape(depth, 1, 2 * D_FF),
    }
    g_fin = g_final.reshape(1, D_MODEL)

    xp, xs = x_prompt, x_sample.reshape(n_dec, D_MODEL)
    ret_p, conv_p, v_s = [], [], []
    ret_s = conv_s = None
    for l in range(depth):
        last = l == depth - 1
        xp, sp, bp = _prompt_layer(l, xp, mod_p, lw, tabs, g_fin, last)
        xs, ret_s, conv_s, vs = _sample_layer(l, xs, mod, state_ret, state_conv, ret_s, conv_s, lw, tabs,
                                              g_fin, last)
        ret_p.append(sp); conv_p.append(bp)
        v_s.append(vs.reshape(n_dec, dec_seq, GM_W))
    return (xp, xs.reshape(n_dec, dec_seq, D_MODEL), jnp.stack(ret_p), jnp.stack(conv_p),
            ret_s, conv_s, jnp.stack(v_s))
```

```python
import functools
import math

import jax
import jax.numpy as jnp
import numpy as np
from jax import lax
from jax.experimental import pallas as pl
from jax.experimental.pallas import tpu as pltpu

D_MODEL = 1024
N_HEADS = 4
HEAD_W = 128
RET_W = N_HEADS * HEAD_W
GM_W = N_HEADS * HEAD_W
IN_W = 4 * RET_W + 2 * GM_W
D_FF = 2048
CONV_W = 3
CHUNK = 128
ROPE_BASE = 10000.0
PAST_LEN = 16384
EPS = 1e-6
N_MOD = 6

V7X_VMEM_LIMIT_BYTES = 60 * 1024 * 1024
PROMPT_TILE = 512
ROW_BLOCK = 32
CONV_PAD = 8
GATE_GROUP_W = 512
DOWN_SPLIT_GROUPS = 2
SAMPLE_SEQ_BLOCK = 16

F32 = jnp.float32
BF16 = jnp.bfloat16
NT_DIMS = (((1,), (1,)), ((), ()))
TN_DIMS = (((0,), (0,)), ((), ()))


def _silu(x):
    return x * (1.0 / (1.0 + jnp.exp(-x)))


def _gelu_tanh(x):
    c = math.sqrt(2.0 / math.pi)
    return 0.5 * x * (1.0 + jnp.tanh(c * (x + 0.044715 * (x * x * x))))


def _rms_unit(x):
    return x * lax.rsqrt(jnp.mean(x * x, axis=-1, keepdims=True) + EPS)


def _center_unit(x):
    xc = x - jnp.mean(x, axis=-1, keepdims=True)
    return xc * lax.rsqrt(jnp.mean(xc * xc, axis=-1, keepdims=True) + EPS)


def _rope(x, cos_full, sin_signed):
    return x * cos_full + pltpu.roll(x, HEAD_W // 2, 1) * sin_signed


def _dot(a, b):
    return jnp.dot(a, b, preferred_element_type=F32)


NORM_BUNDLES = 60
RESID_BUNDLES = 30
CONV_BUNDLES = 170
HEAD_BUNDLES = 160
GROUP_BUNDLES = 130


def _two_unit_order(pieces, orders):
    unit_free = {unit: 0 for unit in orders}
    head = {unit: 0 for unit in orders}
    finish = {}
    order = []
    while any(head[unit] < len(names) for unit, names in orders.items()):
        ready = []
        for unit, names in orders.items():
            if head[unit] < len(names):
                name = names[head[unit]]
                deps = pieces[name][2]
                if all(d in finish for d in deps):
                    ready.append((max([unit_free[unit]] + [finish[d] for d in deps]), unit, name))
        assert ready, "piece lists contradict the dependencies"
        start, unit, name = min(ready)
        finish[name] = unit_free[unit] = start + pieces[name][1]
        head[unit] += 1
        order.append(name)
    return order


def _cast_body(*refs):
    o_ref = refs[-1]
    width = o_ref.shape[-1] // (len(refs) - 1)
    for i, w_ref in enumerate(refs[:-1]):
        o_ref[:, i * width:(i + 1) * width] = w_ref[...].astype(BF16)


def _to_bf16(w, group_gate_columns=False):
    depth, k, n = w.shape
    bn = 2 * GATE_GROUP_W
    if group_gate_columns:
        n_grp = n // bn
        srcs = [pl.BlockSpec((None, k, bn // 2), lambda l, c, half=half: (l, 0, half * n_grp + c)) for half in range(2)]
    else:
        srcs = [pl.BlockSpec((None, k, bn), lambda l, c: (l, 0, c))]
    return pl.pallas_call(
        _cast_body,
        grid=(depth, n // bn),
        in_specs=srcs,
        out_specs=pl.BlockSpec((None, k, bn), lambda l, c: (l, 0, c)),
        out_shape=jax.ShapeDtypeStruct(w.shape, BF16),
        compiler_params=pltpu.CompilerParams(
            dimension_semantics=("arbitrary", "arbitrary"), vmem_limit_bytes=V7X_VMEM_LIMIT_BYTES),
        name="to_bf16",
    )(*([w] * len(srcs)))


def _adaln_body(c_ref, w_ref, b_ref, o_ref):
    c = _silu(c_ref[...]).astype(BF16)
    o_ref[...] = _dot(c, w_ref[...].astype(BF16)) + b_ref[...]


def _adaln(c_all, w_ada, b_ada):
    depth = w_ada.shape[0]
    rows = c_all.shape[0]
    n_out = w_ada.shape[2]
    bn = 3 * D_MODEL
    return pl.pallas_call(
        _adaln_body,
        grid=(depth, n_out // bn),
        in_specs=[
            pl.BlockSpec((rows, D_MODEL), lambda l, j: (0, 0)),
            pl.BlockSpec((None, D_MODEL, bn), lambda l, j: (l, 0, j)),
            pl.BlockSpec((None, 1, bn), lambda l, j: (l, 0, j)),
        ],
        out_specs=pl.BlockSpec((None, rows, bn), lambda l, j: (l, 0, j)),
        out_shape=jax.ShapeDtypeStruct((depth, rows, n_out), F32),
        compiler_params=pltpu.CompilerParams(
            dimension_semantics=("arbitrary", "arbitrary"), vmem_limit_bytes=V7X_VMEM_LIMIT_BYTES),
        name="adaln",
    )(c_all, w_ada, b_ada.reshape(depth, 1, n_out))


def _prompt_layer_body(x_ref, modf_ref, modb_ref, gmix_ref, gffn_ref, win_ref, wout_ref, wup_ref, wdown_ref,
                       cos_ref, sin_ref, dmask_ref, qdec_ref, kdec_ref, cdec_ref, gn_ref, ln_ref,
                       ws_ref, bs_ref, convw_ref, convb_ref, gfin_ref,
                       y_ref, ret_ref, conv_ref,
                       h_ref, p_ref, mix_ref, mo_ref, x1_ref, h2_ref, f_ref, mo2_ref, *a_refs,
                       tm, tiles_per_seq, n_tiles, final_norm):
    s = pl.program_id(0)
    front_on = s < n_tiles
    t_front = lax.rem(jnp.minimum(s, n_tiles - 1), tiles_per_seq)
    t_back = lax.rem(jnp.maximum(s - 1, 0), tiles_per_seq)

    @pl.when(s == 0)
    def _():
        x1_ref[...] = jnp.zeros_like(x1_ref)
        h2_ref[...] = jnp.zeros_like(h2_ref)

    @pl.when(jnp.logical_and(front_on, t_front == 0))
    def _():
        ret_ref[...] = jnp.zeros_like(ret_ref)

    @pl.when(t_back == 0)
    def _():
        conv_ref[...] = jnp.zeros_like(conv_ref)

    sh1, sc1, gt1 = modf_ref[0:1, :], modf_ref[1:2, :], modf_ref[2:3, :]
    sh2, sc2 = modf_ref[3:4, :], modf_ref[4:5, :]
    gt2 = modb_ref[5:6, :]
    row_blocks = [slice(r * ROW_BLOCK, (r + 1) * ROW_BLOCK) for r in range(tm // ROW_BLOCK)]

    cb = GATE_GROUP_W
    tail = slice(CONV_PAD - (CONV_W - 1), CONV_PAD)

    pieces = {}
    n_rb = len(row_blocks)
    n_grp = D_FF // cb
    n_chunks = tm // CHUNK

    def matmul_bundles(k, n):
        return (tm // 16) * (k // 256) * (n // 256) * 8 // 2

    def up_group(j):
        for half in range(2):
            c0 = half * D_FF + j * cb
            a_refs[j][tail, half * cb:(half + 1) * cb] = conv_ref[:, c0:c0 + cb]
        a_refs[j][CONV_PAD:CONV_PAD + tm, :] = _dot(h2_ref[...], wup_ref[:, 2 * j * cb:2 * (j + 1) * cb])
        for half in range(2):
            c0 = half * D_FF + j * cb
            conv_ref[:, c0:c0 + cb] = a_refs[j][CONV_PAD + tm - (CONV_W - 1):CONV_PAD + tm,
                                                half * cb:(half + 1) * cb]

    def conv_rows(j, r):
        halves = []
        for half in range(2):
            c0 = half * D_FF + j * cb
            src = slice(half * cb, (half + 1) * cb)
            window = a_refs[j][r * ROW_BLOCK:CONV_PAD + (r + 1) * ROW_BLOCK, src]
            acc = convb_ref[:, c0:c0 + cb]
            for tap in range(CONV_W):
                back = CONV_W - 1 - tap
                rows_back = pltpu.roll(window, back, 0) if back else window
                acc = acc + rows_back[CONV_PAD:, :] * convw_ref[tap:tap + 1, c0:c0 + cb]
            halves.append(acc)
        f_ref[row_blocks[r], j * cb:(j + 1) * cb] = (_silu(halves[0]) * halves[1]).astype(BF16)

    def down_proj(g0, g1):
        part = _dot(f_ref[:, g0 * cb:g1 * cb], wdown_ref[g0 * cb:g1 * cb, :])
        mo2_ref[...] = part if g0 == 0 else mo2_ref[...] + part

    def resid_out(r):
        rows = row_blocks[r]
        x2 = y_ref[rows, :] + gt2 * mo2_ref[rows, :]
        if final_norm:
            x2 = _rms_unit(x2) * gfin_ref[...]
        y_ref[rows, :] = x2

    for j in range(n_grp):
        pieces[f"up{j}"] = ("mxu", matmul_bundles(D_MODEL, 2 * cb), (), functools.partial(up_group, j))
        for r in range(n_rb):
            pieces[f"conv{j}.{r}"] = ("valu", CONV_BUNDLES, (f"up{j}",), functools.partial(conv_rows, j, r))
    all_conv = tuple(f"conv{j}.{r}" for j in range(n_grp) for r in range(n_rb))
    split = DOWN_SPLIT_GROUPS
    pieces["down_a"] = ("mxu", matmul_bundles(split * cb, D_MODEL), all_conv[:split * n_rb],
                        functools.partial(down_proj, 0, split))
    pieces["down"] = ("mxu", matmul_bundles((n_grp - split) * cb, D_MODEL), all_conv + ("down_a",),
                      functools.partial(down_proj, split, n_grp))
    for r in range(n_rb):
        pieces[f"y{r}"] = ("valu", RESID_BUNDLES, ("down",), functools.partial(resid_out, r))

    def norm_rows(r):
        rows = row_blocks[r]
        h = _rms_unit(x_ref[rows, :]) * gmix_ref[...]
        h_ref[rows, :] = (h * (1.0 + sc1) + sh1).astype(BF16)

    def in_cols(c0, c1):
        p_ref[:, c0:c1] = _dot(h_ref[...], win_ref[:, c0:c1])

    all_norm = tuple(f"norm{r}" for r in range(n_rb))
    for r in range(n_rb):
        pieces[f"norm{r}"] = ("valu", NORM_BUNDLES, (), functools.partial(norm_rows, r))
    pieces["in_ret"] = ("mxu", matmul_bundles(D_MODEL, 4 * RET_W), all_norm, functools.partial(in_cols, 0, 4 * RET_W))
    pieces["in_gm"] = ("mxu", matmul_bundles(D_MODEL, 2 * GM_W), all_norm,
                       functools.partial(in_cols, 4 * RET_W, IN_W))

    def mix_heads(c):
        rows = slice(c * CHUNK, (c + 1) * CHUNK)
        heads = [slice(hh * HEAD_W, (hh + 1) * HEAD_W) for hh in range(N_HEADS)]
        cos_full, sin_signed = cos_ref[rows, :], sin_ref[rows, :]
        q = [_rope(p_ref[rows, hh * HEAD_W:(hh + 1) * HEAD_W], cos_full, sin_signed) for hh in range(N_HEADS)]
        k = [_rope(p_ref[rows, RET_W + hh * HEAD_W:RET_W + (hh + 1) * HEAD_W], cos_full, sin_signed)
             * (HEAD_W ** -0.5) for hh in range(N_HEADS)]
        vb = [p_ref[rows, 2 * RET_W + hh * HEAD_W:2 * RET_W + (hh + 1) * HEAD_W].astype(BF16)
              for hh in range(N_HEADS)]
        scores = [lax.dot_general(q[hh].astype(BF16), k[hh].astype(BF16), NT_DIMS, preferred_element_type=F32)
                  for hh in range(N_HEADS)]
        update = [lax.dot_general((k[hh] * kdec_ref[:, heads[hh]]).astype(BF16), vb[hh], TN_DIMS,
                                  preferred_element_type=F32) for hh in range(N_HEADS)]
        state = [ret_ref[hh] for hh in range(N_HEADS)]
        o = [_dot(jnp.concatenate([(scores[hh] * dmask_ref[hh]).astype(BF16),
                                   (q[hh] * qdec_ref[:, heads[hh]]).astype(BF16)], axis=1),
                  jnp.concatenate([vb[hh], state[hh].astype(BF16)], axis=0)) for hh in range(N_HEADS)]
        for hh in range(N_HEADS):
            ret_ref[hh] = jnp.where(front_on, cdec_ref[hh] * state[hh] + update[hh], state[hh])
        for hh in range(N_HEADS):
            on = _center_unit(o[hh]) * gn_ref[:, heads[hh]]
            g = p_ref[rows, 3 * RET_W + hh * HEAD_W:3 * RET_W + (hh + 1) * HEAD_W]
            mix_ref[rows, heads[hh]] = (_silu(g) * on).astype(BF16)

    def mix_groups(c):
        rows = slice(c * CHUNK, (c + 1) * CHUNK)
        groups = [slice(gg * HEAD_W, (gg + 1) * HEAD_W) for gg in range(N_HEADS)]
        vn = [(_center_unit(_gelu_tanh(p_ref[rows, 4 * RET_W + GM_W + gg * HEAD_W:4 * RET_W + GM_W + (gg + 1) * HEAD_W]))
               * ln_ref[:, groups[gg]]).astype(BF16) for gg in range(N_HEADS)]
        gate = [_dot(ws_ref[gg], vn[gg]) for gg in range(N_HEADS)]
        for gg in range(N_HEADS):
            u = _gelu_tanh(p_ref[rows, 4 * RET_W + gg * HEAD_W:4 * RET_W + (gg + 1) * HEAD_W])
            mix_ref[rows, RET_W + gg * HEAD_W:RET_W + (gg + 1) * HEAD_W] = (
                u * (gate[gg] + bs_ref[:, groups[gg]])).astype(BF16)

    def out_proj():
        mo_ref[...] = _dot(mix_ref[...], wout_ref[...])

    def resid_mid(r):
        rows = row_blocks[r]
        x1 = x_ref[rows, :] + gt1 * mo_ref[rows, :]
        x1_ref[rows, :] = x1
        h = _rms_unit(x1) * gffn_ref[...]
        h2_ref[rows, :] = (h * (1.0 + sc2) + sh2).astype(BF16)

    all_mix = []
    for c in range(n_chunks):
        pieces[f"heads{c}"] = ("valu", N_HEADS * HEAD_BUNDLES, ("in_ret",), functools.partial(mix_heads, c))
        pieces[f"groups{c}"] = ("valu", N_HEADS * GROUP_BUNDLES, ("in_gm",), functools.partial(mix_groups, c))
        all_mix += [f"heads{c}", f"groups{c}"]
    pieces["out"] = ("mxu", matmul_bundles(RET_W + GM_W, D_MODEL), tuple(all_mix), out_proj)
    all_up = tuple(f"up{j}" for j in range(n_grp))
    for r in range(n_rb):
        pieces[f"x1{r}"] = ("valu", RESID_BUNDLES + NORM_BUNDLES, ("out",) + all_up, functools.partial(resid_mid, r))

    orders = {
        "mxu": list(all_up) + ["in_ret", "in_gm", "down_a", "down", "out"],
        "valu": list(all_norm) + list(all_conv) + all_mix
        + [f"y{r}" for r in range(n_rb)] + [f"x1{r}" for r in range(n_rb)],
    }
    y_ref[...] = x1_ref[...]
    for name in _two_unit_order(pieces, orders):
        pieces[name][3]()


def _const_spec(shape):
    zeros = (0,) * len(shape)
    return pl.BlockSpec(shape, lambda s: zeros, pipeline_mode=pl.Buffered(1))


def _prompt_layer(layer, x, mod, lw, tabs, g_final, final_norm):
    batch, seq, _ = x.shape

    def layer_spec(name):
        shape = lw[name].shape
        zeros = (0,) * (len(shape) - 1)
        return pl.BlockSpec((None,) + tuple(shape[1:]), lambda s: (layer,) + zeros, pipeline_mode=pl.Buffered(1))

    tm = PROMPT_TILE
    tiles_per_seq = seq // tm
    n_tiles = batch * tiles_per_seq
    body = functools.partial(_prompt_layer_body, tm=tm, tiles_per_seq=tiles_per_seq, n_tiles=n_tiles,
                             final_norm=final_norm)

    def front(s):
        i = jnp.minimum(s, n_tiles - 1)
        return i // tiles_per_seq, lax.rem(i, tiles_per_seq)

    def back(s):
        i = jnp.maximum(s - 1, 0)
        return i // tiles_per_seq, lax.rem(i, tiles_per_seq)

    in_specs = [
        pl.BlockSpec((None, tm, D_MODEL), lambda s: (*front(s), 0)),
        pl.BlockSpec((None, None, N_MOD, D_MODEL), lambda s: (layer, front(s)[0], 0, 0)),
        pl.BlockSpec((None, None, N_MOD, D_MODEL), lambda s: (layer, back(s)[0], 0, 0)),
        layer_spec("g_mix"), layer_spec("g_ffn"),
        layer_spec("w_in"), layer_spec("w_out"), layer_spec("w_up"), layer_spec("w_down"),
        pl.BlockSpec((tm, HEAD_W), lambda s: (front(s)[1], 0)),
        pl.BlockSpec((tm, HEAD_W), lambda s: (front(s)[1], 0)),
        _const_spec((N_HEADS, CHUNK, CHUNK)), _const_spec((CHUNK, RET_W)), _const_spec((CHUNK, RET_W)),
        _const_spec((N_HEADS, 1, HEAD_W)), layer_spec("gn"), layer_spec("ln"),
        layer_spec("ws_tril"), layer_spec("bs_full"),
        layer_spec("conv_w"), layer_spec("conv_b"), _const_spec((1, D_MODEL)),
    ]
    out_specs = [
        pl.BlockSpec((None, tm, D_MODEL), lambda s: (*back(s), 0)),
        pl.BlockSpec((None, N_HEADS, HEAD_W, HEAD_W), lambda s: (front(s)[0], 0, 0, 0)),
        pl.BlockSpec((None, CONV_W - 1, 2 * D_FF), lambda s: (back(s)[0], 0, 0)),
    ]
    out_shape = [
        jax.ShapeDtypeStruct((batch, seq, D_MODEL), F32),
        jax.ShapeDtypeStruct((batch, N_HEADS, HEAD_W, HEAD_W), F32),
        jax.ShapeDtypeStruct((batch, CONV_W - 1, 2 * D_FF), F32),
    ]
    scratch = [
        pltpu.VMEM((tm, D_MODEL), BF16),
        pltpu.VMEM((tm, IN_W), F32),
        pltpu.VMEM((tm, RET_W + GM_W), BF16),
        pltpu.VMEM((tm, D_MODEL), F32),
        pltpu.VMEM((tm, D_MODEL), F32),
        pltpu.VMEM((tm, D_MODEL), BF16),
        pltpu.VMEM((tm, D_FF), BF16),
        pltpu.VMEM((tm, D_MODEL), F32),
    ] + [pltpu.VMEM((CONV_PAD + tm, 2 * 512), F32)] * (D_FF // 512)
    return pl.pallas_call(
        body,
        grid=(n_tiles + 1,),
        in_specs=in_specs,
        out_specs=out_specs,
        out_shape=out_shape,
        scratch_shapes=scratch,
        compiler_params=pltpu.CompilerParams(
            dimension_semantics=("arbitrary",), vmem_limit_bytes=V7X_VMEM_LIMIT_BYTES),
        name="prompt_layer",
    )(x, mod, mod, lw["g_mix"], lw["g_ffn"], lw["w_in"], lw["w_out"], lw["w_up"], lw["w_down"],
      tabs["cos"], tabs["sin"], tabs["dmask"], tabs["qdec"], tabs["kdec"], tabs["cdec"],
      lw["gn"], lw["ln"], lw["ws_tril"], lw["bs_full"], lw["conv_w"], lw["conv_b"], g_final)


def _sample_in_body(x_ref, mod_ref, gmix_ref, win_ref, cos_ref, sin_ref, ln_ref, ws0_ref, bs0_ref,
                    q_ref, k_ref, v_ref, sg_ref, ogm_ref, vn_ref):
    sh1, sc1 = mod_ref[:, 0:D_MODEL], mod_ref[:, D_MODEL:2 * D_MODEL]
    h = _rms_unit(x_ref[...]) * gmix_ref[...]
    h = (h * (1.0 + sc1) + sh1).astype(BF16)
    p = _dot(h, win_ref[...])
    for hh in range(N_HEADS):
        cs = slice(hh * HEAD_W, (hh + 1) * HEAD_W)
        q_ref[:, cs] = _rope(p[:, hh * HEAD_W:(hh + 1) * HEAD_W], cos_ref[...], sin_ref[...])
        k_ref[:, cs] = _rope(p[:, RET_W + hh * HEAD_W:RET_W + (hh + 1) * HEAD_W],
                             cos_ref[...], sin_ref[...]) * (HEAD_W ** -0.5)
        u = _gelu_tanh(p[:, 4 * RET_W + hh * HEAD_W:4 * RET_W + (hh + 1) * HEAD_W])
        vv = _gelu_tanh(p[:, 4 * RET_W + GM_W + hh * HEAD_W:4 * RET_W + GM_W + (hh + 1) * HEAD_W])
        vn = _center_unit(vv) * ln_ref[:, cs]
        vn_ref[:, cs] = vn
        ogm_ref[:, cs] = u * (ws0_ref[:, cs] * vn + bs0_ref[:, cs])
    v_ref[...] = p[:, 2 * RET_W:3 * RET_W]
    sg_ref[...] = _silu(p[:, 3 * RET_W:4 * RET_W])


def _sample_ret_body(q_ref, k_ref, v_ref, s_ref, gam_ref, *rest, sb, layer, first):
    o_ref, snew_ref = rest[-2:]
    if first:
        for other in range(snew_ref.shape[0]):
            if other != layer:
                snew_ref[other] = jnp.zeros(snew_ref.shape[1:], F32)
        snew_ref = snew_ref.at[layer]
    for hh in range(N_HEADS):
        cs = slice(hh * HEAD_W, (hh + 1) * HEAD_W)
        gam = gam_ref[:, cs]
        v = v_ref[:, cs]
        k = k_ref[:, cs]
        qb = q_ref[:, cs].astype(BF16)
        kt = jnp.concatenate([k, jnp.zeros((HEAD_W - sb, HEAD_W), F32)], axis=0).T
        for s in range(sb):
            state = s_ref[s, hh]
            snew_ref[s, hh] = gam * state + kt[:, s:s + 1] * v[s:s + 1, :]
            o_ref[s:s + 1, cs] = gam * _dot(qb, state.astype(BF16))[s:s + 1, :]
        qk = jnp.sum(q_ref[:, cs] * k, axis=-1, keepdims=True)
        o_ref[:, cs] = o_ref[:, cs] + qk * v


def _sample_out_body(x_ref, mod_ref, o_ref, sg_ref, ogm_ref, gn_ref, wout_ref, gffn_ref, wup_ref,
                     cbuf_ref, convw_ref, convb_ref, wdown_ref, gfin_ref, acc_ref, y_ref, cnew_ref,
                     *, final_norm):
    del acc_ref
    gt1 = mod_ref[:, 2 * D_MODEL:3 * D_MODEL]
    sh2, sc2 = mod_ref[:, 3 * D_MODEL:4 * D_MODEL], mod_ref[:, 4 * D_MODEL:5 * D_MODEL]
    gt2 = mod_ref[:, 5 * D_MODEL:6 * D_MODEL]
    parts = []
    for hh in range(N_HEADS):
        cs = slice(hh * HEAD_W, (hh + 1) * HEAD_W)
        on = _center_unit(o_ref[:, cs]) * gn_ref[:, cs]
        parts.append((sg_ref[:, cs] * on).astype(BF16))
    parts.append(ogm_ref[...].astype(BF16))
    mix = _dot(jnp.concatenate(parts, axis=1), wout_ref[...])
    x1 = x_ref[...] + gt1 * mix
    h = _rms_unit(x1) * gffn_ref[...]
    h = (h * (1.0 + sc2) + sh2).astype(BF16)
    a = _dot(h, wup_ref[...])
    gw = GATE_GROUP_W
    f_parts = []
    for j in range(D_FF // gw):
        halves = []
        for half in range(2):
            c0 = half * D_FF + j * gw
            a_blk = a[:, (2 * j + half) * gw:(2 * j + half + 1) * gw]
            buf0, buf1 = cbuf_ref[:, 0, c0:c0 + gw], cbuf_ref[:, 1, c0:c0 + gw]
            halves.append(convb_ref[:, c0:c0 + gw] + buf0 * convw_ref[0:1, c0:c0 + gw]
                          + buf1 * convw_ref[1:2, c0:c0 + gw] + a_blk * convw_ref[2:3, c0:c0 + gw])
            cnew_ref[:, 0, c0:c0 + gw] = buf1
            cnew_ref[:, 1, c0:c0 + gw] = a_blk
        f_parts.append((_silu(halves[0]) * halves[1]).astype(BF16))
    f = jnp.concatenate(f_parts, axis=1)
    x2 = x1 + gt2 * _dot(f, wdown_ref[...])
    if final_norm:
        x2 = _rms_unit(x2) * gfin_ref[...]
    y_ref[...] = x2


def _whole(shape, pipeline_mode=None):
    zeros = (0,) * len(shape)
    return pl.BlockSpec(shape, lambda *_: zeros, pipeline_mode=pipeline_mode)


def _layer_slab(shape, layer, pipeline_mode=None):
    zeros = (0,) * (len(shape) - 1)
    return pl.BlockSpec((None,) + tuple(shape[1:]), lambda *_: (layer,) + zeros, pipeline_mode=pipeline_mode)


def _sample_layer(layer, x, mod, state_ret, state_conv, ret_acc, conv_acc, lw, tabs, g_final, final_norm):
    n = x.shape[0]
    params = pltpu.CompilerParams(dimension_semantics=("arbitrary",), vmem_limit_bytes=V7X_VMEM_LIMIT_BYTES)
    act = jax.ShapeDtypeStruct((n, RET_W), F32)
    in_args = (x, mod, lw["g_mix"], lw["w_in"], tabs["cos_s"], tabs["sin_s"], lw["ln"], lw["ws0"], lw["bs0"])
    in_specs = [_whole(a.shape) for a in in_args]
    in_specs[1] = pl.BlockSpec((None, n, N_MOD * D_MODEL), lambda i: (layer, 0, 0))
    for i in (2, 3, 6, 7, 8):
        in_specs[i] = _layer_slab(in_args[i].shape, layer)
    q, k, v, sg, ogm, vn = pl.pallas_call(
        _sample_in_body,
        grid=(1,),
        in_specs=in_specs,
        out_specs=[_whole(act.shape)] * 6,
        out_shape=[act] * 6,
        compiler_params=params,
        name="sample_in",
    )(*in_args)

    sb = SAMPLE_SEQ_BLOCK
    row_spec = pl.BlockSpec((sb, RET_W), lambda i: (i, 0))
    state_spec = pl.BlockSpec((None, sb, N_HEADS, HEAD_W, HEAD_W), lambda i: (layer, i, 0, 0, 0))
    carried = [] if ret_acc is None else [ret_acc]
    depth = state_ret.shape[0]
    first = ret_acc is None
    all_layers_spec = pl.BlockSpec((depth, sb, N_HEADS, HEAD_W, HEAD_W), lambda i: (0, i, 0, 0, 0))
    o, ret_acc = pl.pallas_call(
        functools.partial(_sample_ret_body, sb=sb, layer=layer, first=first),
        grid=(n // sb,),
        in_specs=[row_spec, row_spec, row_spec, state_spec, _whole((1, RET_W))]
        + [pl.BlockSpec(memory_space=pl.ANY)] * len(carried),
        out_specs=[row_spec, all_layers_spec if first else state_spec],
        out_shape=[act, jax.ShapeDtypeStruct(state_ret.shape, F32)],
        input_output_aliases={5: 1} if carried else {},
        compiler_params=params,
        name="sample_ret",
    )(q, k, v, state_ret, tabs["gamma_s"], *carried)

    if conv_acc is None:
        conv_acc = jnp.zeros(state_conv.shape, F32)
    out_args = (x, mod, o, sg, ogm, lw["gn"], lw["w_out"], lw["g_ffn"], lw["w_up"], state_conv,
                lw["conv_w"], lw["conv_b"], lw["w_down"], g_final)
    once = pl.Buffered(1)
    in_specs = [_whole(a.shape, once) for a in out_args]
    in_specs[1] = pl.BlockSpec((None, n, N_MOD * D_MODEL), lambda i: (layer, 0, 0), pipeline_mode=once)
    for i in (5, 6, 7, 8, 9, 10, 11, 12):
        in_specs[i] = _layer_slab(out_args[i].shape, layer, once)

    y, conv_acc = pl.pallas_call(
        functools.partial(_sample_out_body, final_norm=final_norm),
        grid=(1,),
        in_specs=in_specs + [pl.BlockSpec(memory_space=pl.ANY)],
        out_specs=[_whole((n, D_MODEL)), _layer_slab(state_conv.shape, layer, once)],
        out_shape=[jax.ShapeDtypeStruct((n, D_MODEL), F32), jax.ShapeDtypeStruct(state_conv.shape, F32)],
        input_output_aliases={len(out_args): 1},
        compiler_params=params,
        name="sample_out",
    )(*out_args, conv_acc)
    return y, ret_acc, conv_acc, vn


def _rope_tables(pos):
    half = HEAD_W // 2
    freqs = np.exp(-math.log(ROPE_BASE) * np.arange(half, dtype=np.float64) / half)
    ang = np.asarray(pos, dtype=np.float64)[:, None] * freqs[None, :]
    cos, sin = np.cos(ang), np.sin(ang)
    return (np.concatenate([cos, cos], axis=-1).astype(np.float32),
            np.concatenate([-sin, sin], axis=-1).astype(np.float32))


def _decay_tables(chunk):
    lg = np.log1p(-np.exp2(-5.0 - np.arange(N_HEADS, dtype=np.float64)))
    i = np.arange(chunk, dtype=np.float64)
    diff = i[:, None] - i[None, :]
    dmask = np.where(diff[None] >= 0.0, np.exp(np.maximum(diff, 0.0)[None] * lg[:, None, None]), 0.0)
    q_dec = np.exp((i[:, None] + 1.0) * lg[None, :])
    k_dec = np.exp((chunk - 1.0 - i)[:, None] * lg[None, :])
    chunk_dec = np.exp(chunk * lg)
    return tuple(a.astype(np.float32) for a in (dmask, q_dec, k_dec, chunk_dec))


def _per_head_lanes(a):
    return a.repeat(HEAD_W, axis=-1)


def kernel(x_prompt, x_sample, state_ret, state_conv, c_prompt, c_sample, w_ada, b_ada, g_mix, w_in,
           ret_gn_gain, gmlp_ln_gain, w_s, b_s, w_out, g_ffn, w_up, conv_w, conv_b, w_down, g_final):
    depth = w_in.shape[0]
    batch, seq, _ = x_prompt.shape
    n_dec, dec_seq, _ = x_sample.shape
    assert dec_seq == 1 and seq % PROMPT_TILE == 0 and PROMPT_TILE % CHUNK == 0

    cos_p, sin_p = _rope_tables(np.arange(seq))
    cos_s, sin_s = _rope_tables(PAST_LEN + np.arange(dec_seq))
    dmask, q_dec, k_dec, chunk_dec = _decay_tables(CHUNK)
    _, _, _, gamma_s = _decay_tables(dec_seq)
    tabs = {
        "cos": cos_p, "sin": sin_p, "cos_s": cos_s, "sin_s": sin_s,
        "dmask": dmask, "qdec": _per_head_lanes(q_dec), "kdec": _per_head_lanes(k_dec),
        "cdec": np.broadcast_to(chunk_dec[:, None, None], (N_HEADS, 1, HEAD_W)),
        "gamma_s": _per_head_lanes(gamma_s[None, :]),
    }
    tabs = {name: jnp.asarray(a) for name, a in tabs.items()}

    mod = _adaln(jnp.concatenate([c_sample, c_prompt], axis=0), w_ada, b_ada)
    mod_p = mod[:, n_dec:].reshape(depth, batch, N_MOD, D_MODEL)
    tril = np.tril(np.ones((CHUNK, CHUNK), dtype=bool))
    lw = {
        "g_mix": g_mix.reshape(depth, 1, D_MODEL), "g_ffn": g_ffn.reshape(depth, 1, D_MODEL),
        "w_in": _to_bf16(w_in), "w_out": _to_bf16(w_out),
        "w_up": _to_bf16(w_up, group_gate_columns=True), "w_down": _to_bf16(w_down),
        "gn": ret_gn_gain.reshape(depth, 1, RET_W), "ln": gmlp_ln_gain.reshape(depth, 1, GM_W),
        "ws_tril": jnp.where(tril, w_s, 0.0).astype(BF16),
        "bs_full": _per_head_lanes(b_s.transpose(0, 2, 1)),
        "ws0": _per_head_lanes(w_s[:, None, :, 0, 0]), "bs0": _per_head_lanes(b_s[:, None, :, 0]),
        "conv_w": conv_w, "conv_b": conv_b.reshape(depth, 1, 2 * D_FF),
    }
    g_fin = g_final.reshape(1, D_MODEL)

    xp, xs = x_prompt, x_sample.reshape(n_dec, D_MODEL)
    ret_p, conv_p, v_s = [], [], []
    ret_s = conv_s = None
    for l in range(depth):
        last = l == depth - 1
        xp, sp, bp = _prompt_layer(l, xp, mod_p, lw, tabs, g_fin, last)
        xs, ret_s, conv_s, vs = _sample_layer(l, xs, mod, state_ret, state_conv, ret_s, conv_s, lw, tabs,
                                              g_fin, last)
        ret_p.append(sp); conv_p.append(bp)
        v_s.append(vs.reshape(n_dec, dec_seq, GM_W))
    return (xp, xs.reshape(n_dec, dec_seq, D_MODEL), jnp.stack(ret_p), jnp.stack(conv_p),
            ret_s, conv_s, jnp.stack(v_s))
```

```python
import functools
import math

import jax
import jax.numpy as jnp
import numpy as np
from jax import lax
from jax.experimental import pallas as pl
from jax.experimental.pallas import tpu as pltpu

D_MODEL = 1024
N_HEADS = 4
HEAD_W = 128
RET_W = N_HEADS * HEAD_W
GM_W = N_HEADS * HEAD_W
IN_W = 4 * RET_W + 2 * GM_W
D_FF = 2048
CONV_W = 3
CHUNK = 128
ROPE_BASE = 10000.0
PAST_LEN = 16384
EPS = 1e-6
N_MOD = 6

V7X_VMEM_LIMIT_BYTES = 60 * 1024 * 1024
PROMPT_TILE = 512
ROW_BLOCK = 32
CONV_PAD = 8
GATE_GROUP_W = 512
DOWN_SPLIT_GROUPS = 2
SAMPLE_SEQ_BLOCK = 16

F32 = jnp.float32
BF16 = jnp.bfloat16
NT_DIMS = (((1,), (1,)), ((), ()))
TN_DIMS = (((0,), (0,)), ((), ()))


def _silu(x):
    return x * (1.0 / (1.0 + jnp.exp(-x)))


def _gelu_tanh(x):
    c = math.sqrt(2.0 / math.pi)
    return 0.5 * x * (1.0 + jnp.tanh(c * (x + 0.044715 * (x * x * x))))


def _rms_unit(x):
    return x * lax.rsqrt(jnp.mean(x * x, axis=-1, keepdims=True) + EPS)


def _center_unit(x):
    xc = x - jnp.mean(x, axis=-1, keepdims=True)
    return xc * lax.rsqrt(jnp.mean(xc * xc, axis=-1, keepdims=True) + EPS)


def _rope(x, cos_full, sin_signed):
    return x * cos_full + pltpu.roll(x, HEAD_W // 2, 1) * sin_signed


def _dot(a, b):
    return jnp.dot(a, b, preferred_element_type=F32)


NORM_BUNDLES = 60
RESID_BUNDLES = 30
CONV_BUNDLES = 170
HEAD_BUNDLES = 160
GROUP_BUNDLES = 130


def _two_unit_order(pieces, orders):
    unit_free = {unit: 0 for unit in orders}
    head = {unit: 0 for unit in orders}
    finish = {}
    order = []
    while any(head[unit] < len(names) for unit, names in orders.items()):
        ready = []
        for unit, names in orders.items():
            if head[unit] < len(names):
                name = names[head[unit]]
                deps = pieces[name][2]
                if all(d in finish for d in deps):
                    ready.append((max([unit_free[unit]] + [finish[d] for d in deps]), unit, name))
        assert ready, "piece lists contradict the dependencies"
        start, unit, name = min(ready)
        finish[name] = unit_free[unit] = start + pieces[name][1]
        head[unit] += 1
        order.append(name)
    return order


def _cast_body(*refs):
    o_ref = refs[-1]
    width = o_ref.shape[-1] // (len(refs) - 1)
    for i, w_ref in enumerate(refs[:-1]):
        o_ref[:, i * width:(i + 1) * width] = w_ref[...].astype(BF16)


def _to_bf16(w, group_gate_columns=False):
    depth, k, n = w.shape
    bn = 2 * GATE_GROUP_W
    if group_gate_columns:
        n_grp = n // bn
        srcs = [pl.BlockSpec((None, k, bn // 2), lambda l, c, half=half: (l, 0, half * n_grp + c)) for half in range(2)]
    else:
        srcs = [pl.BlockSpec((None, k, bn), lambda l, c: (l, 0, c))]
    return pl.pallas_call(
        _cast_body,
        grid=(depth, n // bn),
        in_specs=srcs,
        out_specs=pl.BlockSpec((None, k, bn), lambda l, c: (l, 0, c)),
        out_shape=jax.ShapeDtypeStruct(w.shape, BF16),
        compiler_params=pltpu.CompilerParams(
            dimension_semantics=("arbitrary", "arbitrary"), vmem_limit_bytes=V7X_VMEM_LIMIT_BYTES),
        name="to_bf16",
    )(*([w] * len(srcs)))


def _adaln_body(c_ref, w_ref, b_ref, o_ref):
    c = _silu(c_ref[...]).astype(BF16)
    o_ref[...] = _dot(c, w_ref[...].astype(BF16)) + b_ref[...]


def _adaln(c_all, w_ada, b_ada):
    depth = w_ada.shape[0]
    rows = c_all.shape[0]
    n_out = w_ada.shape[2]
    bn = 3 * D_MODEL
    return pl.pallas_call(
        _adaln_body,
        grid=(depth, n_out // bn),
        in_specs=[
            pl.BlockSpec((rows, D_MODEL), lambda l, j: (0, 0)),
            pl.BlockSpec((None, D_MODEL, bn), lambda l, j: (l, 0, j)),
            pl.BlockSpec((None, 1, bn), lambda l, j: (l, 0, j)),
        ],
        out_specs=pl.BlockSpec((None, rows, bn), lambda l, j: (l, 0, j)),
        out_shape=jax.ShapeDtypeStruct((depth, rows, n_out), F32),
        compiler_params=pltpu.CompilerParams(
            dimension_semantics=("arbitrary", "arbitrary"), vmem_limit_bytes=V7X_VMEM_LIMIT_BYTES),
        name="adaln",
    )(c_all, w_ada, b_ada.reshape(depth, 1, n_out))


def _prompt_layer_body(x_ref, modf_ref, modb_ref, gmix_ref, gffn_ref, win_ref, wout_ref, wup_ref, wdown_ref,
                       cos_ref, sin_ref, dmask_ref, qdec_ref, kdec_ref, cdec_ref, gn_ref, ln_ref,
                       ws_ref, bs_ref, convw_ref, convb_ref, gfin_ref,
                       y_ref, ret_ref, conv_ref,
                       h_ref, p_ref, mix_ref, mo_ref, x1_ref, h2_ref, f_ref, mo2_ref, *a_refs,
                       tm, tiles_per_seq, n_tiles, final_norm):
    s = pl.program_id(0)
    front_on = s < n_tiles
    t_front = lax.rem(jnp.minimum(s, n_tiles - 1), tiles_per_seq)
    t_back = lax.rem(jnp.maximum(s - 1, 0), tiles_per_seq)

    @pl.when(s == 0)
    def _():
        x1_ref[...] = jnp.zeros_like(x1_ref)
        h2_ref[...] = jnp.zeros_like(h2_ref)

    @pl.when(jnp.logical_and(front_on, t_front == 0))
    def _():
        ret_ref[...] = jnp.zeros_like(ret_ref)

    @pl.when(t_back == 0)
    def _():
        conv_ref[...] = jnp.zeros_like(conv_ref)

    sh1, sc1, gt1 = modf_ref[0:1, :], modf_ref[1:2, :], modf_ref[2:3, :]
    sh2, sc2 = modf_ref[3:4, :], modf_ref[4:5, :]
    gt2 = modb_ref[5:6, :]
    row_blocks = [slice(r * ROW_BLOCK, (r + 1) * ROW_BLOCK) for r in range(tm // ROW_BLOCK)]

    cb = GATE_GROUP_W
    tail = slice(CONV_PAD - (CONV_W - 1), CONV_PAD)

    pieces = {}
    n_rb = len(row_blocks)
    n_grp = D_FF // cb
    n_chunks = tm // CHUNK

    def matmul_bundles(k, n):
        return (tm // 16) * (k // 256) * (n // 256) * 8 // 2

    def up_group(j):
        for half in range(2):
            c0 = half * D_FF + j * cb
            a_refs[j][tail, half * cb:(half + 1) * cb] = conv_ref[:, c0:c0 + cb]
        a_refs[j][CONV_PAD:CONV_PAD + tm, :] = _dot(h2_ref[...], wup_ref[:, 2 * j * cb:2 * (j + 1) * cb])
        for half in range(2):
            c0 = half * D_FF + j * cb
            conv_ref[:, c0:c0 + cb] = a_refs[j][CONV_PAD + tm - (CONV_W - 1):CONV_PAD + tm,
                                                half * cb:(half + 1) * cb]

    def conv_rows(j, r):
        halves = []
        for half in range(2):
            c0 = half * D_FF + j * cb
            src = slice(half * cb, (half + 1) * cb)
            window = a_refs[j][r * ROW_BLOCK:CONV_PAD + (r + 1) * ROW_BLOCK, src]
            acc = convb_ref[:, c0:c0 + cb]
            for tap in range(CONV_W):
                back = CONV_W - 1 - tap
                rows_back = pltpu.roll(window, back, 0) if back else window
                acc = acc + rows_back[CONV_PAD:, :] * convw_ref[tap:tap + 1, c0:c0 + cb]
            halves.append(acc)
        f_ref[row_blocks[r], j * cb:(j + 1) * cb] = (_silu(halves[0]) * halves[1]).astype(BF16)

    def down_proj(g0, g1):
        part = _dot(f_ref[:, g0 * cb:g1 * cb], wdown_ref[g0 * cb:g1 * cb, :])
        mo2_ref[...] = part if g0 == 0 else mo2_ref[...] + part

    def resid_out(r):
        rows = row_blocks[r]
        x2 = y_ref[rows, :] + gt2 * mo2_ref[rows, :]
        if final_norm:
            x2 = _rms_unit(x2) * gfin_ref[...]
        y_ref[rows, :] = x2

    for j in range(n_grp):
        pieces[f"up{j}"] = ("mxu", matmul_bundles(D_MODEL, 2 * cb), (), functools.partial(up_group, j))
        for r in range(n_rb):
            pieces[f"conv{j}.{r}"] = ("valu", CONV_BUNDLES, (f"up{j}",), functools.partial(conv_rows, j, r))
    all_conv = tuple(f"conv{j}.{r}" for j in range(n_grp) for r in range(n_rb))
    split = DOWN_SPLIT_GROUPS
    pieces["down_a"] = ("mxu", matmul_bundles(split * cb, D_MODEL), all_conv[:split * n_rb],
                        functools.partial(down_proj, 0, split))
    pieces["down"] = ("mxu", matmul_bundles((n_grp - split) * cb, D_MODEL), all_conv + ("down_a",),
                      functools.partial(down_proj, split, n_grp))
    for r in range(n_rb):
        pieces[f"y{r}"] = ("valu", RESID_BUNDLES, ("down",), functools.partial(resid_out, r))

    def norm_rows(r):
        rows = row_blocks[r]
        h = _rms_unit(x_ref[rows, :]) * gmix_ref[...]
        h_ref[rows, :] = (h * (1.0 + sc1) + sh1).astype(BF16)

    def in_cols(c0, c1):
        p_ref[:, c0:c1] = _dot(h_ref[...], win_ref[:, c0:c1])

    all_norm = tuple(f"norm{r}" for r in range(n_rb))
    for r in range(n_rb):
        pieces[f"norm{r}"] = ("valu", NORM_BUNDLES, (), functools.partial(norm_rows, r))
    pieces["in_ret"] = ("mxu", matmul_bundles(D_MODEL, 4 * RET_W), all_norm, functools.partial(in_cols, 0, 4 * RET_W))
    pieces["in_gm"] = ("mxu", matmul_bundles(D_MODEL, 2 * GM_W), all_norm,
                       functools.partial(in_cols, 4 * RET_W, IN_W))

    def mix_heads(c):
        rows = slice(c * CHUNK, (c + 1) * CHUNK)
        heads = [slice(hh * HEAD_W, (hh + 1) * HEAD_W) for hh in range(N_HEADS)]
        cos_full, sin_signed = cos_ref[rows, :], sin_ref[rows, :]
        q = [_rope(p_ref[rows, hh * HEAD_W:(hh + 1) * HEAD_W], cos_full, sin_signed) for hh in range(N_HEADS)]
        k = [_rope(p_ref[rows, RET_W + hh * HEAD_W:RET_W + (hh + 1) * HEAD_W], cos_full, sin_signed)
             * (HEAD_W ** -0.5) for hh in range(N_HEADS)]
        vb = [p_ref[rows, 2 * RET_W + hh * HEAD_W:2 * RET_W + (hh + 1) * HEAD_W].astype(BF16)
              for hh in range(N_HEADS)]
        scores = [lax.dot_general(q[hh].astype(BF16), k[hh].astype(BF16), NT_DIMS, preferred_element_type=F32)
                  for hh in range(N_HEADS)]
        update = [lax.dot_general((k[hh] * kdec_ref[:, heads[hh]]).astype(BF16), vb[hh], TN_DIMS,
                                  preferred_element_type=F32) for hh in range(N_HEADS)]
        state = [ret_ref[hh] for hh in range(N_HEADS)]
        o = [_dot(jnp.concatenate([(scores[hh] * dmask_ref[hh]).astype(BF16),
                                   (q[hh] * qdec_ref[:, heads[hh]]).astype(BF16)], axis=1),
                  jnp.concatenate([vb[hh], state[hh].astype(BF16)], axis=0)) for hh in range(N_HEADS)]
        for hh in range(N_HEADS):
            ret_ref[hh] = jnp.where(front_on, cdec_ref[hh] * state[hh] + update[hh], state[hh])
        for hh in range(N_HEADS):
            on = _center_unit(o[hh]) * gn_ref[:, heads[hh]]
            g = p_ref[rows, 3 * RET_W + hh * HEAD_W:3 * RET_W + (hh + 1) * HEAD_W]
            mix_ref[rows, heads[hh]] = (_silu(g) * on).astype(BF16)

    def mix_groups(c):
        rows = slice(c * CHUNK, (c + 1) * CHUNK)
        groups = [slice(gg * HEAD_W, (gg + 1) * HEAD_W) for gg in range(N_HEADS)]
        vn = [(_center_unit(_gelu_tanh(p_ref[rows, 4 * RET_W + GM_W + gg * HEAD_W:4 * RET_W + GM_W + (gg + 1) * HEAD_W]))
               * ln_ref[:, groups[gg]]).astype(BF16) for gg in range(N_HEADS)]
        gate = [_dot(ws_ref[gg], vn[gg]) for gg in range(N_HEADS)]
        for gg in range(N_HEADS):
            u = _gelu_tanh(p_ref[rows, 4 * RET_W + gg * HEAD_W:4 * RET_W + (gg + 1) * HEAD_W])
            mix_ref[rows, RET_W + gg * HEAD_W:RET_W + (gg + 1) * HEAD_W] = (
                u * (gate[gg] + bs_ref[:, groups[gg]])).astype(BF16)

    def out_proj():
        mo_ref[...] = _dot(mix_ref[...], wout_ref[...])

    def resid_mid(r):
        rows = row_blocks[r]
        x1 = x_ref[rows, :] + gt1 * mo_ref[rows, :]
        x1_ref[rows, :] = x1
        h = _rms_unit(x1) * gffn_ref[...]
        h2_ref[rows, :] = (h * (1.0 + sc2) + sh2).astype(BF16)

    all_mix = []
    for c in range(n_chunks):
        pieces[f"heads{c}"] = ("valu", N_HEADS * HEAD_BUNDLES, ("in_ret",), functools.partial(mix_heads, c))
        pieces[f"groups{c}"] = ("valu", N_HEADS * GROUP_BUNDLES, ("in_gm",), functools.partial(mix_groups, c))
        all_mix += [f"heads{c}", f"groups{c}"]
    pieces["out"] = ("mxu", matmul_bundles(RET_W + GM_W, D_MODEL), tuple(all_mix), out_proj)
    all_up = tuple(f"up{j}" for j in range(n_grp))
    for r in range(n_rb):
        pieces[f"x1{r}"] = ("valu", RESID_BUNDLES + NORM_BUNDLES, ("out",) + all_up, functools.partial(resid_mid, r))

    orders = {
        "mxu": list(all_up) + ["in_ret", "in_gm", "down_a", "down", "out"],
        "valu": list(all_norm) + list(all_conv) + all_mix
        + [f"y{r}" for r in range(n_rb)] + [f"x1{r}" for r in range(n_rb)],
    }
    y_ref[...] = x1_ref[...]
    for name in _two_unit_order(pieces, orders):
        pieces[name][3]()


def _const_spec(shape):
    zeros = (0,) * len(shape)
    return pl.BlockSpec(shape, lambda s: zeros, pipeline_mode=pl.Buffered(1))


def _prompt_layer(layer, x, mod, lw, tabs, g_final, final_norm):
    batch, seq, _ = x.shape

    def layer_spec(name):
        shape = lw[name].shape
        zeros = (0,) * (len(shape) - 1)
        return pl.BlockSpec((None,) + tuple(shape[1:]), lambda s: (layer,) + zeros, pipeline_mode=pl.Buffered(1))

    tm = PROMPT_TILE
    tiles_per_seq = seq // tm
    n_tiles = batch * tiles_per_seq
    body = functools.partial(_prompt_layer_body, tm=tm, tiles_per_seq=tiles_per_seq, n_tiles=n_tiles,
                             final_norm=final_norm)

    def front(s):
        i = jnp.minimum(s, n_tiles - 1)
        return i // tiles_per_seq, lax.rem(i, tiles_per_seq)

    def back(s):
        i = jnp.maximum(s - 1, 0)
        return i // tiles_per_seq, lax.rem(i, tiles_per_seq)

    in_specs = [
        pl.BlockSpec((None, tm, D_MODEL), lambda s: (*front(s), 0)),
        pl.BlockSpec((None, None, N_MOD, D_MODEL), lambda s: (layer, front(s)[0], 0, 0)),
        pl.BlockSpec((None, None, N_MOD, D_MODEL), lambda s: (layer, back(s)[0], 0, 0)),
        layer_spec("g_mix"), layer_spec("g_ffn"),
        layer_spec("w_in"), layer_spec("w_out"), layer_spec("w_up"), layer_spec("w_down"),
        pl.BlockSpec((tm, HEAD_W), lambda s: (front(s)[1], 0)),
        pl.BlockSpec((tm, HEAD_W), lambda s: (front(s)[1], 0)),
        _const_spec((N_HEADS, CHUNK, CHUNK)), _const_spec((CHUNK, RET_W)), _const_spec((CHUNK, RET_W)),
        _const_spec((N_HEADS, 1, HEAD_W)), layer_spec("gn"), layer_spec("ln"),
        layer_spec("ws_tril"), layer_spec("bs_full"),
        layer_spec("conv_w"), layer_spec("conv_b"), _const_spec((1, D_MODEL)),
    ]
    out_specs = [
        pl.BlockSpec((None, tm, D_MODEL), lambda s: (*back(s), 0)),
        pl.BlockSpec((None, N_HEADS, HEAD_W, HEAD_W), lambda s: (front(s)[0], 0, 0, 0)),
        pl.BlockSpec((None, CONV_W - 1, 2 * D_FF), lambda s: (back(s)[0], 0, 0)),
    ]
    out_shape = [
        jax.ShapeDtypeStruct((batch, seq, D_MODEL), F32),
        jax.ShapeDtypeStruct((batch, N_HEADS, HEAD_W, HEAD_W), F32),
        jax.ShapeDtypeStruct((batch, CONV_W - 1, 2 * D_FF), F32),
    ]
    scratch = [
        pltpu.VMEM((tm, D_MODEL), BF16),
        pltpu.VMEM((tm, IN_W), F32),
        pltpu.VMEM((tm, RET_W + GM_W), BF16),
        pltpu.VMEM((tm, D_MODEL), F32),
        pltpu.VMEM((tm, D_MODEL), F32),
        pltpu.VMEM((tm, D_MODEL), BF16),
        pltpu.VMEM((tm, D_FF), BF16),
        pltpu.VMEM((tm, D_MODEL), F32),
    ] + [pltpu.VMEM((CONV_PAD + tm, 2 * 512), F32)] * (D_FF // 512)
    return pl.pallas_call(
        body,
        grid=(n_tiles + 1,),
        in_specs=in_specs,
        out_specs=out_specs,
        out_shape=out_shape,
        scratch_shapes=scratch,
        compiler_params=pltpu.CompilerParams(
            dimension_semantics=("arbitrary",), vmem_limit_bytes=V7X_VMEM_LIMIT_BYTES),
        name="prompt_layer",
    )(x, mod, mod, lw["g_mix"], lw["g_ffn"], lw["w_in"], lw["w_out"], lw["w_up"], lw["w_down"],
      tabs["cos"], tabs["sin"], tabs["dmask"], tabs["qdec"], tabs["kdec"], tabs["cdec"],
      lw["gn"], lw["ln"], lw["ws_tril"], lw["bs_full"], lw["conv_w"], lw["conv_b"], g_final)


def _sample_in_body(x_ref, mod_ref, gmix_ref, win_ref, cos_ref, sin_ref, ln_ref, ws0_ref, bs0_ref,
                    q_ref, k_ref, v_ref, sg_ref, ogm_ref, vn_ref):
    sh1, sc1 = mod_ref[:, 0:D_MODEL], mod_ref[:, D_MODEL:2 * D_MODEL]
    h = _rms_unit(x_ref[...]) * gmix_ref[...]
    h = (h * (1.0 + sc1) + sh1).astype(BF16)
    p = _dot(h, win_ref[...])
    for hh in range(N_HEADS):
        cs = slice(hh * HEAD_W, (hh + 1) * HEAD_W)
        q_ref[:, cs] = _rope(p[:, hh * HEAD_W:(hh + 1) * HEAD_W], cos_ref[...], sin_ref[...])
        k_ref[:, cs] = _rope(p[:, RET_W + hh * HEAD_W:RET_W + (hh + 1) * HEAD_W],
                             cos_ref[...], sin_ref[...]) * (HEAD_W ** -0.5)
        u = _gelu_tanh(p[:, 4 * RET_W + hh * HEAD_W:4 * RET_W + (hh + 1) * HEAD_W])
        vv = _gelu_tanh(p[:, 4 * RET_W + GM_W + hh * HEAD_W:4 * RET_W + GM_W + (hh + 1) * HEAD_W])
        vn = _center_unit(vv) * ln_ref[:, cs]
        vn_ref[:, cs] = vn
        ogm_ref[:, cs] = u * (ws0_ref[:, cs] * vn + bs0_ref[:, cs])
    v_ref[...] = p[:, 2 * RET_W:3 * RET_W]
    sg_ref[...] = _silu(p[:, 3 * RET_W:4 * RET_W])


def _sample_ret_body(q_ref, k_ref, v_ref, s_ref, gam_ref, *rest, sb, layer, first):
    o_ref, snew_ref = rest[-2:]
    if first:
        for other in range(snew_ref.shape[0]):
            if other != layer:
                snew_ref[other] = jnp.zeros(snew_ref.shape[1:], F32)
        snew_ref = snew_ref.at[layer]
    for hh in range(N_HEADS):
        cs = slice(hh * HEAD_W, (hh + 1) * HEAD_W)
        gam = gam_ref[:, cs]
        v = v_ref[:, cs]
        k = k_ref[:, cs]
        qb = q_ref[:, cs].astype(BF16)
        kt = jnp.concatenate([k, jnp.zeros((HEAD_W - sb, HEAD_W), F32)], axis=0).T
        for s in range(sb):
            state = s_ref[s, hh]
            snew_ref[s, hh] = gam * state + kt[:, s:s + 1] * v[s:s + 1, :]
            o_ref[s:s + 1, cs] = gam * _dot(qb, state.astype(BF16))[s:s + 1, :]
        qk = jnp.sum(q_ref[:, cs] * k, axis=-1, keepdims=True)
        o_ref[:, cs] = o_ref[:, cs] + qk * v


def _sample_out_body(x_ref, mod_ref, o_ref, sg_ref, ogm_ref, gn_ref, wout_ref, gffn_ref, wup_ref,
                     cbuf_ref, convw_ref, convb_ref, wdown_ref, gfin_ref, *rest, final_norm, first):
    y_ref, cnew_ref = rest[-2:]
    if first:
        @pl.when(pl.program_id(0) > 0)
        def _():
            cnew_ref[...] = jnp.zeros_like(cnew_ref)

        pl.when(pl.program_id(0) == 0)(functools.partial(
            _sample_out_layer, x_ref, mod_ref, o_ref, sg_ref, ogm_ref, gn_ref, wout_ref, gffn_ref, wup_ref,
            cbuf_ref, convw_ref, convb_ref, wdown_ref, gfin_ref, y_ref, cnew_ref, final_norm))
    else:
        _sample_out_layer(x_ref, mod_ref, o_ref, sg_ref, ogm_ref, gn_ref, wout_ref, gffn_ref, wup_ref,
                          cbuf_ref, convw_ref, convb_ref, wdown_ref, gfin_ref, y_ref, cnew_ref, final_norm)


def _sample_out_layer(x_ref, mod_ref, o_ref, sg_ref, ogm_ref, gn_ref, wout_ref, gffn_ref, wup_ref,
                      cbuf_ref, convw_ref, convb_ref, wdown_ref, gfin_ref, y_ref, cnew_ref, final_norm):
    gt1 = mod_ref[:, 2 * D_MODEL:3 * D_MODEL]
    sh2, sc2 = mod_ref[:, 3 * D_MODEL:4 * D_MODEL], mod_ref[:, 4 * D_MODEL:5 * D_MODEL]
    gt2 = mod_ref[:, 5 * D_MODEL:6 * D_MODEL]
    parts = []
    for hh in range(N_HEADS):
        cs = slice(hh * HEAD_W, (hh + 1) * HEAD_W)
        on = _center_unit(o_ref[:, cs]) * gn_ref[:, cs]
        parts.append((sg_ref[:, cs] * on).astype(BF16))
    parts.append(ogm_ref[...].astype(BF16))
    mix = _dot(jnp.concatenate(parts, axis=1), wout_ref[...])
    x1 = x_ref[...] + gt1 * mix
    h = _rms_unit(x1) * gffn_ref[...]
    h = (h * (1.0 + sc2) + sh2).astype(BF16)
    a = _dot(h, wup_ref[...])
    gw = GATE_GROUP_W
    f_parts = []
    for j in range(D_FF // gw):
        halves = []
        for half in range(2):
            c0 = half * D_FF + j * gw
            a_blk = a[:, (2 * j + half) * gw:(2 * j + half + 1) * gw]
            buf0, buf1 = cbuf_ref[:, 0, c0:c0 + gw], cbuf_ref[:, 1, c0:c0 + gw]
            halves.append(convb_ref[:, c0:c0 + gw] + buf0 * convw_ref[0:1, c0:c0 + gw]
                          + buf1 * convw_ref[1:2, c0:c0 + gw] + a_blk * convw_ref[2:3, c0:c0 + gw])
            cnew_ref[:, 0, c0:c0 + gw] = buf1
            cnew_ref[:, 1, c0:c0 + gw] = a_blk
        f_parts.append((_silu(halves[0]) * halves[1]).astype(BF16))
    f = jnp.concatenate(f_parts, axis=1)
    x2 = x1 + gt2 * _dot(f, wdown_ref[...])
    if final_norm:
        x2 = _rms_unit(x2) * gfin_ref[...]
    y_ref[...] = x2


def _whole(shape, pipeline_mode=None):
    zeros = (0,) * len(shape)
    return pl.BlockSpec(shape, lambda *_: zeros, pipeline_mode=pipeline_mode)


def _layer_slab(shape, layer, pipeline_mode=None):
    zeros = (0,) * (len(shape) - 1)
    return pl.BlockSpec((None,) + tuple(shape[1:]), lambda *_: (layer,) + zeros, pipeline_mode=pipeline_mode)


def _sample_layer(layer, x, mod, state_ret, state_conv, ret_acc, conv_acc, lw, tabs, g_final, final_norm):
    n = x.shape[0]
    params = pltpu.CompilerParams(dimension_semantics=("arbitrary",), vmem_limit_bytes=V7X_VMEM_LIMIT_BYTES)
    act = jax.ShapeDtypeStruct((n, RET_W), F32)
    in_args = (x, mod, lw["g_mix"], lw["w_in"], tabs["cos_s"], tabs["sin_s"], lw["ln"], lw["ws0"], lw["bs0"])
    in_specs = [_whole(a.shape) for a in in_args]
    in_specs[1] = pl.BlockSpec((None, n, N_MOD * D_MODEL), lambda i: (layer, 0, 0))
    for i in (2, 3, 6, 7, 8):
        in_specs[i] = _layer_slab(in_args[i].shape, layer)
    q, k, v, sg, ogm, vn = pl.pallas_call(
        _sample_in_body,
        grid=(1,),
        in_specs=in_specs,
        out_specs=[_whole(act.shape)] * 6,
        out_shape=[act] * 6,
        compiler_params=params,
        name="sample_in",
    )(*in_args)

    sb = SAMPLE_SEQ_BLOCK
    row_spec = pl.BlockSpec((sb, RET_W), lambda i: (i, 0))
    state_spec = pl.BlockSpec((None, sb, N_HEADS, HEAD_W, HEAD_W), lambda i: (layer, i, 0, 0, 0))
    carried = [] if ret_acc is None else [ret_acc]
    depth = state_ret.shape[0]
    first = ret_acc is None
    all_layers_spec = pl.BlockSpec((depth, sb, N_HEADS, HEAD_W, HEAD_W), lambda i: (0, i, 0, 0, 0))
    o, ret_acc = pl.pallas_call(
        functools.partial(_sample_ret_body, sb=sb, layer=layer, first=first),
        grid=(n // sb,),
        in_specs=[row_spec, row_spec, row_spec, state_spec, _whole((1, RET_W))]
        + [pl.BlockSpec(memory_space=pl.ANY)] * len(carried),
        out_specs=[row_spec, all_layers_spec if first else state_spec],
        out_shape=[act, jax.ShapeDtypeStruct(state_ret.shape, F32)],
        input_output_aliases={5: 1} if carried else {},
        compiler_params=params,
        name="sample_ret",
    )(q, k, v, state_ret, tabs["gamma_s"], *carried)

    out_args = (x, mod, o, sg, ogm, lw["gn"], lw["w_out"], lw["g_ffn"], lw["w_up"], state_conv,
                lw["conv_w"], lw["conv_b"], lw["w_down"], g_final)
    once = pl.Buffered(1)
    in_specs = [_whole(a.shape, once) for a in out_args]
    in_specs[1] = pl.BlockSpec((None, n, N_MOD * D_MODEL), lambda i: (layer, 0, 0), pipeline_mode=once)
    for i in (5, 6, 7, 8, 9, 10, 11, 12):
        in_specs[i] = _layer_slab(out_args[i].shape, layer, once)
    first = conv_acc is None
    carried = [] if first else [conv_acc]
    slab_spec = pl.BlockSpec((None,) + tuple(state_conv.shape[1:]),
                             lambda i: (lax.rem(layer + i, depth), 0, 0, 0), pipeline_mode=once)

    y, conv_acc = pl.pallas_call(
        functools.partial(_sample_out_body, final_norm=final_norm, first=first),
        grid=(depth if first else 1,),
        in_specs=in_specs + [pl.BlockSpec(memory_space=pl.ANY)] * len(carried),
        out_specs=[_whole((n, D_MODEL)), slab_spec],
        out_shape=[jax.ShapeDtypeStruct((n, D_MODEL), F32), jax.ShapeDtypeStruct(state_conv.shape, F32)],
        input_output_aliases={len(out_args): 1} if carried else {},
        compiler_params=params,
        name="sample_out",
    )(*out_args, *carried)
    return y, ret_acc, conv_acc, vn


def _rope_tables(pos):
    half = HEAD_W // 2
    freqs = np.exp(-math.log(ROPE_BASE) * np.arange(half, dtype=np.float64) / half)
    ang = np.asarray(pos, dtype=np.float64)[:, None] * freqs[None, :]
    cos, sin = np.cos(ang), np.sin(ang)
    return (np.concatenate([cos, cos], axis=-1).astype(np.float32),
            np.concatenate([-sin, sin], axis=-1).astype(np.float32))


def _decay_tables(chunk):
    lg = np.log1p(-np.exp2(-5.0 - np.arange(N_HEADS, dtype=np.float64)))
    i = np.arange(chunk, dtype=np.float64)
    diff = i[:, None] - i[None, :]
    dmask = np.where(diff[None] >= 0.0, np.exp(np.maximum(diff, 0.0)[None] * lg[:, None, None]), 0.0)
    q_dec = np.exp((i[:, None] + 1.0) * lg[None, :])
    k_dec = np.exp((chunk - 1.0 - i)[:, None] * lg[None, :])
    chunk_dec = np.exp(chunk * lg)
    return tuple(a.astype(np.float32) for a in (dmask, q_dec, k_dec, chunk_dec))


def _per_head_lanes(a):
    return a.repeat(HEAD_W, axis=-1)


def kernel(x_prompt, x_sample, state_ret, state_conv, c_prompt, c_sample, w_ada, b_ada, g_mix, w_in,
           ret_gn_gain, gmlp_ln_gain, w_s, b_s, w_out, g_ffn, w_up, conv_w, conv_b, w_down, g_final):
    depth = w_in.shape[0]
    batch, seq, _ = x_prompt.shape
    n_dec, dec_seq, _ = x_sample.shape
    assert dec_seq == 1 and seq % PROMPT_TILE == 0 and PROMPT_TILE % CHUNK == 0

    cos_p, sin_p = _rope_tables(np.arange(seq))
    cos_s, sin_s = _rope_tables(PAST_LEN + np.arange(dec_seq))
    dmask, q_dec, k_dec, chunk_dec = _decay_tables(CHUNK)
    _, _, _, gamma_s = _decay_tables(dec_seq)
    tabs = {
        "cos": cos_p, "sin": sin_p, "cos_s": cos_s, "sin_s": sin_s,
        "dmask": dmask, "qdec": _per_head_lanes(q_dec), "kdec": _per_head_lanes(k_dec),
        "cdec": np.broadcast_to(chunk_dec[:, None, None], (N_HEADS, 1, HEAD_W)),
        "gamma_s": _per_head_lanes(gamma_s[None, :]),
    }
    tabs = {name: jnp.asarray(a) for name, a in tabs.items()}

    mod = _adaln(jnp.concatenate([c_sample, c_prompt], axis=0), w_ada, b_ada)
    mod_p = mod[:, n_dec:].reshape(depth, batch, N_MOD, D_MODEL)
    tril = np.tril(np.ones((CHUNK, CHUNK), dtype=bool))
    lw = {
        "g_mix": g_mix.reshape(depth, 1, D_MODEL), "g_ffn": g_ffn.reshape(depth, 1, D_MODEL),
        "w_in": _to_bf16(w_in), "w_out": _to_bf16(w_out),
        "w_up": _to_bf16(w_up, group_gate_columns=True), "w_down": _to_bf16(w_down),
        "gn": ret_gn_gain.reshape(depth, 1, RET_W), "ln": gmlp_ln_gain.reshape(depth, 1, GM_W),
        "ws_tril": jnp.where(tril, w_s, 0.0).astype(BF16),
        "bs_full": _per_head_lanes(b_s.transpose(0, 2, 1)),
        "ws0": _per_head_lanes(w_s[:, None, :, 0, 0]), "bs0": _per_head_lanes(b_s[:, None, :, 0]),
        "conv_w": conv_w, "conv_b": conv_b.reshape(depth, 1, 2 * D_FF),
    }
    g_fin = g_final.reshape(1, D_MODEL)

    xp, xs = x_prompt, x_sample.reshape(n_dec, D_MODEL)
    ret_p, conv_p, v_s = [], [], []
    ret_s = conv_s = None
    for l in range(depth):
        last = l == depth - 1
        xp, sp, bp = _prompt_layer(l, xp, mod_p, lw, tabs, g_fin, last)
        xs, ret_s, conv_s, vs = _sample_layer(l, xs, mod, state_ret, state_conv, ret_s, conv_s, lw, tabs,
                                              g_fin, last)
        ret_p.append(sp); conv_p.append(bp)
        v_s.append(vs.reshape(n_dec, dec_seq, GM_W))
    return (xp, xs.reshape(n_dec, dec_seq, D_MODEL), jnp.stack(ret_p), jnp.stack(conv_p),
            ret_s, conv_s, jnp.stack(v_s))
```

```python
import functools
import math

import jax
import jax.numpy as jnp
import numpy as np
from jax import lax
from jax.experimental import pallas as pl
from jax.experimental.pallas import tpu as pltpu

D_MODEL = 1024
N_HEADS = 4
HEAD_W = 128
RET_W = N_HEADS * HEAD_W
GM_W = N_HEADS * HEAD_W
IN_W = 4 * RET_W + 2 * GM_W
D_FF = 2048
CONV_W = 3
CHUNK = 128
ROPE_BASE = 10000.0
PAST_LEN = 16384
EPS = 1e-6
N_MOD = 6

V7X_VMEM_LIMIT_BYTES = 60 * 1024 * 1024
PROMPT_TILE = 512
ROW_BLOCK = 32
CONV_PAD = 8
GATE_GROUP_W = 256
CAST_COLS = 1024
DOWN_SPLIT_GROUPS = 4
SAMPLE_SEQ_BLOCK = 16

F32 = jnp.float32
BF16 = jnp.bfloat16
NT_DIMS = (((1,), (1,)), ((), ()))
TN_DIMS = (((0,), (0,)), ((), ()))


def _silu(x):
    return x * (1.0 / (1.0 + jnp.exp(-x)))


def _gelu_tanh(x):
    c = math.sqrt(2.0 / math.pi)
    return 0.5 * x * (1.0 + jnp.tanh(c * (x + 0.044715 * (x * x * x))))


def _rms_unit(x):
    return x * lax.rsqrt(jnp.mean(x * x, axis=-1, keepdims=True) + EPS)


def _center_unit(x):
    xc = x - jnp.mean(x, axis=-1, keepdims=True)
    return xc * lax.rsqrt(jnp.mean(xc * xc, axis=-1, keepdims=True) + EPS)


def _rope(x, cos_full, sin_signed):
    return x * cos_full + pltpu.roll(x, HEAD_W // 2, 1) * sin_signed


def _dot(a, b):
    return jnp.dot(a, b, preferred_element_type=F32)


NORM_BUNDLES = 60
RESID_BUNDLES = 30
CONV_BUNDLES = 170
HEAD_BUNDLES = 160
GROUP_BUNDLES = 130


def _two_unit_order(pieces, orders):
    unit_free = {unit: 0 for unit in orders}
    head = {unit: 0 for unit in orders}
    finish = {}
    order = []
    while any(head[unit] < len(names) for unit, names in orders.items()):
        ready = []
        for unit, names in orders.items():
            if head[unit] < len(names):
                name = names[head[unit]]
                deps = pieces[name][2]
                if all(d in finish for d in deps):
                    ready.append((max([unit_free[unit]] + [finish[d] for d in deps]), unit, name))
        assert ready, "piece lists contradict the dependencies"
        start, unit, name = min(ready)
        finish[name] = unit_free[unit] = start + pieces[name][1]
        head[unit] += 1
        order.append(name)
    return order


def _cast_body(*refs):
    o_ref = refs[-1]
    width = o_ref.shape[-1] // (len(refs) - 1)
    for i, w_ref in enumerate(refs[:-1]):
        o_ref[:, i * width:(i + 1) * width] = w_ref[...].astype(BF16)


def _to_bf16(w, group_gate_columns=False):
    depth, k, n = w.shape
    if group_gate_columns:
        bn = 2 * GATE_GROUP_W
        n_grp = n // bn
        srcs = [pl.BlockSpec((None, k, bn // 2), lambda l, c, half=half: (l, 0, half * n_grp + c)) for half in range(2)]
    else:
        bn = CAST_COLS
        srcs = [pl.BlockSpec((None, k, bn), lambda l, c: (l, 0, c))]
    return pl.pallas_call(
        _cast_body,
        grid=(depth, n // bn),
        in_specs=srcs,
        out_specs=pl.BlockSpec((None, k, bn), lambda l, c: (l, 0, c)),
        out_shape=jax.ShapeDtypeStruct(w.shape, BF16),
        compiler_params=pltpu.CompilerParams(
            dimension_semantics=("arbitrary", "arbitrary"), vmem_limit_bytes=V7X_VMEM_LIMIT_BYTES),
        name="to_bf16",
    )(*([w] * len(srcs)))


def _adaln_body(c_ref, w_ref, b_ref, o_ref):
    c = _silu(c_ref[...]).astype(BF16)
    o_ref[...] = _dot(c, w_ref[...].astype(BF16)) + b_ref[...]


def _adaln(c_all, w_ada, b_ada):
    depth = w_ada.shape[0]
    rows = c_all.shape[0]
    n_out = w_ada.shape[2]
    bn = 3 * D_MODEL
    return pl.pallas_call(
        _adaln_body,
        grid=(depth, n_out // bn),
        in_specs=[
            pl.BlockSpec((rows, D_MODEL), lambda l, j: (0, 0)),
            pl.BlockSpec((None, D_MODEL, bn), lambda l, j: (l, 0, j)),
            pl.BlockSpec((None, 1, bn), lambda l, j: (l, 0, j)),
        ],
        out_specs=pl.BlockSpec((None, rows, bn), lambda l, j: (l, 0, j)),
        out_shape=jax.ShapeDtypeStruct((depth, rows, n_out), F32),
        compiler_params=pltpu.CompilerParams(
            dimension_semantics=("arbitrary", "arbitrary"), vmem_limit_bytes=V7X_VMEM_LIMIT_BYTES),
        name="adaln",
    )(c_all, w_ada, b_ada.reshape(depth, 1, n_out))


def _prompt_layer_body(x_ref, modf_ref, modb_ref, gmix_ref, gffn_ref, win_ref, wout_ref, wup_ref, wdown_ref,
                       cos_ref, sin_ref, dmask_ref, qdec_ref, kdec_ref, cdec_ref, gn_ref, ln_ref,
                       ws_ref, bs_ref, convw_ref, convb_ref, gfin_ref,
                       y_ref, ret_ref, conv_ref,
                       h_ref, p_ref, mix_ref, mo_ref, x1_ref, h2_ref, f_ref, mo2_ref, *a_refs,
                       tm, tiles_per_seq, n_tiles, final_norm):
    s = pl.program_id(0)
    front_on = s < n_tiles
    t_front = lax.rem(jnp.minimum(s, n_tiles - 1), tiles_per_seq)
    t_back = lax.rem(jnp.maximum(s - 1, 0), tiles_per_seq)

    @pl.when(s == 0)
    def _():
        x1_ref[...] = jnp.zeros_like(x1_ref)
        h2_ref[...] = jnp.zeros_like(h2_ref)

    @pl.when(jnp.logical_and(front_on, t_front == 0))
    def _():
        ret_ref[...] = jnp.zeros_like(ret_ref)

    @pl.when(t_back == 0)
    def _():
        conv_ref[...] = jnp.zeros_like(conv_ref)

    sh1, sc1, gt1 = modf_ref[0:1, :], modf_ref[1:2, :], modf_ref[2:3, :]
    sh2, sc2 = modf_ref[3:4, :], modf_ref[4:5, :]
    gt2 = modb_ref[5:6, :]
    row_blocks = [slice(r * ROW_BLOCK, (r + 1) * ROW_BLOCK) for r in range(tm // ROW_BLOCK)]

    cb = GATE_GROUP_W
    tail = slice(CONV_PAD - (CONV_W - 1), CONV_PAD)

    pieces = {}
    n_rb = len(row_blocks)
    n_grp = D_FF // cb
    n_chunks = tm // CHUNK

    def matmul_bundles(k, n):
        return (tm // 16) * (k // 256) * (n // 256) * 8 // 2

    def up_group(j):
        for half in range(2):
            c0 = half * D_FF + j * cb
            a_refs[j][tail, half * cb:(half + 1) * cb] = conv_ref[:, c0:c0 + cb]
        a_refs[j][CONV_PAD:CONV_PAD + tm, :] = _dot(h2_ref[...], wup_ref[:, 2 * j * cb:2 * (j + 1) * cb])
        for half in range(2):
            c0 = half * D_FF + j * cb
            conv_ref[:, c0:c0 + cb] = a_refs[j][CONV_PAD + tm - (CONV_W - 1):CONV_PAD + tm,
                                                half * cb:(half + 1) * cb]

    def conv_rows(j, r):
        halves = []
        for half in range(2):
            c0 = half * D_FF + j * cb
            src = slice(half * cb, (half + 1) * cb)
            window = a_refs[j][r * ROW_BLOCK:CONV_PAD + (r + 1) * ROW_BLOCK, src]
            acc = convb_ref[:, c0:c0 + cb]
            for tap in range(CONV_W):
                back = CONV_W - 1 - tap
                rows_back = pltpu.roll(window, back, 0) if back else window
                acc = acc + rows_back[CONV_PAD:, :] * convw_ref[tap:tap + 1, c0:c0 + cb]
            halves.append(acc)
        f_ref[row_blocks[r], j * cb:(j + 1) * cb] = (_silu(halves[0]) * halves[1]).astype(BF16)

    def down_proj(g0, g1):
        part = _dot(f_ref[:, g0 * cb:g1 * cb], wdown_ref[g0 * cb:g1 * cb, :])
        mo2_ref[...] = part if g0 == 0 else mo2_ref[...] + part

    def resid_out(r):
        rows = row_blocks[r]
        x2 = y_ref[rows, :] + gt2 * mo2_ref[rows, :]
        if final_norm:
            x2 = _rms_unit(x2) * gfin_ref[...]
        y_ref[rows, :] = x2

    for j in range(n_grp):
        pieces[f"up{j}"] = ("mxu", matmul_bundles(D_MODEL, 2 * cb), (), functools.partial(up_group, j))
        for r in range(n_rb):
            pieces[f"conv{j}.{r}"] = ("valu", CONV_BUNDLES, (f"up{j}",), functools.partial(conv_rows, j, r))
    all_conv = tuple(f"conv{j}.{r}" for j in range(n_grp) for r in range(n_rb))
    split = DOWN_SPLIT_GROUPS
    pieces["down_a"] = ("mxu", matmul_bundles(split * cb, D_MODEL), all_conv[:split * n_rb],
                        functools.partial(down_proj, 0, split))
    pieces["down"] = ("mxu", matmul_bundles((n_grp - split) * cb, D_MODEL), all_conv + ("down_a",),
                      functools.partial(down_proj, split, n_grp))
    for r in range(n_rb):
        pieces[f"y{r}"] = ("valu", RESID_BUNDLES, ("down",), functools.partial(resid_out, r))

    def norm_rows(r):
        rows = row_blocks[r]
        h = _rms_unit(x_ref[rows, :]) * gmix_ref[...]
        h_ref[rows, :] = (h * (1.0 + sc1) + sh1).astype(BF16)

    def in_cols(c0, c1):
        p_ref[:, c0:c1] = _dot(h_ref[...], win_ref[:, c0:c1])

    all_norm = tuple(f"norm{r}" for r in range(n_rb))
    for r in range(n_rb):
        pieces[f"norm{r}"] = ("valu", NORM_BUNDLES, (), functools.partial(norm_rows, r))
    pieces["in_ret"] = ("mxu", matmul_bundles(D_MODEL, 4 * RET_W), all_norm, functools.partial(in_cols, 0, 4 * RET_W))
    pieces["in_gm"] = ("mxu", matmul_bundles(D_MODEL, 2 * GM_W), all_norm,
                       functools.partial(in_cols, 4 * RET_W, IN_W))

    def mix_heads(c):
        rows = slice(c * CHUNK, (c + 1) * CHUNK)
        heads = [slice(hh * HEAD_W, (hh + 1) * HEAD_W) for hh in range(N_HEADS)]
        cos_full, sin_signed = cos_ref[rows, :], sin_ref[rows, :]
        q = [_rope(p_ref[rows, hh * HEAD_W:(hh + 1) * HEAD_W], cos_full, sin_signed) for hh in range(N_HEADS)]
        k = [_rope(p_ref[rows, RET_W + hh * HEAD_W:RET_W + (hh + 1) * HEAD_W], cos_full, sin_signed)
             * (HEAD_W ** -0.5) for hh in range(N_HEADS)]
        vb = [p_ref[rows, 2 * RET_W + hh * HEAD_W:2 * RET_W + (hh + 1) * HEAD_W].astype(BF16)
              for hh in range(N_HEADS)]
        scores = [lax.dot_general(q[hh].astype(BF16), k[hh].astype(BF16), NT_DIMS, preferred_element_type=F32)
                  for hh in range(N_HEADS)]
        update = [lax.dot_general((k[hh] * kdec_ref[:, heads[hh]]).astype(BF16), vb[hh], TN_DIMS,
                                  preferred_element_type=F32) for hh in range(N_HEADS)]
        state = [ret_ref[hh] for hh in range(N_HEADS)]
        o = [_dot(jnp.concatenate([(scores[hh] * dmask_ref[hh]).astype(BF16),
                                   (q[hh] * qdec_ref[:, heads[hh]]).astype(BF16)], axis=1),
                  jnp.concatenate([vb[hh], state[hh].astype(BF16)], axis=0)) for hh in range(N_HEADS)]
        for hh in range(N_HEADS):
            ret_ref[hh] = jnp.where(front_on, cdec_ref[hh] * state[hh] + update[hh], state[hh])
        for hh in range(N_HEADS):
            on = _center_unit(o[hh]) * gn_ref[:, heads[hh]]
            g = p_ref[rows, 3 * RET_W + hh * HEAD_W:3 * RET_W + (hh + 1) * HEAD_W]
            mix_ref[rows, heads[hh]] = (_silu(g) * on).astype(BF16)

    def mix_groups(c):
        rows = slice(c * CHUNK, (c + 1) * CHUNK)
        groups = [slice(gg * HEAD_W, (gg + 1) * HEAD_W) for gg in range(N_HEADS)]
        vn = [(_center_unit(_gelu_tanh(p_ref[rows, 4 * RET_W + GM_W + gg * HEAD_W:4 * RET_W + GM_W + (gg + 1) * HEAD_W]))
               * ln_ref[:, groups[gg]]).astype(BF16) for gg in range(N_HEADS)]
        gate = [_dot(ws_ref[gg], vn[gg]) for gg in range(N_HEADS)]
        for gg in range(N_HEADS):
            u = _gelu_tanh(p_ref[rows, 4 * RET_W + gg * HEAD_W:4 * RET_W + (gg + 1) * HEAD_W])
            mix_ref[rows, RET_W + gg * HEAD_W:RET_W + (gg + 1) * HEAD_W] = (
                u * (gate[gg] + bs_ref[:, groups[gg]])).astype(BF16)

    def out_proj():
        mo_ref[...] = _dot(mix_ref[...], wout_ref[...])

    def resid_mid(r):
        rows = row_blocks[r]
        x1 = x_ref[rows, :] + gt1 * mo_ref[rows, :]
        x1_ref[rows, :] = x1
        h = _rms_unit(x1) * gffn_ref[...]
        h2_ref[rows, :] = (h * (1.0 + sc2) + sh2).astype(BF16)

    all_mix = []
    for c in range(n_chunks):
        pieces[f"heads{c}"] = ("valu", N_HEADS * HEAD_BUNDLES, ("in_ret",), functools.partial(mix_heads, c))
        pieces[f"groups{c}"] = ("valu", N_HEADS * GROUP_BUNDLES, ("in_gm",), functools.partial(mix_groups, c))
        all_mix += [f"heads{c}", f"groups{c}"]
    pieces["out"] = ("mxu", matmul_bundles(RET_W + GM_W, D_MODEL), tuple(all_mix), out_proj)
    all_up = tuple(f"up{j}" for j in range(n_grp))
    for r in range(n_rb):
        pieces[f"x1{r}"] = ("valu", RESID_BUNDLES + NORM_BUNDLES, ("out",) + all_up, functools.partial(resid_mid, r))

    orders = {
        "mxu": list(all_up) + ["in_ret", "in_gm", "down_a", "down", "out"],
        "valu": list(all_norm) + list(all_conv) + all_mix
        + [f"y{r}" for r in range(n_rb)] + [f"x1{r}" for r in range(n_rb)],
    }
    y_ref[...] = x1_ref[...]
    for name in _two_unit_order(pieces, orders):
        pieces[name][3]()


def _const_spec(shape):
    zeros = (0,) * len(shape)
    return pl.BlockSpec(shape, lambda s: zeros, pipeline_mode=pl.Buffered(1))


def _prompt_layer(layer, x, mod, lw, tabs, g_final, final_norm):
    batch, seq, _ = x.shape

    def layer_spec(name):
        shape = lw[name].shape
        zeros = (0,) * (len(shape) - 1)
        return pl.BlockSpec((None,) + tuple(shape[1:]), lambda s: (layer,) + zeros, pipeline_mode=pl.Buffered(1))

    tm = PROMPT_TILE
    tiles_per_seq = seq // tm
    n_tiles = batch * tiles_per_seq
    body = functools.partial(_prompt_layer_body, tm=tm, tiles_per_seq=tiles_per_seq, n_tiles=n_tiles,
                             final_norm=final_norm)

    def front(s):
        i = jnp.minimum(s, n_tiles - 1)
        return i // tiles_per_seq, lax.rem(i, tiles_per_seq)

    def back(s):
        i = jnp.maximum(s - 1, 0)
        return i // tiles_per_seq, lax.rem(i, tiles_per_seq)

    in_specs = [
        pl.BlockSpec((None, tm, D_MODEL), lambda s: (*front(s), 0)),
        pl.BlockSpec((None, None, N_MOD, D_MODEL), lambda s: (layer, front(s)[0], 0, 0)),
        pl.BlockSpec((None, None, N_MOD, D_MODEL), lambda s: (layer, back(s)[0], 0, 0)),
        layer_spec("g_mix"), layer_spec("g_ffn"),
        layer_spec("w_in"), layer_spec("w_out"), layer_spec("w_up"), layer_spec("w_down"),
        pl.BlockSpec((tm, HEAD_W), lambda s: (front(s)[1], 0)),
        pl.BlockSpec((tm, HEAD_W), lambda s: (front(s)[1], 0)),
        _const_spec((N_HEADS, CHUNK, CHUNK)), _const_spec((CHUNK, RET_W)), _const_spec((CHUNK, RET_W)),
        _const_spec((N_HEADS, 1, HEAD_W)), layer_spec("gn"), layer_spec("ln"),
        layer_spec("ws_tril"), layer_spec("bs_full"),
        layer_spec("conv_w"), layer_spec("conv_b"), _const_spec((1, D_MODEL)),
    ]
    out_specs = [
        pl.BlockSpec((None, tm, D_MODEL), lambda s: (*back(s), 0)),
        pl.BlockSpec((None, N_HEADS, HEAD_W, HEAD_W), lambda s: (front(s)[0], 0, 0, 0)),
        pl.BlockSpec((None, CONV_W - 1, 2 * D_FF), lambda s: (back(s)[0], 0, 0)),
    ]
    out_shape = [
        jax.ShapeDtypeStruct((batch, seq, D_MODEL), F32),
        jax.ShapeDtypeStruct((batch, N_HEADS, HEAD_W, HEAD_W), F32),
        jax.ShapeDtypeStruct((batch, CONV_W - 1, 2 * D_FF), F32),
    ]
    scratch = [
        pltpu.VMEM((tm, D_MODEL), BF16),
        pltpu.VMEM((tm, IN_W), F32),
        pltpu.VMEM((tm, RET_W + GM_W), BF16),
        pltpu.VMEM((tm, D_MODEL), F32),
        pltpu.VMEM((tm, D_MODEL), F32),
        pltpu.VMEM((tm, D_MODEL), BF16),
        pltpu.VMEM((tm, D_FF), BF16),
        pltpu.VMEM((tm, D_MODEL), F32),
    ] + [pltpu.VMEM((CONV_PAD + tm, 2 * GATE_GROUP_W), F32)] * (D_FF // GATE_GROUP_W)
    return pl.pallas_call(
        body,
        grid=(n_tiles + 1,),
        in_specs=in_specs,
        out_specs=out_specs,
        out_shape=out_shape,
        scratch_shapes=scratch,
        compiler_params=pltpu.CompilerParams(
            dimension_semantics=("arbitrary",), vmem_limit_bytes=V7X_VMEM_LIMIT_BYTES),
        name="prompt_layer",
    )(x, mod, mod, lw["g_mix"], lw["g_ffn"], lw["w_in"], lw["w_out"], lw["w_up"], lw["w_down"],
      tabs["cos"], tabs["sin"], tabs["dmask"], tabs["qdec"], tabs["kdec"], tabs["cdec"],
      lw["gn"], lw["ln"], lw["ws_tril"], lw["bs_full"], lw["conv_w"], lw["conv_b"], g_final)


def _sample_in_body(x_ref, mod_ref, gmix_ref, win_ref, cos_ref, sin_ref, ln_ref, ws0_ref, bs0_ref,
                    q_ref, k_ref, v_ref, sg_ref, ogm_ref, vn_ref):
    sh1, sc1 = mod_ref[:, 0:D_MODEL], mod_ref[:, D_MODEL:2 * D_MODEL]
    h = _rms_unit(x_ref[...]) * gmix_ref[...]
    h = (h * (1.0 + sc1) + sh1).astype(BF16)
    p = _dot(h, win_ref[...])
    for hh in range(N_HEADS):
        cs = slice(hh * HEAD_W, (hh + 1) * HEAD_W)
        q_ref[:, cs] = _rope(p[:, hh * HEAD_W:(hh + 1) * HEAD_W], cos_ref[...], sin_ref[...])
        k_ref[:, cs] = _rope(p[:, RET_W + hh * HEAD_W:RET_W + (hh + 1) * HEAD_W],
                             cos_ref[...], sin_ref[...]) * (HEAD_W ** -0.5)
        u = _gelu_tanh(p[:, 4 * RET_W + hh * HEAD_W:4 * RET_W + (hh + 1) * HEAD_W])
        vv = _gelu_tanh(p[:, 4 * RET_W + GM_W + hh * HEAD_W:4 * RET_W + GM_W + (hh + 1) * HEAD_W])
        vn = _center_unit(vv) * ln_ref[:, cs]
        vn_ref[:, cs] = vn
        ogm_ref[:, cs] = u * (ws0_ref[:, cs] * vn + bs0_ref[:, cs])
    v_ref[...] = p[:, 2 * RET_W:3 * RET_W]
    sg_ref[...] = _silu(p[:, 3 * RET_W:4 * RET_W])


def _sample_ret_body(q_ref, k_ref, v_ref, s_ref, gam_ref, *rest, sb, layer, first):
    o_ref, snew_ref = rest[-2:]
    if first:
        for other in range(snew_ref.shape[0]):
            if other != layer:
                snew_ref[other] = jnp.zeros(snew_ref.shape[1:], F32)
        snew_ref = snew_ref.at[layer]
    for hh in range(N_HEADS):
        cs = slice(hh * HEAD_W, (hh + 1) * HEAD_W)
        gam = gam_ref[:, cs]
        v = v_ref[:, cs]
        k = k_ref[:, cs]
        qb = q_ref[:, cs].astype(BF16)
        kt = jnp.concatenate([k, jnp.zeros((HEAD_W - sb, HEAD_W), F32)], axis=0).T
        for s in range(sb):
            state = s_ref[s, hh]
            snew_ref[s, hh] = gam * state + kt[:, s:s + 1] * v[s:s + 1, :]
            o_ref[s:s + 1, cs] = gam * _dot(qb, state.astype(BF16))[s:s + 1, :]
        qk = jnp.sum(q_ref[:, cs] * k, axis=-1, keepdims=True)
        o_ref[:, cs] = o_ref[:, cs] + qk * v


def _sample_out_body(x_ref, mod_ref, o_ref, sg_ref, ogm_ref, gn_ref, wout_ref, gffn_ref, wup_ref,
                     cbuf_ref, convw_ref, convb_ref, wdown_ref, gfin_ref, *rest, final_norm, first):
    y_ref, cnew_ref = rest[-2:]
    if first:
        @pl.when(pl.program_id(0) > 0)
        def _():
            cnew_ref[...] = jnp.zeros_like(cnew_ref)

        pl.when(pl.program_id(0) == 0)(functools.partial(
            _sample_out_layer, x_ref, mod_ref, o_ref, sg_ref, ogm_ref, gn_ref, wout_ref, gffn_ref, wup_ref,
            cbuf_ref, convw_ref, convb_ref, wdown_ref, gfin_ref, y_ref, cnew_ref, final_norm))
    else:
        _sample_out_layer(x_ref, mod_ref, o_ref, sg_ref, ogm_ref, gn_ref, wout_ref, gffn_ref, wup_ref,
                          cbuf_ref, convw_ref, convb_ref, wdown_ref, gfin_ref, y_ref, cnew_ref, final_norm)


def _sample_out_layer(x_ref, mod_ref, o_ref, sg_ref, ogm_ref, gn_ref, wout_ref, gffn_ref, wup_ref,
                      cbuf_ref, convw_ref, convb_ref, wdown_ref, gfin_ref, y_ref, cnew_ref, final_norm):
    gt1 = mod_ref[:, 2 * D_MODEL:3 * D_MODEL]
    sh2, sc2 = mod_ref[:, 3 * D_MODEL:4 * D_MODEL], mod_ref[:, 4 * D_MODEL:5 * D_MODEL]
    gt2 = mod_ref[:, 5 * D_MODEL:6 * D_MODEL]
    parts = []
    for hh in range(N_HEADS):
        cs = slice(hh * HEAD_W, (hh + 1) * HEAD_W)
        on = _center_unit(o_ref[:, cs]) * gn_ref[:, cs]
        parts.append((sg_ref[:, cs] * on).astype(BF16))
    parts.append(ogm_ref[...].astype(BF16))
    mix = _dot(jnp.concatenate(parts, axis=1), wout_ref[...])
    x1 = x_ref[...] + gt1 * mix
    h = _rms_unit(x1) * gffn_ref[...]
    h = (h * (1.0 + sc2) + sh2).astype(BF16)
    a = _dot(h, wup_ref[...])
    gw = GATE_GROUP_W
    f_parts = []
    for j in range(D_FF // gw):
        halves = []
        for half in range(2):
            c0 = half * D_FF + j * gw
            a_blk = a[:, (2 * j + half) * gw:(2 * j + half + 1) * gw]
            buf0, buf1 = cbuf_ref[:, 0, c0:c0 + gw], cbuf_ref[:, 1, c0:c0 + gw]
            halves.append(convb_ref[:, c0:c0 + gw] + buf0 * convw_ref[0:1, c0:c0 + gw]
                          + buf1 * convw_ref[1:2, c0:c0 + gw] + a_blk * convw_ref[2:3, c0:c0 + gw])
            cnew_ref[:, 0, c0:c0 + gw] = buf1
            cnew_ref[:, 1, c0:c0 + gw] = a_blk
        f_parts.append((_silu(halves[0]) * halves[1]).astype(BF16))
    f = jnp.concatenate(f_parts, axis=1)
    x2 = x1 + gt2 * _dot(f, wdown_ref[...])
    if final_norm:
        x2 = _rms_unit(x2) * gfin_ref[...]
    y_ref[...] = x2


def _whole(shape, pipeline_mode=None):
    zeros = (0,) * len(shape)
    return pl.BlockSpec(shape, lambda *_: zeros, pipeline_mode=pipeline_mode)


def _layer_slab(shape, layer, pipeline_mode=None):
    zeros = (0,) * (len(shape) - 1)
    return pl.BlockSpec((None,) + tuple(shape[1:]), lambda *_: (layer,) + zeros, pipeline_mode=pipeline_mode)


def _sample_layer(layer, x, mod, state_ret, state_conv, ret_acc, conv_acc, lw, tabs, g_final, final_norm):
    n = x.shape[0]
    params = pltpu.CompilerParams(dimension_semantics=("arbitrary",), vmem_limit_bytes=V7X_VMEM_LIMIT_BYTES)
    act = jax.ShapeDtypeStruct((n, RET_W), F32)
    in_args = (x, mod, lw["g_mix"], lw["w_in"], tabs["cos_s"], tabs["sin_s"], lw["ln"], lw["ws0"], lw["bs0"])
    in_specs = [_whole(a.shape) for a in in_args]
    in_specs[1] = pl.BlockSpec((None, n, N_MOD * D_MODEL), lambda i: (layer, 0, 0))
    for i in (2, 3, 6, 7, 8):
        in_specs[i] = _layer_slab(in_args[i].shape, layer)
    q, k, v, sg, ogm, vn = pl.pallas_call(
        _sample_in_body,
        grid=(1,),
        in_specs=in_specs,
        out_specs=[_whole(act.shape)] * 6,
        out_shape=[act] * 6,
        compiler_params=params,
        name="sample_in",
    )(*in_args)

    sb = SAMPLE_SEQ_BLOCK
    row_spec = pl.BlockSpec((sb, RET_W), lambda i: (i, 0))
    state_spec = pl.BlockSpec((None, sb, N_HEADS, HEAD_W, HEAD_W), lambda i: (layer, i, 0, 0, 0))
    carried = [] if ret_acc is None else [ret_acc]
    depth = state_ret.shape[0]
    first = ret_acc is None
    all_layers_spec = pl.BlockSpec((depth, sb, N_HEADS, HEAD_W, HEAD_W), lambda i: (0, i, 0, 0, 0))
    o, ret_acc = pl.pallas_call(
        functools.partial(_sample_ret_body, sb=sb, layer=layer, first=first),
        grid=(n // sb,),
        in_specs=[row_spec, row_spec, row_spec, state_spec, _whole((1, RET_W))]
        + [pl.BlockSpec(memory_space=pl.ANY)] * len(carried),
        out_specs=[row_spec, all_layers_spec if first else state_spec],
        out_shape=[act, jax.ShapeDtypeStruct(state_ret.shape, F32)],
        input_output_aliases={5: 1} if carried else {},
        compiler_params=params,
        name="sample_ret",
    )(q, k, v, state_ret, tabs["gamma_s"], *carried)

    out_args = (x, mod, o, sg, ogm, lw["gn"], lw["w_out"], lw["g_ffn"], lw["w_up"], state_conv,
                lw["conv_w"], lw["conv_b"], lw["w_down"], g_final)
    once = pl.Buffered(1)
    in_specs = [_whole(a.shape, once) for a in out_args]
    in_specs[1] = pl.BlockSpec((None, n, N_MOD * D_MODEL), lambda i: (layer, 0, 0), pipeline_mode=once)
    for i in (5, 6, 7, 8, 9, 10, 11, 12):
        in_specs[i] = _layer_slab(out_args[i].shape, layer, once)
    first = conv_acc is None
    carried = [] if first else [conv_acc]
    slab_spec = pl.BlockSpec((None,) + tuple(state_conv.shape[1:]),
                             lambda i: (lax.rem(layer + i, depth), 0, 0, 0), pipeline_mode=once)

    y, conv_acc = pl.pallas_call(
        functools.partial(_sample_out_body, final_norm=final_norm, first=first),
        grid=(depth if first else 1,),
        in_specs=in_specs + [pl.BlockSpec(memory_space=pl.ANY)] * len(carried),
        out_specs=[_whole((n, D_MODEL)), slab_spec],
        out_shape=[jax.ShapeDtypeStruct((n, D_MODEL), F32), jax.ShapeDtypeStruct(state_conv.shape, F32)],
        input_output_aliases={len(out_args): 1} if carried else {},
        compiler_params=params,
        name="sample_out",
    )(*out_args, *carried)
    return y, ret_acc, conv_acc, vn


def _rope_tables(pos):
    half = HEAD_W // 2
    freqs = np.exp(-math.log(ROPE_BASE) * np.arange(half, dtype=np.float64) / half)
    ang = np.asarray(pos, dtype=np.float64)[:, None] * freqs[None, :]
    cos, sin = np.cos(ang), np.sin(ang)
    return (np.concatenate([cos, cos], axis=-1).astype(np.float32),
            np.concatenate([-sin, sin], axis=-1).astype(np.float32))


def _decay_tables(chunk):
    lg = np.log1p(-np.exp2(-5.0 - np.arange(N_HEADS, dtype=np.float64)))
    i = np.arange(chunk, dtype=np.float64)
    diff = i[:, None] - i[None, :]
    dmask = np.where(diff[None] >= 0.0, np.exp(np.maximum(diff, 0.0)[None] * lg[:, None, None]), 0.0)
    q_dec = np.exp((i[:, None] + 1.0) * lg[None, :])
    k_dec = np.exp((chunk - 1.0 - i)[:, None] * lg[None, :])
    chunk_dec = np.exp(chunk * lg)
    return tuple(a.astype(np.float32) for a in (dmask, q_dec, k_dec, chunk_dec))


def _per_head_lanes(a):
    return a.repeat(HEAD_W, axis=-1)


def kernel(x_prompt, x_sample, state_ret, state_conv, c_prompt, c_sample, w_ada, b_ada, g_mix, w_in,
           ret_gn_gain, gmlp_ln_gain, w_s, b_s, w_out, g_ffn, w_up, conv_w, conv_b, w_down, g_final):
    depth = w_in.shape[0]
    batch, seq, _ = x_prompt.shape
    n_dec, dec_seq, _ = x_sample.shape
    assert dec_seq == 1 and seq % PROMPT_TILE == 0 and PROMPT_TILE % CHUNK == 0

    cos_p, sin_p = _rope_tables(np.arange(seq))
    cos_s, sin_s = _rope_tables(PAST_LEN + np.arange(dec_seq))
    dmask, q_dec, k_dec, chunk_dec = _decay_tables(CHUNK)
    _, _, _, gamma_s = _decay_tables(dec_seq)
    tabs = {
        "cos": cos_p, "sin": sin_p, "cos_s": cos_s, "sin_s": sin_s,
        "dmask": dmask, "qdec": _per_head_lanes(q_dec), "kdec": _per_head_lanes(k_dec),
        "cdec": np.broadcast_to(chunk_dec[:, None, None], (N_HEADS, 1, HEAD_W)),
        "gamma_s": _per_head_lanes(gamma_s[None, :]),
    }
    tabs = {name: jnp.asarray(a) for name, a in tabs.items()}

    mod = _adaln(jnp.concatenate([c_sample, c_prompt], axis=0), w_ada, b_ada)
    mod_p = mod[:, n_dec:].reshape(depth, batch, N_MOD, D_MODEL)
    tril = np.tril(np.ones((CHUNK, CHUNK), dtype=bool))
    lw = {
        "g_mix": g_mix.reshape(depth, 1, D_MODEL), "g_ffn": g_ffn.reshape(depth, 1, D_MODEL),
        "w_in": _to_bf16(w_in), "w_out": _to_bf16(w_out),
        "w_up": _to_bf16(w_up, group_gate_columns=True), "w_down": _to_bf16(w_down),
        "gn": ret_gn_gain.reshape(depth, 1, RET_W), "ln": gmlp_ln_gain.reshape(depth, 1, GM_W),
        "ws_tril": jnp.where(tril, w_s, 0.0).astype(BF16),
        "bs_full": _per_head_lanes(b_s.transpose(0, 2, 1)),
        "ws0": _per_head_lanes(w_s[:, None, :, 0, 0]), "bs0": _per_head_lanes(b_s[:, None, :, 0]),
        "conv_w": conv_w, "conv_b": conv_b.reshape(depth, 1, 2 * D_FF),
    }
    g_fin = g_final.reshape(1, D_MODEL)

    xp, xs = x_prompt, x_sample.reshape(n_dec, D_MODEL)
    ret_p, conv_p, v_s = [], [], []
    ret_s = conv_s = None
    for l in range(depth):
        last = l == depth - 1
        xp, sp, bp = _prompt_layer(l, xp, mod_p, lw, tabs, g_fin, last)
        xs, ret_s, conv_s, vs = _sample_layer(l, xs, mod, state_ret, state_conv, ret_s, conv_s, lw, tabs,
                                              g_fin, last)
        ret_p.append(sp); conv_p.append(bp)
        v_s.append(vs.reshape(n_dec, dec_seq, GM_W))
    return (xp, xs.reshape(n_dec, dec_seq, D_MODEL), jnp.stack(ret_p), jnp.stack(conv_p),
            ret_s, conv_s, jnp.stack(v_s))
```

```python
import functools
import math

import jax
import jax.numpy as jnp
import numpy as np
from jax import lax
from jax.experimental import pallas as pl
from jax.experimental.pallas import tpu as pltpu

D_MODEL = 1024
N_HEADS = 4
HEAD_W = 128
RET_W = N_HEADS * HEAD_W
GM_W = N_HEADS * HEAD_W
IN_W = 4 * RET_W + 2 * GM_W
D_FF = 2048
CONV_W = 3
CHUNK = 128
ROPE_BASE = 10000.0
PAST_LEN = 16384
EPS = 1e-6
N_MOD = 6

V7X_VMEM_LIMIT_BYTES = 60 * 1024 * 1024
PROMPT_TILE = 512
ROW_BLOCK = 64
CONV_PAD = 8
GATE_GROUP_W = 256
CAST_COLS = 1024
DOWN_SPLIT_GROUPS = 4
SAMPLE_SEQ_BLOCK = 16

F32 = jnp.float32
BF16 = jnp.bfloat16
NT_DIMS = (((1,), (1,)), ((), ()))
TN_DIMS = (((0,), (0,)), ((), ()))


def _silu(x):
    return x * (1.0 / (1.0 + jnp.exp(-x)))


def _gelu_tanh(x):
    c = math.sqrt(2.0 / math.pi)
    return 0.5 * x * (1.0 + jnp.tanh(c * (x + 0.044715 * (x * x * x))))


def _rms_unit(x):
    return x * lax.rsqrt(jnp.mean(x * x, axis=-1, keepdims=True) + EPS)


def _center_unit(x):
    xc = x - jnp.mean(x, axis=-1, keepdims=True)
    return xc * lax.rsqrt(jnp.mean(xc * xc, axis=-1, keepdims=True) + EPS)


def _rope(x, cos_full, sin_signed):
    return x * cos_full + pltpu.roll(x, HEAD_W // 2, 1) * sin_signed


def _dot(a, b):
    return jnp.dot(a, b, preferred_element_type=F32)


NORM_BUNDLES = 120
RESID_BUNDLES = 60
CONV_BUNDLES = 170
HEAD_BUNDLES = 160
GROUP_BUNDLES = 130


def _two_unit_order(pieces, orders):
    unit_free = {unit: 0 for unit in orders}
    head = {unit: 0 for unit in orders}
    finish = {}
    order = []
    while any(head[unit] < len(names) for unit, names in orders.items()):
        ready = []
        for unit, names in orders.items():
            if head[unit] < len(names):
                name = names[head[unit]]
                deps = pieces[name][2]
                if all(d in finish for d in deps):
                    ready.append((max([unit_free[unit]] + [finish[d] for d in deps]), unit, name))
        assert ready, "piece lists contradict the dependencies"
        start, unit, name = min(ready)
        finish[name] = unit_free[unit] = start + pieces[name][1]
        head[unit] += 1
        order.append(name)
    return order


def _cast_body(*refs):
    o_ref = refs[-1]
    width = o_ref.shape[-1] // (len(refs) - 1)
    for i, w_ref in enumerate(refs[:-1]):
        o_ref[:, i * width:(i + 1) * width] = w_ref[...].astype(BF16)


def _to_bf16(w, group_gate_columns=False):
    depth, k, n = w.shape
    if group_gate_columns:
        bn = 2 * GATE_GROUP_W
        n_grp = n // bn
        srcs = [pl.BlockSpec((None, k, bn // 2), lambda l, c, half=half: (l, 0, half * n_grp + c)) for half in range(2)]
    else:
        bn = CAST_COLS
        srcs = [pl.BlockSpec((None, k, bn), lambda l, c: (l, 0, c))]
    return pl.pallas_call(
        _cast_body,
        grid=(depth, n // bn),
        in_specs=srcs,
        out_specs=pl.BlockSpec((None, k, bn), lambda l, c: (l, 0, c)),
        out_shape=jax.ShapeDtypeStruct(w.shape, BF16),
        compiler_params=pltpu.CompilerParams(
            dimension_semantics=("arbitrary", "arbitrary"), vmem_limit_bytes=V7X_VMEM_LIMIT_BYTES),
        name="to_bf16",
    )(*([w] * len(srcs)))


def _adaln_body(c_ref, w_ref, b_ref, o_ref):
    c = _silu(c_ref[...]).astype(BF16)
    o_ref[...] = _dot(c, w_ref[...].astype(BF16)) + b_ref[...]


def _adaln(c_all, w_ada, b_ada):
    depth = w_ada.shape[0]
    rows = c_all.shape[0]
    n_out = w_ada.shape[2]
    bn = 3 * D_MODEL
    return pl.pallas_call(
        _adaln_body,
        grid=(depth, n_out // bn),
        in_specs=[
            pl.BlockSpec((rows, D_MODEL), lambda l, j: (0, 0)),
            pl.BlockSpec((None, D_MODEL, bn), lambda l, j: (l, 0, j)),
            pl.BlockSpec((None, 1, bn), lambda l, j: (l, 0, j)),
        ],
        out_specs=pl.BlockSpec((None, rows, bn), lambda l, j: (l, 0, j)),
        out_shape=jax.ShapeDtypeStruct((depth, rows, n_out), F32),
        compiler_params=pltpu.CompilerParams(
            dimension_semantics=("arbitrary", "arbitrary"), vmem_limit_bytes=V7X_VMEM_LIMIT_BYTES),
        name="adaln",
    )(c_all, w_ada, b_ada.reshape(depth, 1, n_out))


def _prompt_layer_body(x_ref, modf_ref, modb_ref, gmix_ref, gffn_ref, win_ref, wout_ref, wup_ref, wdown_ref,
                       cos_ref, sin_ref, dmask_ref, qdec_ref, kdec_ref, cdec_ref, gn_ref, ln_ref,
                       ws_ref, bs_ref, convw_ref, convb_ref, gfin_ref,
                       y_ref, ret_ref, conv_ref,
                       h_ref, p_ref, mix_ref, mo_ref, x1_ref, h2_ref, f_ref, mo2_ref, *a_refs,
                       tm, tiles_per_seq, n_tiles, final_norm):
    s = pl.program_id(0)
    front_on = s < n_tiles
    t_front = lax.rem(jnp.minimum(s, n_tiles - 1), tiles_per_seq)
    t_back = lax.rem(jnp.maximum(s - 1, 0), tiles_per_seq)

    @pl.when(s == 0)
    def _():
        x1_ref[...] = jnp.zeros_like(x1_ref)
        h2_ref[...] = jnp.zeros_like(h2_ref)

    @pl.when(jnp.logical_and(front_on, t_front == 0))
    def _():
        ret_ref[...] = jnp.zeros_like(ret_ref)

    @pl.when(t_back == 0)
    def _():
        conv_ref[...] = jnp.zeros_like(conv_ref)

    sh1, sc1, gt1 = modf_ref[0:1, :], modf_ref[1:2, :], modf_ref[2:3, :]
    sh2, sc2 = modf_ref[3:4, :], modf_ref[4:5, :]
    gt2 = modb_ref[5:6, :]
    row_blocks = [slice(r * ROW_BLOCK, (r + 1) * ROW_BLOCK) for r in range(tm // ROW_BLOCK)]

    cb = GATE_GROUP_W
    tail = slice(CONV_PAD - (CONV_W - 1), CONV_PAD)

    pieces = {}
    n_rb = len(row_blocks)
    n_grp = D_FF // cb
    n_chunks = tm // CHUNK

    def matmul_bundles(k, n):
        return (tm // 16) * (k // 256) * (n // 256) * 8 // 2

    def up_group(j):
        for half in range(2):
            c0 = half * D_FF + j * cb
            a_refs[j][tail, half * cb:(half + 1) * cb] = conv_ref[:, c0:c0 + cb]
        a_refs[j][CONV_PAD:CONV_PAD + tm, :] = _dot(h2_ref[...], wup_ref[:, 2 * j * cb:2 * (j + 1) * cb])
        for half in range(2):
            c0 = half * D_FF + j * cb
            conv_ref[:, c0:c0 + cb] = a_refs[j][CONV_PAD + tm - (CONV_W - 1):CONV_PAD + tm,
                                                half * cb:(half + 1) * cb]

    def conv_rows(j, r):
        halves = []
        for half in range(2):
            c0 = half * D_FF + j * cb
            src = slice(half * cb, (half + 1) * cb)
            window = a_refs[j][r * ROW_BLOCK:CONV_PAD + (r + 1) * ROW_BLOCK, src]
            acc = convb_ref[:, c0:c0 + cb]
            for tap in range(CONV_W):
                back = CONV_W - 1 - tap
                rows_back = pltpu.roll(window, back, 0) if back else window
                acc = acc + rows_back[CONV_PAD:, :] * convw_ref[tap:tap + 1, c0:c0 + cb]
            halves.append(acc)
        f_ref[row_blocks[r], j * cb:(j + 1) * cb] = (_silu(halves[0]) * halves[1]).astype(BF16)

    def down_proj(g0, g1):
        part = _dot(f_ref[:, g0 * cb:g1 * cb], wdown_ref[g0 * cb:g1 * cb, :])
        mo2_ref[...] = part if g0 == 0 else mo2_ref[...] + part

    def resid_out(r):
        rows = row_blocks[r]
        x2 = y_ref[rows, :] + gt2 * mo2_ref[rows, :]
        if final_norm:
            x2 = _rms_unit(x2) * gfin_ref[...]
        y_ref[rows, :] = x2

    for j in range(n_grp):
        pieces[f"up{j}"] = ("mxu", matmul_bundles(D_MODEL, 2 * cb), (), functools.partial(up_group, j))
        for r in range(n_rb):
            pieces[f"conv{j}.{r}"] = ("valu", CONV_BUNDLES, (f"up{j}",), functools.partial(conv_rows, j, r))
    all_conv = tuple(f"conv{j}.{r}" for j in range(n_grp) for r in range(n_rb))
    split = DOWN_SPLIT_GROUPS
    pieces["down_a"] = ("mxu", matmul_bundles(split * cb, D_MODEL), all_conv[:split * n_rb],
                        functools.partial(down_proj, 0, split))
    pieces["down"] = ("mxu", matmul_bundles((n_grp - split) * cb, D_MODEL), all_conv + ("down_a",),
                      functools.partial(down_proj, split, n_grp))
    for r in range(n_rb):
        pieces[f"y{r}"] = ("valu", RESID_BUNDLES, ("down",), functools.partial(resid_out, r))

    def norm_rows(r):
        rows = row_blocks[r]
        h = _rms_unit(x_ref[rows, :]) * gmix_ref[...]
        h_ref[rows, :] = (h * (1.0 + sc1) + sh1).astype(BF16)

    def in_cols(c0, c1):
        p_ref[:, c0:c1] = _dot(h_ref[...], win_ref[:, c0:c1])

    all_norm = tuple(f"norm{r}" for r in range(n_rb))
    for r in range(n_rb):
        pieces[f"norm{r}"] = ("valu", NORM_BUNDLES, (), functools.partial(norm_rows, r))
    pieces["in_ret"] = ("mxu", matmul_bundles(D_MODEL, 4 * RET_W), all_norm, functools.partial(in_cols, 0, 4 * RET_W))
    pieces["in_gm"] = ("mxu", matmul_bundles(D_MODEL, 2 * GM_W), all_norm,
                       functools.partial(in_cols, 4 * RET_W, IN_W))

    def mix_heads(c):
        rows = slice(c * CHUNK, (c + 1) * CHUNK)
        heads = [slice(hh * HEAD_W, (hh + 1) * HEAD_W) for hh in range(N_HEADS)]
        cos_full, sin_signed = cos_ref[rows, :], sin_ref[rows, :]
        q = [_rope(p_ref[rows, hh * HEAD_W:(hh + 1) * HEAD_W], cos_full, sin_signed) for hh in range(N_HEADS)]
        k = [_rope(p_ref[rows, RET_W + hh * HEAD_W:RET_W + (hh + 1) * HEAD_W], cos_full, sin_signed)
             * (HEAD_W ** -0.5) for hh in range(N_HEADS)]
        vb = [p_ref[rows, 2 * RET_W + hh * HEAD_W:2 * RET_W + (hh + 1) * HEAD_W].astype(BF16)
              for hh in range(N_HEADS)]
        scores = [lax.dot_general(q[hh].astype(BF16), k[hh].astype(BF16), NT_DIMS, preferred_element_type=F32)
                  for hh in range(N_HEADS)]
        update = [lax.dot_general((k[hh] * kdec_ref[:, heads[hh]]).astype(BF16), vb[hh], TN_DIMS,
                                  preferred_element_type=F32) for hh in range(N_HEADS)]
        state = [ret_ref[hh] for hh in range(N_HEADS)]
        o = [_dot(jnp.concatenate([(scores[hh] * dmask_ref[hh]).astype(BF16),
                                   (q[hh] * qdec_ref[:, heads[hh]]).astype(BF16)], axis=1),
                  jnp.concatenate([vb[hh], state[hh].astype(BF16)], axis=0)) for hh in range(N_HEADS)]
        for hh in range(N_HEADS):
            ret_ref[hh] = jnp.where(front_on, cdec_ref[hh] * state[hh] + update[hh], state[hh])
        for hh in range(N_HEADS):
            on = _center_unit(o[hh]) * gn_ref[:, heads[hh]]
            g = p_ref[rows, 3 * RET_W + hh * HEAD_W:3 * RET_W + (hh + 1) * HEAD_W]
            mix_ref[rows, heads[hh]] = (_silu(g) * on).astype(BF16)

    def mix_groups(c):
        rows = slice(c * CHUNK, (c + 1) * CHUNK)
        groups = [slice(gg * HEAD_W, (gg + 1) * HEAD_W) for gg in range(N_HEADS)]
        vn = [(_center_unit(_gelu_tanh(p_ref[rows, 4 * RET_W + GM_W + gg * HEAD_W:4 * RET_W + GM_W + (gg + 1) * HEAD_W]))
               * ln_ref[:, groups[gg]]).astype(BF16) for gg in range(N_HEADS)]
        gate = [_dot(ws_ref[gg], vn[gg]) for gg in range(N_HEADS)]
        for gg in range(N_HEADS):
            u = _gelu_tanh(p_ref[rows, 4 * RET_W + gg * HEAD_W:4 * RET_W + (gg + 1) * HEAD_W])
            mix_ref[rows, RET_W + gg * HEAD_W:RET_W + (gg + 1) * HEAD_W] = (
                u * (gate[gg] + bs_ref[:, groups[gg]])).astype(BF16)

    def out_proj():
        mo_ref[...] = _dot(mix_ref[...], wout_ref[...])

    def resid_mid(r):
        rows = row_blocks[r]
        x1 = x_ref[rows, :] + gt1 * mo_ref[rows, :]
        x1_ref[rows, :] = x1
        h = _rms_unit(x1) * gffn_ref[...]
        h2_ref[rows, :] = (h * (1.0 + sc2) + sh2).astype(BF16)

    all_mix = []
    for c in range(n_chunks):
        pieces[f"heads{c}"] = ("valu", N_HEADS * HEAD_BUNDLES, ("in_ret",), functools.partial(mix_heads, c))
        pieces[f"groups{c}"] = ("valu", N_HEADS * GROUP_BUNDLES, ("in_gm",), functools.partial(mix_groups, c))
        all_mix += [f"heads{c}", f"groups{c}"]
    pieces["out"] = ("mxu", matmul_bundles(RET_W + GM_W, D_MODEL), tuple(all_mix), out_proj)
    all_up = tuple(f"up{j}" for j in range(n_grp))
    for r in range(n_rb):
        pieces[f"x1{r}"] = ("valu", RESID_BUNDLES + NORM_BUNDLES, ("out",) + all_up, functools.partial(resid_mid, r))

    orders = {
        "mxu": list(all_up) + ["in_gm", "in_ret", "down_a", "down", "out"],
        "valu": list(all_norm) + list(all_conv) + sorted(all_mix)
        + [f"y{r}" for r in range(n_rb)] + [f"x1{r}" for r in range(n_rb)],
    }
    y_ref[...] = x1_ref[...]
    for name in _two_unit_order(pieces, orders):
        pieces[name][3]()


def _const_spec(shape):
    zeros = (0,) * len(shape)
    return pl.BlockSpec(shape, lambda s: zeros, pipeline_mode=pl.Buffered(1))


def _prompt_layer(layer, x, mod, lw, tabs, g_final, final_norm):
    batch, seq, _ = x.shape

    def layer_spec(name):
        shape = lw[name].shape
        zeros = (0,) * (len(shape) - 1)
        return pl.BlockSpec((None,) + tuple(shape[1:]), lambda s: (layer,) + zeros, pipeline_mode=pl.Buffered(1))

    tm = PROMPT_TILE
    tiles_per_seq = seq // tm
    n_tiles = batch * tiles_per_seq
    body = functools.partial(_prompt_layer_body, tm=tm, tiles_per_seq=tiles_per_seq, n_tiles=n_tiles,
                             final_norm=final_norm)

    def front(s):
        i = jnp.minimum(s, n_tiles - 1)
        return i // tiles_per_seq, lax.rem(i, tiles_per_seq)

    def back(s):
        i = jnp.maximum(s - 1, 0)
        return i // tiles_per_seq, lax.rem(i, tiles_per_seq)

    in_specs = [
        pl.BlockSpec((None, tm, D_MODEL), lambda s: (*front(s), 0)),
        pl.BlockSpec((None, None, N_MOD, D_MODEL), lambda s: (layer, front(s)[0], 0, 0)),
        pl.BlockSpec((None, None, N_MOD, D_MODEL), lambda s: (layer, back(s)[0], 0, 0)),
        layer_spec("g_mix"), layer_spec("g_ffn"),
        layer_spec("w_in"), layer_spec("w_out"), layer_spec("w_up"), layer_spec("w_down"),
        pl.BlockSpec((tm, HEAD_W), lambda s: (front(s)[1], 0)),
        pl.BlockSpec((tm, HEAD_W), lambda s: (front(s)[1], 0)),
        _const_spec((N_HEADS, CHUNK, CHUNK)), _const_spec((CHUNK, RET_W)), _const_spec((CHUNK, RET_W)),
        _const_spec((N_HEADS, 1, HEAD_W)), layer_spec("gn"), layer_spec("ln"),
        layer_spec("ws_tril"), layer_spec("bs_full"),
        layer_spec("conv_w"), layer_spec("conv_b"), _const_spec((1, D_MODEL)),
    ]
    out_specs = [
        pl.BlockSpec((None, tm, D_MODEL), lambda s: (*back(s), 0)),
        pl.BlockSpec((None, N_HEADS, HEAD_W, HEAD_W), lambda s: (front(s)[0], 0, 0, 0)),
        pl.BlockSpec((None, CONV_W - 1, 2 * D_FF), lambda s: (back(s)[0], 0, 0)),
    ]
    out_shape = [
        jax.ShapeDtypeStruct((batch, seq, D_MODEL), F32),
        jax.ShapeDtypeStruct((batch, N_HEADS, HEAD_W, HEAD_W), F32),
        jax.ShapeDtypeStruct((batch, CONV_W - 1, 2 * D_FF), F32),
    ]
    scratch = [
        pltpu.VMEM((tm, D_MODEL), BF16),
        pltpu.VMEM((tm, IN_W), F32),
        pltpu.VMEM((tm, RET_W + GM_W), BF16),
        pltpu.VMEM((tm, D_MODEL), F32),
        pltpu.VMEM((tm, D_MODEL), F32),
        pltpu.VMEM((tm, D_MODEL), BF16),
        pltpu.VMEM((tm, D_FF), BF16),
        pltpu.VMEM((tm, D_MODEL), F32),
    ] + [pltpu.VMEM((CONV_PAD + tm, 2 * GATE_GROUP_W), F32)] * (D_FF // GATE_GROUP_W)
    return pl.pallas_call(
        body,
        grid=(n_tiles + 1,),
        in_specs=in_specs,
        out_specs=out_specs,
        out_shape=out_shape,
        scratch_shapes=scratch,
        compiler_params=pltpu.CompilerParams(
            dimension_semantics=("arbitrary",), vmem_limit_bytes=V7X_VMEM_LIMIT_BYTES),
        name="prompt_layer",
    )(x, mod, mod, lw["g_mix"], lw["g_ffn"], lw["w_in"], lw["w_out"], lw["w_up"], lw["w_down"],
      tabs["cos"], tabs["sin"], tabs["dmask"], tabs["qdec"], tabs["kdec"], tabs["cdec"],
      lw["gn"], lw["ln"], lw["ws_tril"], lw["bs_full"], lw["conv_w"], lw["conv_b"], g_final)


def _sample_in_body(x_ref, mod_ref, gmix_ref, win_ref, cos_ref, sin_ref, ln_ref, ws0_ref, bs0_ref,
                    q_ref, k_ref, v_ref, sg_ref, ogm_ref, vn_ref):
    sh1, sc1 = mod_ref[:, 0:D_MODEL], mod_ref[:, D_MODEL:2 * D_MODEL]
    h = _rms_unit(x_ref[...]) * gmix_ref[...]
    h = (h * (1.0 + sc1) + sh1).astype(BF16)
    p = _dot(h, win_ref[...])
    for hh in range(N_HEADS):
        cs = slice(hh * HEAD_W, (hh + 1) * HEAD_W)
        q_ref[:, cs] = _rope(p[:, hh * HEAD_W:(hh + 1) * HEAD_W], cos_ref[...], sin_ref[...])
        k_ref[:, cs] = _rope(p[:, RET_W + hh * HEAD_W:RET_W + (hh + 1) * HEAD_W],
                             cos_ref[...], sin_ref[...]) * (HEAD_W ** -0.5)
        u = _gelu_tanh(p[:, 4 * RET_W + hh * HEAD_W:4 * RET_W + (hh + 1) * HEAD_W])
        vv = _gelu_tanh(p[:, 4 * RET_W + GM_W + hh * HEAD_W:4 * RET_W + GM_W + (hh + 1) * HEAD_W])
        vn = _center_unit(vv) * ln_ref[:, cs]
        vn_ref[:, cs] = vn
        ogm_ref[:, cs] = u * (ws0_ref[:, cs] * vn + bs0_ref[:, cs])
    v_ref[...] = p[:, 2 * RET_W:3 * RET_W]
    sg_ref[...] = _silu(p[:, 3 * RET_W:4 * RET_W])


def _sample_ret_body(q_ref, k_ref, v_ref, s_ref, gam_ref, *rest, sb, layer, first):
    o_ref, snew_ref = rest[-2:]
    if first:
        for other in range(snew_ref.shape[0]):
            if other != layer:
                snew_ref[other] = jnp.zeros(snew_ref.shape[1:], F32)
        snew_ref = snew_ref.at[layer]
    for hh in range(N_HEADS):
        cs = slice(hh * HEAD_W, (hh + 1) * HEAD_W)
        gam = gam_ref[:, cs]
        v = v_ref[:, cs]
        k = k_ref[:, cs]
        qb = q_ref[:, cs].astype(BF16)
        kt = jnp.concatenate([k, jnp.zeros((HEAD_W - sb, HEAD_W), F32)], axis=0).T
        for s in range(sb):
            state = s_ref[s, hh]
            snew_ref[s, hh] = gam * state + kt[:, s:s + 1] * v[s:s + 1, :]
            o_ref[s:s + 1, cs] = gam * _dot(qb, state.astype(BF16))[s:s + 1, :]
        qk = jnp.sum(q_ref[:, cs] * k, axis=-1, keepdims=True)
        o_ref[:, cs] = o_ref[:, cs] + qk * v


def _sample_out_body(x_ref, mod_ref, o_ref, sg_ref, ogm_ref, gn_ref, wout_ref, gffn_ref, wup_ref,
                     cbuf_ref, convw_ref, convb_ref, wdown_ref, gfin_ref, *rest, final_norm, first):
    y_ref, cnew_ref = rest[-2:]
    if first:
        @pl.when(pl.program_id(0) > 0)
        def _():
            cnew_ref[...] = jnp.zeros_like(cnew_ref)

        pl.when(pl.program_id(0) == 0)(functools.partial(
            _sample_out_layer, x_ref, mod_ref, o_ref, sg_ref, ogm_ref, gn_ref, wout_ref, gffn_ref, wup_ref,
            cbuf_ref, convw_ref, convb_ref, wdown_ref, gfin_ref, y_ref, cnew_ref, final_norm))
    else:
        _sample_out_layer(x_ref, mod_ref, o_ref, sg_ref, ogm_ref, gn_ref, wout_ref, gffn_ref, wup_ref,
                          cbuf_ref, convw_ref, convb_ref, wdown_ref, gfin_ref, y_ref, cnew_ref, final_norm)


def _sample_out_layer(x_ref, mod_ref, o_ref, sg_ref, ogm_ref, gn_ref, wout_ref, gffn_ref, wup_ref,
                      cbuf_ref, convw_ref, convb_ref, wdown_ref, gfin_ref, y_ref, cnew_ref, final_norm):
    gt1 = mod_ref[:, 2 * D_MODEL:3 * D_MODEL]
    sh2, sc2 = mod_ref[:, 3 * D_MODEL:4 * D_MODEL], mod_ref[:, 4 * D_MODEL:5 * D_MODEL]
    gt2 = mod_ref[:, 5 * D_MODEL:6 * D_MODEL]
    parts = []
    for hh in range(N_HEADS):
        cs = slice(hh * HEAD_W, (hh + 1) * HEAD_W)
        on = _center_unit(o_ref[:, cs]) * gn_ref[:, cs]
        parts.append((sg_ref[:, cs] * on).astype(BF16))
    parts.append(ogm_ref[...].astype(BF16))
    mix = _dot(jnp.concatenate(parts, axis=1), wout_ref[...])
    x1 = x_ref[...] + gt1 * mix
    h = _rms_unit(x1) * gffn_ref[...]
    h = (h * (1.0 + sc2) + sh2).astype(BF16)
    a = _dot(h, wup_ref[...])
    gw = GATE_GROUP_W
    f_parts = []
    for j in range(D_FF // gw):
        halves = []
        for half in range(2):
            c0 = half * D_FF + j * gw
            a_blk = a[:, (2 * j + half) * gw:(2 * j + half + 1) * gw]
            buf0, buf1 = cbuf_ref[:, 0, c0:c0 + gw], cbuf_ref[:, 1, c0:c0 + gw]
            halves.append(convb_ref[:, c0:c0 + gw] + buf0 * convw_ref[0:1, c0:c0 + gw]
                          + buf1 * convw_ref[1:2, c0:c0 + gw] + a_blk * convw_ref[2:3, c0:c0 + gw])
            cnew_ref[:, 0, c0:c0 + gw] = buf1
            cnew_ref[:, 1, c0:c0 + gw] = a_blk
        f_parts.append((_silu(halves[0]) * halves[1]).astype(BF16))
    f = jnp.concatenate(f_parts, axis=1)
    x2 = x1 + gt2 * _dot(f, wdown_ref[...])
    if final_norm:
        x2 = _rms_unit(x2) * gfin_ref[...]
    y_ref[...] = x2


def _whole(shape, pipeline_mode=None):
    zeros = (0,) * len(shape)
    return pl.BlockSpec(shape, lambda *_: zeros, pipeline_mode=pipeline_mode)


def _layer_slab(shape, layer, pipeline_mode=None):
    zeros = (0,) * (len(shape) - 1)
    return pl.BlockSpec((None,) + tuple(shape[1:]), lambda *_: (layer,) + zeros, pipeline_mode=pipeline_mode)


def _sample_layer(layer, x, mod, state_ret, state_conv, ret_acc, conv_acc, lw, tabs, g_final, final_norm):
    n = x.shape[0]
    params = pltpu.CompilerParams(dimension_semantics=("arbitrary",), vmem_limit_bytes=V7X_VMEM_LIMIT_BYTES)
    act = jax.ShapeDtypeStruct((n, RET_W), F32)
    in_args = (x, mod, lw["g_mix"], lw["w_in"], tabs["cos_s"], tabs["sin_s"], lw["ln"], lw["ws0"], lw["bs0"])
    in_specs = [_whole(a.shape) for a in in_args]
    in_specs[1] = pl.BlockSpec((None, n, N_MOD * D_MODEL), lambda i: (layer, 0, 0))
    for i in (2, 3, 6, 7, 8):
        in_specs[i] = _layer_slab(in_args[i].shape, layer)
    q, k, v, sg, ogm, vn = pl.pallas_call(
        _sample_in_body,
        grid=(1,),
        in_specs=in_specs,
        out_specs=[_whole(act.shape)] * 6,
        out_shape=[act] * 6,
        compiler_params=params,
        name="sample_in",
    )(*in_args)

    sb = SAMPLE_SEQ_BLOCK
    row_spec = pl.BlockSpec((sb, RET_W), lambda i: (i, 0))
    state_spec = pl.BlockSpec((None, sb, N_HEADS, HEAD_W, HEAD_W), lambda i: (layer, i, 0, 0, 0))
    carried = [] if ret_acc is None else [ret_acc]
    depth = state_ret.shape[0]
    first = ret_acc is None
    all_layers_spec = pl.BlockSpec((depth, sb, N_HEADS, HEAD_W, HEAD_W), lambda i: (0, i, 0, 0, 0))
    o, ret_acc = pl.pallas_call(
        functools.partial(_sample_ret_body, sb=sb, layer=layer, first=first),
        grid=(n // sb,),
        in_specs=[row_spec, row_spec, row_spec, state_spec, _whole((1, RET_W))]
        + [pl.BlockSpec(memory_space=pl.ANY)] * len(carried),
        out_specs=[row_spec, all_layers_spec if first else state_spec],
        out_shape=[act, jax.ShapeDtypeStruct(state_ret.shape, F32)],
        input_output_aliases={5: 1} if carried else {},
        compiler_params=params,
        name="sample_ret",
    )(q, k, v, state_ret, tabs["gamma_s"], *carried)

    out_args = (x, mod, o, sg, ogm, lw["gn"], lw["w_out"], lw["g_ffn"], lw["w_up"], state_conv,
                lw["conv_w"], lw["conv_b"], lw["w_down"], g_final)
    once = pl.Buffered(1)
    in_specs = [_whole(a.shape, once) for a in out_args]
    in_specs[1] = pl.BlockSpec((None, n, N_MOD * D_MODEL), lambda i: (layer, 0, 0), pipeline_mode=once)
    for i in (5, 6, 7, 8, 9, 10, 11, 12):
        in_specs[i] = _layer_slab(out_args[i].shape, layer, once)
    first = conv_acc is None
    carried = [] if first else [conv_acc]
    slab_spec = pl.BlockSpec((None,) + tuple(state_conv.shape[1:]),
                             lambda i: (lax.rem(layer + i, depth), 0, 0, 0), pipeline_mode=once)

    y, conv_acc = pl.pallas_call(
        functools.partial(_sample_out_body, final_norm=final_norm, first=first),
        grid=(depth if first else 1,),
        in_specs=in_specs + [pl.BlockSpec(memory_space=pl.ANY)] * len(carried),
        out_specs=[_whole((n, D_MODEL)), slab_spec],
        out_shape=[jax.ShapeDtypeStruct((n, D_MODEL), F32), jax.ShapeDtypeStruct(state_conv.shape, F32)],
        input_output_aliases={len(out_args): 1} if carried else {},
        compiler_params=params,
        name="sample_out",
    )(*out_args, *carried)
    return y, ret_acc, conv_acc, vn


def _rope_tables(pos):
    half = HEAD_W // 2
    freqs = np.exp(-math.log(ROPE_BASE) * np.arange(half, dtype=np.float64) / half)
    ang = np.asarray(pos, dtype=np.float64)[:, None] * freqs[None, :]
    cos, sin = np.cos(ang), np.sin(ang)
    return (np.concatenate([cos, cos], axis=-1).astype(np.float32),
            np.concatenate([-sin, sin], axis=-1).astype(np.float32))


def _decay_tables(chunk):
    lg = np.log1p(-np.exp2(-5.0 - np.arange(N_HEADS, dtype=np.float64)))
    i = np.arange(chunk, dtype=np.float64)
    diff = i[:, None] - i[None, :]
    dmask = np.where(diff[None] >= 0.0, np.exp(np.maximum(diff, 0.0)[None] * lg[:, None, None]), 0.0)
    q_dec = np.exp((i[:, None] + 1.0) * lg[None, :])
    k_dec = np.exp((chunk - 1.0 - i)[:, None] * lg[None, :])
    chunk_dec = np.exp(chunk * lg)
    return tuple(a.astype(np.float32) for a in (dmask, q_dec, k_dec, chunk_dec))


def _per_head_lanes(a):
    return a.repeat(HEAD_W, axis=-1)


def kernel(x_prompt, x_sample, state_ret, state_conv, c_prompt, c_sample, w_ada, b_ada, g_mix, w_in,
           ret_gn_gain, gmlp_ln_gain, w_s, b_s, w_out, g_ffn, w_up, conv_w, conv_b, w_down, g_final):
    depth = w_in.shape[0]
    batch, seq, _ = x_prompt.shape
    n_dec, dec_seq, _ = x_sample.shape
    assert dec_seq == 1 and seq % PROMPT_TILE == 0 and PROMPT_TILE % CHUNK == 0

    cos_p, sin_p = _rope_tables(np.arange(seq))
    cos_s, sin_s = _rope_tables(PAST_LEN + np.arange(dec_seq))
    dmask, q_dec, k_dec, chunk_dec = _decay_tables(CHUNK)
    _, _, _, gamma_s = _decay_tables(dec_seq)
    tabs = {
        "cos": cos_p, "sin": sin_p, "cos_s": cos_s, "sin_s": sin_s,
        "dmask": dmask, "qdec": _per_head_lanes(q_dec), "kdec": _per_head_lanes(k_dec),
        "cdec": np.broadcast_to(chunk_dec[:, None, None], (N_HEADS, 1, HEAD_W)),
        "gamma_s": _per_head_lanes(gamma_s[None, :]),
    }
    tabs = {name: jnp.asarray(a) for name, a in tabs.items()}

    mod = _adaln(jnp.concatenate([c_sample, c_prompt], axis=0), w_ada, b_ada)
    mod_p = mod[:, n_dec:].reshape(depth, batch, N_MOD, D_MODEL)
    tril = np.tril(np.ones((CHUNK, CHUNK), dtype=bool))
    lw = {
        "g_mix": g_mix.reshape(depth, 1, D_MODEL), "g_ffn": g_ffn.reshape(depth, 1, D_MODEL),
        "w_in": _to_bf16(w_in), "w_out": _to_bf16(w_out),
        "w_up": _to_bf16(w_up, group_gate_columns=True), "w_down": _to_bf16(w_down),
        "gn": ret_gn_gain.reshape(depth, 1, RET_W), "ln": gmlp_ln_gain.reshape(depth, 1, GM_W),
        "ws_tril": jnp.where(tril, w_s, 0.0).astype(BF16),
        "bs_full": _per_head_lanes(b_s.transpose(0, 2, 1)),
        "ws0": _per_head_lanes(w_s[:, None, :, 0, 0]), "bs0": _per_head_lanes(b_s[:, None, :, 0]),
        "conv_w": conv_w, "conv_b": conv_b.reshape(depth, 1, 2 * D_FF),
    }
    g_fin = g_final.reshape(1, D_MODEL)

    xp, xs = x_prompt, x_sample.reshape(n_dec, D_MODEL)
    ret_p, conv_p, v_s = [], [], []
    ret_s = conv_s = None
    for l in range(depth):
        last = l == depth - 1
        xp, sp, bp = _prompt_layer(l, xp, mod_p, lw, tabs, g_fin, last)
        xs, ret_s, conv_s, vs = _sample_layer(l, xs, mod, state_ret, state_conv, ret_s, conv_s, lw, tabs,
                                              g_fin, last)
        ret_p.append(sp); conv_p.append(bp)
        v_s.append(vs.reshape(n_dec, dec_seq, GM_W))
    return (xp, xs.reshape(n_dec, dec_seq, D_MODEL), jnp.stack(ret_p), jnp.stack(conv_p),
            ret_s, conv_s, jnp.stack(v_s))
```

```python
import functools
import math

import jax
import jax.numpy as jnp
import numpy as np
from jax import lax
from jax.experimental import pallas as pl
from jax.experimental.pallas import tpu as pltpu

D_MODEL = 1024
N_HEADS = 4
HEAD_W = 128
RET_W = N_HEADS * HEAD_W
GM_W = N_HEADS * HEAD_W
IN_W = 4 * RET_W + 2 * GM_W
D_FF = 2048
CONV_W = 3
CHUNK = 128
ROPE_BASE = 10000.0
PAST_LEN = 16384
EPS = 1e-6
N_MOD = 6

V7X_VMEM_LIMIT_BYTES = 60 * 1024 * 1024
PROMPT_TILE = 512
ROW_BLOCK = 64
CONV_PAD = 8
GATE_GROUP_W = 256
CAST_COLS = 1024
DOWN_SPLIT_GROUPS = 4
SAMPLE_SEQ_BLOCK = 16

F32 = jnp.float32
BF16 = jnp.bfloat16
NT_DIMS = (((1,), (1,)), ((), ()))
TN_DIMS = (((0,), (0,)), ((), ()))


def _silu(x):
    return x * (1.0 / (1.0 + jnp.exp(-x)))


def _gelu_tanh(x):
    c = math.sqrt(2.0 / math.pi)
    return 0.5 * x * (1.0 + jnp.tanh(c * (x + 0.044715 * (x * x * x))))


def _rms_unit(x):
    return x * lax.rsqrt(jnp.mean(x * x, axis=-1, keepdims=True) + EPS)


def _center_unit(x):
    xc = x - jnp.mean(x, axis=-1, keepdims=True)
    return xc * lax.rsqrt(jnp.mean(xc * xc, axis=-1, keepdims=True) + EPS)


def _rope(x, cos_full, sin_signed):
    return x * cos_full + pltpu.roll(x, HEAD_W // 2, 1) * sin_signed


def _dot(a, b):
    return jnp.dot(a, b, preferred_element_type=F32)


NORM_BUNDLES = 120
RESID_BUNDLES = 60
CONV_BUNDLES = 170
HEAD_BUNDLES = 160
GROUP_BUNDLES = 130


def _two_unit_order(pieces, orders):
    unit_free = {unit: 0 for unit in orders}
    head = {unit: 0 for unit in orders}
    finish = {}
    order = []
    while any(head[unit] < len(names) for unit, names in orders.items()):
        ready = []
        for unit, names in orders.items():
            if head[unit] < len(names):
                name = names[head[unit]]
                deps = pieces[name][2]
                if all(d in finish for d in deps):
                    ready.append((max([unit_free[unit]] + [finish[d] for d in deps]), unit, name))
        assert ready, "piece lists contradict the dependencies"
        start, unit, name = min(ready)
        finish[name] = unit_free[unit] = start + pieces[name][1]
        head[unit] += 1
        order.append(name)
    return order


def _cast_body(*refs):
    o_ref = refs[-1]
    width = o_ref.shape[-1] // (len(refs) - 1)
    for i, w_ref in enumerate(refs[:-1]):
        o_ref[:, i * width:(i + 1) * width] = w_ref[...].astype(BF16)


def _to_bf16(w, group_gate_columns=False):
    depth, k, n = w.shape
    bn = CAST_COLS
    if group_gate_columns:
        gw = GATE_GROUP_W
        n_grp = n // (2 * gw)
        per_step = bn // (2 * gw)
        srcs = [pl.BlockSpec((None, k, gw), lambda l, c, g=g, half=half: (l, 0, half * n_grp + c * per_step + g))
                for g in range(per_step) for half in range(2)]
    else:
        srcs = [pl.BlockSpec((None, k, bn), lambda l, c: (l, 0, c))]
    return pl.pallas_call(
        _cast_body,
        grid=(depth, n // bn),
        in_specs=srcs,
        out_specs=pl.BlockSpec((None, k, bn), lambda l, c: (l, 0, c)),
        out_shape=jax.ShapeDtypeStruct(w.shape, BF16),
        compiler_params=pltpu.CompilerParams(
            dimension_semantics=("arbitrary", "arbitrary"), vmem_limit_bytes=V7X_VMEM_LIMIT_BYTES),
        name="to_bf16",
    )(*([w] * len(srcs)))


def _adaln_body(c_ref, w_ref, b_ref, o_ref):
    c = _silu(c_ref[...]).astype(BF16)
    o_ref[...] = _dot(c, w_ref[...].astype(BF16)) + b_ref[...]


def _adaln(c_all, w_ada, b_ada):
    depth = w_ada.shape[0]
    rows = c_all.shape[0]
    n_out = w_ada.shape[2]
    bn = 3 * D_MODEL
    return pl.pallas_call(
        _adaln_body,
        grid=(depth, n_out // bn),
        in_specs=[
            pl.BlockSpec((rows, D_MODEL), lambda l, j: (0, 0)),
            pl.BlockSpec((None, D_MODEL, bn), lambda l, j: (l, 0, j)),
            pl.BlockSpec((None, 1, bn), lambda l, j: (l, 0, j)),
        ],
        out_specs=pl.BlockSpec((None, rows, bn), lambda l, j: (l, 0, j)),
        out_shape=jax.ShapeDtypeStruct((depth, rows, n_out), F32),
        compiler_params=pltpu.CompilerParams(
            dimension_semantics=("arbitrary", "arbitrary"), vmem_limit_bytes=V7X_VMEM_LIMIT_BYTES),
        name="adaln",
    )(c_all, w_ada, b_ada.reshape(depth, 1, n_out))


def _prompt_layer_body(x_ref, modf_ref, modb_ref, gmix_ref, gffn_ref, win_ref, wout_ref, wup_ref, wdown_ref,
                       cos_ref, sin_ref, dmask_ref, qdec_ref, kdec_ref, cdec_ref, gn_ref, ln_ref,
                       ws_ref, bs_ref, convw_ref, convb_ref, gfin_ref,
                       y_ref, ret_ref, conv_ref,
                       h_ref, p_ref, mix_ref, mo_ref, x1_ref, h2_ref, f_ref, mo2_ref, *a_refs,
                       tm, tiles_per_seq, n_tiles, final_norm):
    s = pl.program_id(0)
    front_on = s < n_tiles
    t_front = lax.rem(jnp.minimum(s, n_tiles - 1), tiles_per_seq)
    t_back = lax.rem(jnp.maximum(s - 1, 0), tiles_per_seq)

    @pl.when(s == 0)
    def _():
        x1_ref[...] = jnp.zeros_like(x1_ref)
        h2_ref[...] = jnp.zeros_like(h2_ref)

    @pl.when(jnp.logical_and(front_on, t_front == 0))
    def _():
        ret_ref[...] = jnp.zeros_like(ret_ref)

    @pl.when(t_back == 0)
    def _():
        conv_ref[...] = jnp.zeros_like(conv_ref)

    sh1, sc1, gt1 = modf_ref[0:1, :], modf_ref[1:2, :], modf_ref[2:3, :]
    sh2, sc2 = modf_ref[3:4, :], modf_ref[4:5, :]
    gt2 = modb_ref[5:6, :]
    row_blocks = [slice(r * ROW_BLOCK, (r + 1) * ROW_BLOCK) for r in range(tm // ROW_BLOCK)]

    cb = GATE_GROUP_W
    tail = slice(CONV_PAD - (CONV_W - 1), CONV_PAD)

    pieces = {}
    n_rb = len(row_blocks)
    n_grp = D_FF // cb
    n_chunks = tm // CHUNK

    def matmul_bundles(k, n):
        return (tm // 16) * (k // 256) * (n // 256) * 8 // 2

    def up_group(j):
        for half in range(2):
            c0 = half * D_FF + j * cb
            a_refs[j][tail, half * cb:(half + 1) * cb] = conv_ref[:, c0:c0 + cb]
        a_refs[j][CONV_PAD:CONV_PAD + tm, :] = _dot(h2_ref[...], wup_ref[:, 2 * j * cb:2 * (j + 1) * cb])
        for half in range(2):
            c0 = half * D_FF + j * cb
            conv_ref[:, c0:c0 + cb] = a_refs[j][CONV_PAD + tm - (CONV_W - 1):CONV_PAD + tm,
                                                half * cb:(half + 1) * cb]

    def conv_rows(j, r):
        halves = []
        for half in range(2):
            c0 = half * D_FF + j * cb
            src = slice(half * cb, (half + 1) * cb)
            window = a_refs[j][r * ROW_BLOCK:CONV_PAD + (r + 1) * ROW_BLOCK, src]
            acc = convb_ref[:, c0:c0 + cb]
            for tap in range(CONV_W):
                back = CONV_W - 1 - tap
                rows_back = pltpu.roll(window, back, 0) if back else window
                acc = acc + rows_back[CONV_PAD:, :] * convw_ref[tap:tap + 1, c0:c0 + cb]
            halves.append(acc)
        f_ref[row_blocks[r], j * cb:(j + 1) * cb] = (_silu(halves[0]) * halves[1]).astype(BF16)

    def down_proj(g0, g1):
        part = _dot(f_ref[:, g0 * cb:g1 * cb], wdown_ref[g0 * cb:g1 * cb, :])
        mo2_ref[...] = part if g0 == 0 else mo2_ref[...] + part

    def resid_out(r):
        rows = row_blocks[r]
        x2 = y_ref[rows, :] + gt2 * mo2_ref[rows, :]
        if final_norm:
            x2 = _rms_unit(x2) * gfin_ref[...]
        y_ref[rows, :] = x2

    for j in range(n_grp):
        pieces[f"up{j}"] = ("mxu", matmul_bundles(D_MODEL, 2 * cb), (), functools.partial(up_group, j))
        for r in range(n_rb):
            pieces[f"conv{j}.{r}"] = ("valu", CONV_BUNDLES, (f"up{j}",), functools.partial(conv_rows, j, r))
    all_conv = tuple(f"conv{j}.{r}" for j in range(n_grp) for r in range(n_rb))
    split = DOWN_SPLIT_GROUPS
    pieces["down_a"] = ("mxu", matmul_bundles(split * cb, D_MODEL), all_conv[:split * n_rb],
                        functools.partial(down_proj, 0, split))
    pieces["down"] = ("mxu", matmul_bundles((n_grp - split) * cb, D_MODEL), all_conv + ("down_a",),
                      functools.partial(down_proj, split, n_grp))
    for r in range(n_rb):
        pieces[f"y{r}"] = ("valu", RESID_BUNDLES, ("down",), functools.partial(resid_out, r))

    def norm_rows(r):
        rows = row_blocks[r]
        h = _rms_unit(x_ref[rows, :]) * gmix_ref[...]
        h_ref[rows, :] = (h * (1.0 + sc1) + sh1).astype(BF16)

    def in_cols(c0, c1):
        p_ref[:, c0:c1] = _dot(h_ref[...], win_ref[:, c0:c1])

    all_norm = tuple(f"norm{r}" for r in range(n_rb))
    for r in range(n_rb):
        pieces[f"norm{r}"] = ("valu", NORM_BUNDLES, (), functools.partial(norm_rows, r))
    pieces["in_ret"] = ("mxu", matmul_bundles(D_MODEL, 4 * RET_W), all_norm, functools.partial(in_cols, 0, 4 * RET_W))
    pieces["in_gm"] = ("mxu", matmul_bundles(D_MODEL, 2 * GM_W), all_norm,
                       functools.partial(in_cols, 4 * RET_W, IN_W))

    def mix_heads(c):
        rows = slice(c * CHUNK, (c + 1) * CHUNK)
        heads = [slice(hh * HEAD_W, (hh + 1) * HEAD_W) for hh in range(N_HEADS)]
        cos_full, sin_signed = cos_ref[rows, :], sin_ref[rows, :]
        q = [_rope(p_ref[rows, hh * HEAD_W:(hh + 1) * HEAD_W], cos_full, sin_signed) for hh in range(N_HEADS)]
        k = [_rope(p_ref[rows, RET_W + hh * HEAD_W:RET_W + (hh + 1) * HEAD_W], cos_full, sin_signed)
             * (HEAD_W ** -0.5) for hh in range(N_HEADS)]
        vb = [p_ref[rows, 2 * RET_W + hh * HEAD_W:2 * RET_W + (hh + 1) * HEAD_W].astype(BF16)
              for hh in range(N_HEADS)]
        scores = [lax.dot_general(q[hh].astype(BF16), k[hh].astype(BF16), NT_DIMS, preferred_element_type=F32)
                  for hh in range(N_HEADS)]
        update = [lax.dot_general((k[hh] * kdec_ref[:, heads[hh]]).astype(BF16), vb[hh], TN_DIMS,
                                  preferred_element_type=F32) for hh in range(N_HEADS)]
        state = [ret_ref[hh] for hh in range(N_HEADS)]
        o = [_dot(jnp.concatenate([(scores[hh] * dmask_ref[hh]).astype(BF16),
                                   (q[hh] * qdec_ref[:, heads[hh]]).astype(BF16)], axis=1),
                  jnp.concatenate([vb[hh], state[hh].astype(BF16)], axis=0)) for hh in range(N_HEADS)]
        for hh in range(N_HEADS):
            ret_ref[hh] = jnp.where(front_on, cdec_ref[hh] * state[hh] + update[hh], state[hh])
        for hh in range(N_HEADS):
            on = _center_unit(o[hh]) * gn_ref[:, heads[hh]]
            g = p_ref[rows, 3 * RET_W + hh * HEAD_W:3 * RET_W + (hh + 1) * HEAD_W]
            mix_ref[rows, heads[hh]] = (_silu(g) * on).astype(BF16)

    def mix_groups(c):
        rows = slice(c * CHUNK, (c + 1) * CHUNK)
        groups = [slice(gg * HEAD_W, (gg + 1) * HEAD_W) for gg in range(N_HEADS)]
        vn = [(_center_unit(_gelu_tanh(p_ref[rows, 4 * RET_W + GM_W + gg * HEAD_W:4 * RET_W + GM_W + (gg + 1) * HEAD_W]))
               * ln_ref[:, groups[gg]]).astype(BF16) for gg in range(N_HEADS)]
        gate = [_dot(ws_ref[gg], vn[gg]) for gg in range(N_HEADS)]
        for gg in range(N_HEADS):
            u = _gelu_tanh(p_ref[rows, 4 * RET_W + gg * HEAD_W:4 * RET_W + (gg + 1) * HEAD_W])
            mix_ref[rows, RET_W + gg * HEAD_W:RET_W + (gg + 1) * HEAD_W] = (
                u * (gate[gg] + bs_ref[:, groups[gg]])).astype(BF16)

    def out_proj():
        mo_ref[...] = _dot(mix_ref[...], wout_ref[...])

    def resid_mid(r):
        rows = row_blocks[r]
        x1 = x_ref[rows, :] + gt1 * mo_ref[rows, :]
        x1_ref[rows, :] = x1
        h = _rms_unit(x1) * gffn_ref[...]
        h2_ref[rows, :] = (h * (1.0 + sc2) + sh2).astype(BF16)

    all_mix = []
    for c in range(n_chunks):
        pieces[f"heads{c}"] = ("valu", N_HEADS * HEAD_BUNDLES, ("in_ret",), functools.partial(mix_heads, c))
        pieces[f"groups{c}"] = ("valu", N_HEADS * GROUP_BUNDLES, ("in_gm",), functools.partial(mix_groups, c))
        all_mix += [f"heads{c}", f"groups{c}"]
    pieces["out"] = ("mxu", matmul_bundles(RET_W + GM_W, D_MODEL), tuple(all_mix), out_proj)
    all_up = tuple(f"up{j}" for j in range(n_grp))
    for r in range(n_rb):
        pieces[f"x1{r}"] = ("valu", RESID_BUNDLES + NORM_BUNDLES, ("out",) + all_up, functools.partial(resid_mid, r))

    orders = {
        "mxu": list(all_up) + ["in_gm", "in_ret", "down_a", "down", "out"],
        "valu": list(all_norm) + list(all_conv) + sorted(all_mix)
        + [f"y{r}" for r in range(n_rb)] + [f"x1{r}" for r in range(n_rb)],
    }
    y_ref[...] = x1_ref[...]
    for name in _two_unit_order(pieces, orders):
        pieces[name][3]()


def _const_spec(shape):
    zeros = (0,) * len(shape)
    return pl.BlockSpec(shape, lambda s: zeros, pipeline_mode=pl.Buffered(1))


def _prompt_layer(layer, x, mod, lw, tabs, g_final, final_norm):
    batch, seq, _ = x.shape

    def layer_spec(name):
        shape = lw[name].shape
        zeros = (0,) * (len(shape) - 1)
        return pl.BlockSpec((None,) + tuple(shape[1:]), lambda s: (layer,) + zeros, pipeline_mode=pl.Buffered(1))

    tm = PROMPT_TILE
    tiles_per_seq = seq // tm
    n_tiles = batch * tiles_per_seq
    body = functools.partial(_prompt_layer_body, tm=tm, tiles_per_seq=tiles_per_seq, n_tiles=n_tiles,
                             final_norm=final_norm)

    def front(s):
        i = jnp.minimum(s, n_tiles - 1)
        return i // tiles_per_seq, lax.rem(i, tiles_per_seq)

    def back(s):
        i = jnp.maximum(s - 1, 0)
        return i // tiles_per_seq, lax.rem(i, tiles_per_seq)

    in_specs = [
        pl.BlockSpec((None, tm, D_MODEL), lambda s: (*front(s), 0)),
        pl.BlockSpec((None, None, N_MOD, D_MODEL), lambda s: (layer, front(s)[0], 0, 0)),
        pl.BlockSpec((None, None, N_MOD, D_MODEL), lambda s: (layer, back(s)[0], 0, 0)),
        layer_spec("g_mix"), layer_spec("g_ffn"),
        layer_spec("w_in"), layer_spec("w_out"), layer_spec("w_up"), layer_spec("w_down"),
        pl.BlockSpec((tm, HEAD_W), lambda s: (front(s)[1], 0)),
        pl.BlockSpec((tm, HEAD_W), lambda s: (front(s)[1], 0)),
        _const_spec((N_HEADS, CHUNK, CHUNK)), _const_spec((CHUNK, RET_W)), _const_spec((CHUNK, RET_W)),
        _const_spec((N_HEADS, 1, HEAD_W)), layer_spec("gn"), layer_spec("ln"),
        layer_spec("ws_tril"), layer_spec("bs_full"),
        layer_spec("conv_w"), layer_spec("conv_b"), _const_spec((1, D_MODEL)),
    ]
    out_specs = [
        pl.BlockSpec((None, tm, D_MODEL), lambda s: (*back(s), 0)),
        pl.BlockSpec((None, N_HEADS, HEAD_W, HEAD_W), lambda s: (front(s)[0], 0, 0, 0)),
        pl.BlockSpec((None, CONV_W - 1, 2 * D_FF), lambda s: (back(s)[0], 0, 0)),
    ]
    out_shape = [
        jax.ShapeDtypeStruct((batch, seq, D_MODEL), F32),
        jax.ShapeDtypeStruct((batch, N_HEADS, HEAD_W, HEAD_W), F32),
        jax.ShapeDtypeStruct((batch, CONV_W - 1, 2 * D_FF), F32),
    ]
    scratch = [
        pltpu.VMEM((tm, D_MODEL), BF16),
        pltpu.VMEM((tm, IN_W), F32),
        pltpu.VMEM((tm, RET_W + GM_W), BF16),
        pltpu.VMEM((tm, D_MODEL), F32),
        pltpu.VMEM((tm, D_MODEL), F32),
        pltpu.VMEM((tm, D_MODEL), BF16),
        pltpu.VMEM((tm, D_FF), BF16),
        pltpu.VMEM((tm, D_MODEL), F32),
    ] + [pltpu.VMEM((CONV_PAD + tm, 2 * GATE_GROUP_W), F32)] * (D_FF // GATE_GROUP_W)
    return pl.pallas_call(
        body,
        grid=(n_tiles + 1,),
        in_specs=in_specs,
        out_specs=out_specs,
        out_shape=out_shape,
        scratch_shapes=scratch,
        compiler_params=pltpu.CompilerParams(
            dimension_semantics=("arbitrary",), vmem_limit_bytes=V7X_VMEM_LIMIT_BYTES),
        name="prompt_layer",
    )(x, mod, mod, lw["g_mix"], lw["g_ffn"], lw["w_in"], lw["w_out"], lw["w_up"], lw["w_down"],
      tabs["cos"], tabs["sin"], tabs["dmask"], tabs["qdec"], tabs["kdec"], tabs["cdec"],
      lw["gn"], lw["ln"], lw["ws_tril"], lw["bs_full"], lw["conv_w"], lw["conv_b"], g_final)


def _sample_in_body(x_ref, mod_ref, gmix_ref, win_ref, cos_ref, sin_ref, ln_ref, ws0_ref, bs0_ref,
                    q_ref, k_ref, v_ref, sg_ref, ogm_ref, vn_ref):
    sh1, sc1 = mod_ref[:, 0:D_MODEL], mod_ref[:, D_MODEL:2 * D_MODEL]
    h = _rms_unit(x_ref[...]) * gmix_ref[...]
    h = (h * (1.0 + sc1) + sh1).astype(BF16)
    p = _dot(h, win_ref[...])
    for hh in range(N_HEADS):
        cs = slice(hh * HEAD_W, (hh + 1) * HEAD_W)
        q_ref[:, cs] = _rope(p[:, hh * HEAD_W:(hh + 1) * HEAD_W], cos_ref[...], sin_ref[...])
        k_ref[:, cs] = _rope(p[:, RET_W + hh * HEAD_W:RET_W + (hh + 1) * HEAD_W],
                             cos_ref[...], sin_ref[...]) * (HEAD_W ** -0.5)
        u = _gelu_tanh(p[:, 4 * RET_W + hh * HEAD_W:4 * RET_W + (hh + 1) * HEAD_W])
        vv = _gelu_tanh(p[:, 4 * RET_W + GM_W + hh * HEAD_W:4 * RET_W + GM_W + (hh + 1) * HEAD_W])
        vn = _center_unit(vv) * ln_ref[:, cs]
        vn_ref[:, cs] = vn
        ogm_ref[:, cs] = u * (ws0_ref[:, cs] * vn + bs0_ref[:, cs])
    v_ref[...] = p[:, 2 * RET_W:3 * RET_W]
    sg_ref[...] = _silu(p[:, 3 * RET_W:4 * RET_W])


def _sample_ret_body(q_ref, k_ref, v_ref, s_ref, gam_ref, *rest, sb, layer, first):
    o_ref, snew_ref = rest[-2:]
    if first:
        for other in range(snew_ref.shape[0]):
            if other != layer:
                snew_ref[other] = jnp.zeros(snew_ref.shape[1:], F32)
        snew_ref = snew_ref.at[layer]
    for hh in range(N_HEADS):
        cs = slice(hh * HEAD_W, (hh + 1) * HEAD_W)
        gam = gam_ref[:, cs]
        v = v_ref[:, cs]
        k = k_ref[:, cs]
        qb = q_ref[:, cs].astype(BF16)
        kt = jnp.concatenate([k, jnp.zeros((HEAD_W - sb, HEAD_W), F32)], axis=0).T
        for s in range(sb):
            state = s_ref[s, hh]
            snew_ref[s, hh] = gam * state + kt[:, s:s + 1] * v[s:s + 1, :]
            o_ref[s:s + 1, cs] = gam * _dot(qb, state.astype(BF16))[s:s + 1, :]
        qk = jnp.sum(q_ref[:, cs] * k, axis=-1, keepdims=True)
        o_ref[:, cs] = o_ref[:, cs] + qk * v


def _sample_out_body(x_ref, mod_ref, o_ref, sg_ref, ogm_ref, gn_ref, wout_ref, gffn_ref, wup_ref,
                     cbuf_ref, convw_ref, convb_ref, wdown_ref, gfin_ref, *rest, final_norm, first):
    y_ref, cnew_ref = rest[-2:]
    if first:
        @pl.when(pl.program_id(0) > 0)
        def _():
            cnew_ref[...] = jnp.zeros_like(cnew_ref)

        pl.when(pl.program_id(0) == 0)(functools.partial(
            _sample_out_layer, x_ref, mod_ref, o_ref, sg_ref, ogm_ref, gn_ref, wout_ref, gffn_ref, wup_ref,
            cbuf_ref, convw_ref, convb_ref, wdown_ref, gfin_ref, y_ref, cnew_ref, final_norm))
    else:
        _sample_out_layer(x_ref, mod_ref, o_ref, sg_ref, ogm_ref, gn_ref, wout_ref, gffn_ref, wup_ref,
                          cbuf_ref, convw_ref, convb_ref, wdown_ref, gfin_ref, y_ref, cnew_ref, final_norm)


def _sample_out_layer(x_ref, mod_ref, o_ref, sg_ref, ogm_ref, gn_ref, wout_ref, gffn_ref, wup_ref,
                      cbuf_ref, convw_ref, convb_ref, wdown_ref, gfin_ref, y_ref, cnew_ref, final_norm):
    gt1 = mod_ref[:, 2 * D_MODEL:3 * D_MODEL]
    sh2, sc2 = mod_ref[:, 3 * D_MODEL:4 * D_MODEL], mod_ref[:, 4 * D_MODEL:5 * D_MODEL]
    gt2 = mod_ref[:, 5 * D_MODEL:6 * D_MODEL]
    parts = []
    for hh in range(N_HEADS):
        cs = slice(hh * HEAD_W, (hh + 1) * HEAD_W)
        on = _center_unit(o_ref[:, cs]) * gn_ref[:, cs]
        parts.append((sg_ref[:, cs] * on).astype(BF16))
    parts.append(ogm_ref[...].astype(BF16))
    mix = _dot(jnp.concatenate(parts, axis=1), wout_ref[...])
    x1 = x_ref[...] + gt1 * mix
    h = _rms_unit(x1) * gffn_ref[...]
    h = (h * (1.0 + sc2) + sh2).astype(BF16)
    a = _dot(h, wup_ref[...])
    gw = GATE_GROUP_W
    f_parts = []
    for j in range(D_FF // gw):
        halves = []
        for half in range(2):
            c0 = half * D_FF + j * gw
            a_blk = a[:, (2 * j + half) * gw:(2 * j + half + 1) * gw]
            buf0, buf1 = cbuf_ref[:, 0, c0:c0 + gw], cbuf_ref[:, 1, c0:c0 + gw]
            halves.append(convb_ref[:, c0:c0 + gw] + buf0 * convw_ref[0:1, c0:c0 + gw]
                          + buf1 * convw_ref[1:2, c0:c0 + gw] + a_blk * convw_ref[2:3, c0:c0 + gw])
            cnew_ref[:, 0, c0:c0 + gw] = buf1
            cnew_ref[:, 1, c0:c0 + gw] = a_blk
        f_parts.append((_silu(halves[0]) * halves[1]).astype(BF16))
    f = jnp.concatenate(f_parts, axis=1)
    x2 = x1 + gt2 * _dot(f, wdown_ref[...])
    if final_norm:
        x2 = _rms_unit(x2) * gfin_ref[...]
    y_ref[...] = x2


def _whole(shape, pipeline_mode=None):
    zeros = (0,) * len(shape)
    return pl.BlockSpec(shape, lambda *_: zeros, pipeline_mode=pipeline_mode)


def _layer_slab(shape, layer, pipeline_mode=None):
    zeros = (0,) * (len(shape) - 1)
    return pl.BlockSpec((None,) + tuple(shape[1:]), lambda *_: (layer,) + zeros, pipeline_mode=pipeline_mode)


def _sample_layer(layer, x, mod, state_ret, state_conv, ret_acc, conv_acc, lw, tabs, g_final, final_norm):
    n = x.shape[0]
    params = pltpu.CompilerParams(dimension_semantics=("arbitrary",), vmem_limit_bytes=V7X_VMEM_LIMIT_BYTES)
    act = jax.ShapeDtypeStruct((n, RET_W), F32)
    in_args = (x, mod, lw["g_mix"], lw["w_in"], tabs["cos_s"], tabs["sin_s"], lw["ln"], lw["ws0"], lw["bs0"])
    in_specs = [_whole(a.shape) for a in in_args]
    in_specs[1] = pl.BlockSpec((None, n, N_MOD * D_MODEL), lambda i: (layer, 0, 0))
    for i in (2, 3, 6, 7, 8):
        in_specs[i] = _layer_slab(in_args[i].shape, layer)
    q, k, v, sg, ogm, vn = pl.pallas_call(
        _sample_in_body,
        grid=(1,),
        in_specs=in_specs,
        out_specs=[_whole(act.shape)] * 6,
        out_shape=[act] * 6,
        compiler_params=params,
        name="sample_in",
    )(*in_args)

    sb = SAMPLE_SEQ_BLOCK
    row_spec = pl.BlockSpec((sb, RET_W), lambda i: (i, 0))
    state_spec = pl.BlockSpec((None, sb, N_HEADS, HEAD_W, HEAD_W), lambda i: (layer, i, 0, 0, 0))
    carried = [] if ret_acc is None else [ret_acc]
    depth = state_ret.shape[0]
    first = ret_acc is None
    all_layers_spec = pl.BlockSpec((depth, sb, N_HEADS, HEAD_W, HEAD_W), lambda i: (0, i, 0, 0, 0))
    o, ret_acc = pl.pallas_call(
        functools.partial(_sample_ret_body, sb=sb, layer=layer, first=first),
        grid=(n // sb,),
        in_specs=[row_spec, row_spec, row_spec, state_spec, _whole((1, RET_W))]
        + [pl.BlockSpec(memory_space=pl.ANY)] * len(carried),
        out_specs=[row_spec, all_layers_spec if first else state_spec],
        out_shape=[act, jax.ShapeDtypeStruct(state_ret.shape, F32)],
        input_output_aliases={5: 1} if carried else {},
        compiler_params=params,
        name="sample_ret",
    )(q, k, v, state_ret, tabs["gamma_s"], *carried)

    out_args = (x, mod, o, sg, ogm, lw["gn"], lw["w_out"], lw["g_ffn"], lw["w_up"], state_conv,
                lw["conv_w"], lw["conv_b"], lw["w_down"], g_final)
    once = pl.Buffered(1)
    in_specs = [_whole(a.shape, once) for a in out_args]
    in_specs[1] = pl.BlockSpec((None, n, N_MOD * D_MODEL), lambda i: (layer, 0, 0), pipeline_mode=once)
    for i in (5, 6, 7, 8, 9, 10, 11, 12):
        in_specs[i] = _layer_slab(out_args[i].shape, layer, once)
    first = conv_acc is None
    carried = [] if first else [conv_acc]
    slab_spec = pl.BlockSpec((None,) + tuple(state_conv.shape[1:]),
                             lambda i: (lax.rem(layer + i, depth), 0, 0, 0), pipeline_mode=once)

    y, conv_acc = pl.pallas_call(
        functools.partial(_sample_out_body, final_norm=final_norm, first=first),
        grid=(depth if first else 1,),
        in_specs=in_specs + [pl.BlockSpec(memory_space=pl.ANY)] * len(carried),
        out_specs=[_whole((n, D_MODEL)), slab_spec],
        out_shape=[jax.ShapeDtypeStruct((n, D_MODEL), F32), jax.ShapeDtypeStruct(state_conv.shape, F32)],
        input_output_aliases={len(out_args): 1} if carried else {},
        compiler_params=params,
        name="sample_out",
    )(*out_args, *carried)
    return y, ret_acc, conv_acc, vn


def _rope_tables(pos):
    half = HEAD_W // 2
    freqs = np.exp(-math.log(ROPE_BASE) * np.arange(half, dtype=np.float64) / half)
    ang = np.asarray(pos, dtype=np.float64)[:, None] * freqs[None, :]
    cos, sin = np.cos(ang), np.sin(ang)
    return (np.concatenate([cos, cos], axis=-1).astype(np.float32),
            np.concatenate([-sin, sin], axis=-1).astype(np.float32))


def _decay_tables(chunk):
    lg = np.log1p(-np.exp2(-5.0 - np.arange(N_HEADS, dtype=np.float64)))
    i = np.arange(chunk, dtype=np.float64)
    diff = i[:, None] - i[None, :]
    dmask = np.where(diff[None] >= 0.0, np.exp(np.maximum(diff, 0.0)[None] * lg[:, None, None]), 0.0)
    q_dec = np.exp((i[:, None] + 1.0) * lg[None, :])
    k_dec = np.exp((chunk - 1.0 - i)[:, None] * lg[None, :])
    chunk_dec = np.exp(chunk * lg)
    return tuple(a.astype(np.float32) for a in (dmask, q_dec, k_dec, chunk_dec))


def _per_head_lanes(a):
    return a.repeat(HEAD_W, axis=-1)


def kernel(x_prompt, x_sample, state_ret, state_conv, c_prompt, c_sample, w_ada, b_ada, g_mix, w_in,
           ret_gn_gain, gmlp_ln_gain, w_s, b_s, w_out, g_ffn, w_up, conv_w, conv_b, w_down, g_final):
    depth = w_in.shape[0]
    batch, seq, _ = x_prompt.shape
    n_dec, dec_seq, _ = x_sample.shape
    assert dec_seq == 1 and seq % PROMPT_TILE == 0 and PROMPT_TILE % CHUNK == 0

    cos_p, sin_p = _rope_tables(np.arange(seq))
    cos_s, sin_s = _rope_tables(PAST_LEN + np.arange(dec_seq))
    dmask, q_dec, k_dec, chunk_dec = _decay_tables(CHUNK)
    _, _, _, gamma_s = _decay_tables(dec_seq)
    tabs = {
        "cos": cos_p, "sin": sin_p, "cos_s": cos_s, "sin_s": sin_s,
        "dmask": dmask, "qdec": _per_head_lanes(q_dec), "kdec": _per_head_lanes(k_dec),
        "cdec": np.broadcast_to(chunk_dec[:, None, None], (N_HEADS, 1, HEAD_W)),
        "gamma_s": _per_head_lanes(gamma_s[None, :]),
    }
    tabs = {name: jnp.asarray(a) for name, a in tabs.items()}

    mod = _adaln(jnp.concatenate([c_sample, c_prompt], axis=0), w_ada, b_ada)
    mod_p = mod[:, n_dec:].reshape(depth, batch, N_MOD, D_MODEL)
    tril = np.tril(np.ones((CHUNK, CHUNK), dtype=bool))
    lw = {
        "g_mix": g_mix.reshape(depth, 1, D_MODEL), "g_ffn": g_ffn.reshape(depth, 1, D_MODEL),
        "w_in": _to_bf16(w_in), "w_out": _to_bf16(w_out),
        "w_up": _to_bf16(w_up, group_gate_columns=True), "w_down": _to_bf16(w_down),
        "gn": ret_gn_gain.reshape(depth, 1, RET_W), "ln": gmlp_ln_gain.reshape(depth, 1, GM_W),
        "ws_tril": jnp.where(tril, w_s, 0.0).astype(BF16),
        "bs_full": _per_head_lanes(b_s.transpose(0, 2, 1)),
        "ws0": _per_head_lanes(w_s[:, None, :, 0, 0]), "bs0": _per_head_lanes(b_s[:, None, :, 0]),
        "conv_w": conv_w, "conv_b": conv_b.reshape(depth, 1, 2 * D_FF),
    }
    g_fin = g_final.reshape(1, D_MODEL)

    xp, xs = x_prompt, x_sample.reshape(n_dec, D_MODEL)
    ret_p, conv_p, v_s = [], [], []
    ret_s = conv_s = None
    for l in range(depth):
        last = l == depth - 1
        xp, sp, bp = _prompt_layer(l, xp, mod_p, lw, tabs, g_fin, last)
        xs, ret_s, conv_s, vs = _sample_layer(l, xs, mod, state_ret, state_conv, ret_s, conv_s, lw, tabs,
                                              g_fin, last)
        ret_p.append(sp); conv_p.append(bp)
        v_s.append(vs.reshape(n_dec, dec_seq, GM_W))
    return (xp, xs.reshape(n_dec, dec_seq, D_MODEL), jnp.stack(ret_p), jnp.stack(conv_p),
            ret_s, conv_s, jnp.stack(v_s))
```

```python
import functools
import math

import jax
import jax.numpy as jnp
import numpy as np
from jax import lax
from jax.experimental import pallas as pl
from jax.experimental.pallas import tpu as pltpu

D_MODEL = 1024
N_HEADS = 4
HEAD_W = 128
RET_W = N_HEADS * HEAD_W
GM_W = N_HEADS * HEAD_W
IN_W = 4 * RET_W + 2 * GM_W
D_FF = 2048
CONV_W = 3
CHUNK = 128
ROPE_BASE = 10000.0
PAST_LEN = 16384
EPS = 1e-6
N_MOD = 6

V7X_VMEM_LIMIT_BYTES = 60 * 1024 * 1024
PROMPT_TILE = 512
ROW_BLOCK = 64
CONV_PAD = 8
GATE_GROUP_W = 256
CAST_COLS = 1024
DOWN_SPLIT_GROUPS = 4
SAMPLE_SEQ_BLOCK = 16
STATE_RING = 3

F32 = jnp.float32
BF16 = jnp.bfloat16
NT_DIMS = (((1,), (1,)), ((), ()))
TN_DIMS = (((0,), (0,)), ((), ()))


def _silu(x):
    return x * (1.0 / (1.0 + jnp.exp(-x)))


def _gelu_tanh(x):
    c = math.sqrt(2.0 / math.pi)
    return 0.5 * x * (1.0 + jnp.tanh(c * (x + 0.044715 * (x * x * x))))


def _rms_unit(x):
    return x * lax.rsqrt(jnp.mean(x * x, axis=-1, keepdims=True) + EPS)


def _center_unit(x):
    xc = x - jnp.mean(x, axis=-1, keepdims=True)
    return xc * lax.rsqrt(jnp.mean(xc * xc, axis=-1, keepdims=True) + EPS)


def _rope(x, cos_full, sin_signed):
    return x * cos_full + pltpu.roll(x, HEAD_W // 2, 1) * sin_signed


def _dot(a, b):
    return jnp.dot(a, b, preferred_element_type=F32)


NORM_BUNDLES = 120
RESID_BUNDLES = 60
CONV_BUNDLES = 170
HEAD_BUNDLES = 160
GROUP_BUNDLES = 130


def _two_unit_order(pieces, orders):
    unit_free = {unit: 0 for unit in orders}
    head = {unit: 0 for unit in orders}
    finish = {}
    order = []
    while any(head[unit] < len(names) for unit, names in orders.items()):
        ready = []
        for unit, names in orders.items():
            if head[unit] < len(names):
                name = names[head[unit]]
                deps = pieces[name][2]
                if all(d in finish for d in deps):
                    ready.append((max([unit_free[unit]] + [finish[d] for d in deps]), unit, name))
        assert ready, "piece lists contradict the dependencies"
        start, unit, name = min(ready)
        finish[name] = unit_free[unit] = start + pieces[name][1]
        head[unit] += 1
        order.append(name)
    return order


def _cast_body(*refs):
    o_ref = refs[-1]
    width = o_ref.shape[-1] // (len(refs) - 1)
    for i, w_ref in enumerate(refs[:-1]):
        o_ref[:, i * width:(i + 1) * width] = w_ref[...].astype(BF16)


def _to_bf16(w, group_gate_columns=False):
    depth, k, n = w.shape
    bn = CAST_COLS
    if group_gate_columns:
        gw = GATE_GROUP_W
        n_grp = n // (2 * gw)
        per_step = bn // (2 * gw)
        srcs = [pl.BlockSpec((None, k, gw), lambda l, c, g=g, half=half: (l, 0, half * n_grp + c * per_step + g))
                for g in range(per_step) for half in range(2)]
    else:
        srcs = [pl.BlockSpec((None, k, bn), lambda l, c: (l, 0, c))]
    return pl.pallas_call(
        _cast_body,
        grid=(depth, n // bn),
        in_specs=srcs,
        out_specs=pl.BlockSpec((None, k, bn), lambda l, c: (l, 0, c)),
        out_shape=jax.ShapeDtypeStruct(w.shape, BF16),
        compiler_params=pltpu.CompilerParams(
            dimension_semantics=("arbitrary", "arbitrary"), vmem_limit_bytes=V7X_VMEM_LIMIT_BYTES),
        name="to_bf16",
    )(*([w] * len(srcs)))


def _adaln_body(c_ref, w_ref, b_ref, o_ref):
    c = _silu(c_ref[...]).astype(BF16)
    o_ref[...] = _dot(c, w_ref[...].astype(BF16)) + b_ref[...]


def _adaln(c_all, w_ada, b_ada):
    depth = w_ada.shape[0]
    rows = c_all.shape[0]
    n_out = w_ada.shape[2]
    bn = 3 * D_MODEL
    return pl.pallas_call(
        _adaln_body,
        grid=(depth, n_out // bn),
        in_specs=[
            pl.BlockSpec((rows, D_MODEL), lambda l, j: (0, 0)),
            pl.BlockSpec((None, D_MODEL, bn), lambda l, j: (l, 0, j)),
            pl.BlockSpec((None, 1, bn), lambda l, j: (l, 0, j)),
        ],
        out_specs=pl.BlockSpec((None, rows, bn), lambda l, j: (l, 0, j)),
        out_shape=jax.ShapeDtypeStruct((depth, rows, n_out), F32),
        compiler_params=pltpu.CompilerParams(
            dimension_semantics=("arbitrary", "arbitrary"), vmem_limit_bytes=V7X_VMEM_LIMIT_BYTES),
        name="adaln",
    )(c_all, w_ada, b_ada.reshape(depth, 1, n_out))


def _prompt_layer_body(x_ref, modf_ref, modb_ref, gmix_ref, gffn_ref, win_ref, wout_ref, wup_ref, wdown_ref,
                       cos_ref, sin_ref, dmask_ref, qdec_ref, kdec_ref, cdec_ref, gn_ref, ln_ref,
                       ws_ref, bs_ref, convw_ref, convb_ref, gfin_ref,
                       y_ref, ret_ref, conv_ref,
                       h_ref, p_ref, mix_ref, mo_ref, x1_ref, h2_ref, f_ref, mo2_ref, *a_refs,
                       tm, tiles_per_seq, n_tiles, final_norm):
    s = pl.program_id(0)
    front_on = s < n_tiles
    t_front = lax.rem(jnp.minimum(s, n_tiles - 1), tiles_per_seq)
    t_back = lax.rem(jnp.maximum(s - 1, 0), tiles_per_seq)

    @pl.when(s == 0)
    def _():
        x1_ref[...] = jnp.zeros_like(x1_ref)
        h2_ref[...] = jnp.zeros_like(h2_ref)

    @pl.when(jnp.logical_and(front_on, t_front == 0))
    def _():
        ret_ref[...] = jnp.zeros_like(ret_ref)

    @pl.when(t_back == 0)
    def _():
        conv_ref[...] = jnp.zeros_like(conv_ref)

    sh1, sc1, gt1 = modf_ref[0:1, :], modf_ref[1:2, :], modf_ref[2:3, :]
    sh2, sc2 = modf_ref[3:4, :], modf_ref[4:5, :]
    gt2 = modb_ref[5:6, :]
    row_blocks = [slice(r * ROW_BLOCK, (r + 1) * ROW_BLOCK) for r in range(tm // ROW_BLOCK)]

    cb = GATE_GROUP_W
    tail = slice(CONV_PAD - (CONV_W - 1), CONV_PAD)

    pieces = {}
    n_rb = len(row_blocks)
    n_grp = D_FF // cb
    n_chunks = tm // CHUNK

    def matmul_bundles(k, n):
        return (tm // 16) * (k // 256) * (n // 256) * 8 // 2

    def up_group(j):
        for half in range(2):
            c0 = half * D_FF + j * cb
            a_refs[j][tail, half * cb:(half + 1) * cb] = conv_ref[:, c0:c0 + cb]
        a_refs[j][CONV_PAD:CONV_PAD + tm, :] = _dot(h2_ref[...], wup_ref[:, 2 * j * cb:2 * (j + 1) * cb])
        for half in range(2):
            c0 = half * D_FF + j * cb
            conv_ref[:, c0:c0 + cb] = a_refs[j][CONV_PAD + tm - (CONV_W - 1):CONV_PAD + tm,
                                                half * cb:(half + 1) * cb]

    def conv_rows(j, r):
        halves = []
        for half in range(2):
            c0 = half * D_FF + j * cb
            src = slice(half * cb, (half + 1) * cb)
            window = a_refs[j][r * ROW_BLOCK:CONV_PAD + (r + 1) * ROW_BLOCK, src]
            acc = convb_ref[:, c0:c0 + cb]
            for tap in range(CONV_W):
                back = CONV_W - 1 - tap
                rows_back = pltpu.roll(window, back, 0) if back else window
                acc = acc + rows_back[CONV_PAD:, :] * convw_ref[tap:tap + 1, c0:c0 + cb]
            halves.append(acc)
        f_ref[row_blocks[r], j * cb:(j + 1) * cb] = (_silu(halves[0]) * halves[1]).astype(BF16)

    def down_proj(g0, g1):
        part = _dot(f_ref[:, g0 * cb:g1 * cb], wdown_ref[g0 * cb:g1 * cb, :])
        mo2_ref[...] = part if g0 == 0 else mo2_ref[...] + part

    def resid_out(r):
        rows = row_blocks[r]
        x2 = y_ref[rows, :] + gt2 * mo2_ref[rows, :]
        if final_norm:
            x2 = _rms_unit(x2) * gfin_ref[...]
        y_ref[rows, :] = x2

    for j in range(n_grp):
        pieces[f"up{j}"] = ("mxu", matmul_bundles(D_MODEL, 2 * cb), (), functools.partial(up_group, j))
        for r in range(n_rb):
            pieces[f"conv{j}.{r}"] = ("valu", CONV_BUNDLES, (f"up{j}",), functools.partial(conv_rows, j, r))
    all_conv = tuple(f"conv{j}.{r}" for j in range(n_grp) for r in range(n_rb))
    split = DOWN_SPLIT_GROUPS
    pieces["down_a"] = ("mxu", matmul_bundles(split * cb, D_MODEL), all_conv[:split * n_rb],
                        functools.partial(down_proj, 0, split))
    pieces["down"] = ("mxu", matmul_bundles((n_grp - split) * cb, D_MODEL), all_conv + ("down_a",),
                      functools.partial(down_proj, split, n_grp))
    for r in range(n_rb):
        pieces[f"y{r}"] = ("valu", RESID_BUNDLES, ("down",), functools.partial(resid_out, r))

    def norm_rows(r):
        rows = row_blocks[r]
        h = _rms_unit(x_ref[rows, :]) * gmix_ref[...]
        h_ref[rows, :] = (h * (1.0 + sc1) + sh1).astype(BF16)

    def in_cols(c0, c1):
        p_ref[:, c0:c1] = _dot(h_ref[...], win_ref[:, c0:c1])

    all_norm = tuple(f"norm{r}" for r in range(n_rb))
    for r in range(n_rb):
        pieces[f"norm{r}"] = ("valu", NORM_BUNDLES, (), functools.partial(norm_rows, r))
    pieces["in_ret"] = ("mxu", matmul_bundles(D_MODEL, 4 * RET_W), all_norm, functools.partial(in_cols, 0, 4 * RET_W))
    pieces["in_gm"] = ("mxu", matmul_bundles(D_MODEL, 2 * GM_W), all_norm,
                       functools.partial(in_cols, 4 * RET_W, IN_W))

    def mix_heads(c):
        rows = slice(c * CHUNK, (c + 1) * CHUNK)
        heads = [slice(hh * HEAD_W, (hh + 1) * HEAD_W) for hh in range(N_HEADS)]
        cos_full, sin_signed = cos_ref[rows, :], sin_ref[rows, :]
        q = [_rope(p_ref[rows, hh * HEAD_W:(hh + 1) * HEAD_W], cos_full, sin_signed) for hh in range(N_HEADS)]
        k = [_rope(p_ref[rows, RET_W + hh * HEAD_W:RET_W + (hh + 1) * HEAD_W], cos_full, sin_signed)
             * (HEAD_W ** -0.5) for hh in range(N_HEADS)]
        vb = [p_ref[rows, 2 * RET_W + hh * HEAD_W:2 * RET_W + (hh + 1) * HEAD_W].astype(BF16)
              for hh in range(N_HEADS)]
        scores = [lax.dot_general(q[hh].astype(BF16), k[hh].astype(BF16), NT_DIMS, preferred_element_type=F32)
                  for hh in range(N_HEADS)]
        update = [lax.dot_general((k[hh] * kdec_ref[:, heads[hh]]).astype(BF16), vb[hh], TN_DIMS,
                                  preferred_element_type=F32) for hh in range(N_HEADS)]
        state = [ret_ref[hh] for hh in range(N_HEADS)]
        o = [_dot(jnp.concatenate([(scores[hh] * dmask_ref[hh]).astype(BF16),
                                   (q[hh] * qdec_ref[:, heads[hh]]).astype(BF16)], axis=1),
                  jnp.concatenate([vb[hh], state[hh].astype(BF16)], axis=0)) for hh in range(N_HEADS)]
        for hh in range(N_HEADS):
            ret_ref[hh] = jnp.where(front_on, cdec_ref[hh] * state[hh] + update[hh], state[hh])
        for hh in range(N_HEADS):
            on = _center_unit(o[hh]) * gn_ref[:, heads[hh]]
            g = p_ref[rows, 3 * RET_W + hh * HEAD_W:3 * RET_W + (hh + 1) * HEAD_W]
            mix_ref[rows, heads[hh]] = (_silu(g) * on).astype(BF16)

    def mix_groups(c):
        rows = slice(c * CHUNK, (c + 1) * CHUNK)
        groups = [slice(gg * HEAD_W, (gg + 1) * HEAD_W) for gg in range(N_HEADS)]
        vn = [(_center_unit(_gelu_tanh(p_ref[rows, 4 * RET_W + GM_W + gg * HEAD_W:4 * RET_W + GM_W + (gg + 1) * HEAD_W]))
               * ln_ref[:, groups[gg]]).astype(BF16) for gg in range(N_HEADS)]
        gate = [_dot(ws_ref[gg], vn[gg]) for gg in range(N_HEADS)]
        for gg in range(N_HEADS):
            u = _gelu_tanh(p_ref[rows, 4 * RET_W + gg * HEAD_W:4 * RET_W + (gg + 1) * HEAD_W])
            mix_ref[rows, RET_W + gg * HEAD_W:RET_W + (gg + 1) * HEAD_W] = (
                u * (gate[gg] + bs_ref[:, groups[gg]])).astype(BF16)

    def out_proj():
        mo_ref[...] = _dot(mix_ref[...], wout_ref[...])

    def resid_mid(r):
        rows = row_blocks[r]
        x1 = x_ref[rows, :] + gt1 * mo_ref[rows, :]
        x1_ref[rows, :] = x1
        h = _rms_unit(x1) * gffn_ref[...]
        h2_ref[rows, :] = (h * (1.0 + sc2) + sh2).astype(BF16)

    all_mix = []
    for c in range(n_chunks):
        pieces[f"heads{c}"] = ("valu", N_HEADS * HEAD_BUNDLES, ("in_ret",), functools.partial(mix_heads, c))
        pieces[f"groups{c}"] = ("valu", N_HEADS * GROUP_BUNDLES, ("in_gm",), functools.partial(mix_groups, c))
        all_mix += [f"heads{c}", f"groups{c}"]
    pieces["out"] = ("mxu", matmul_bundles(RET_W + GM_W, D_MODEL), tuple(all_mix), out_proj)
    all_up = tuple(f"up{j}" for j in range(n_grp))
    for r in range(n_rb):
        pieces[f"x1{r}"] = ("valu", RESID_BUNDLES + NORM_BUNDLES, ("out",) + all_up, functools.partial(resid_mid, r))

    orders = {
        "mxu": list(all_up) + ["in_gm", "in_ret", "down_a", "down", "out"],
        "valu": list(all_norm) + list(all_conv) + sorted(all_mix)
        + [f"y{r}" for r in range(n_rb)] + [f"x1{r}" for r in range(n_rb)],
    }
    y_ref[...] = x1_ref[...]
    for name in _two_unit_order(pieces, orders):
        pieces[name][3]()


def _const_spec(shape):
    zeros = (0,) * len(shape)
    return pl.BlockSpec(shape, lambda s: zeros, pipeline_mode=pl.Buffered(1))


def _prompt_layer(layer, x, mod, lw, tabs, g_final, final_norm):
    batch, seq, _ = x.shape

    def layer_spec(name):
        shape = lw[name].shape
        zeros = (0,) * (len(shape) - 1)
        return pl.BlockSpec((None,) + tuple(shape[1:]), lambda s: (layer,) + zeros, pipeline_mode=pl.Buffered(1))

    tm = PROMPT_TILE
    tiles_per_seq = seq // tm
    n_tiles = batch * tiles_per_seq
    body = functools.partial(_prompt_layer_body, tm=tm, tiles_per_seq=tiles_per_seq, n_tiles=n_tiles,
                             final_norm=final_norm)

    def front(s):
        i = jnp.minimum(s, n_tiles - 1)
        return i // tiles_per_seq, lax.rem(i, tiles_per_seq)

    def back(s):
        i = jnp.maximum(s - 1, 0)
        return i // tiles_per_seq, lax.rem(i, tiles_per_seq)

    in_specs = [
        pl.BlockSpec((None, tm, D_MODEL), lambda s: (*front(s), 0)),
        pl.BlockSpec((None, None, N_MOD, D_MODEL), lambda s: (layer, front(s)[0], 0, 0)),
        pl.BlockSpec((None, None, N_MOD, D_MODEL), lambda s: (layer, back(s)[0], 0, 0)),
        layer_spec("g_mix"), layer_spec("g_ffn"),
        layer_spec("w_in"), layer_spec("w_out"), layer_spec("w_up"), layer_spec("w_down"),
        pl.BlockSpec((tm, HEAD_W), lambda s: (front(s)[1], 0)),
        pl.BlockSpec((tm, HEAD_W), lambda s: (front(s)[1], 0)),
        _const_spec((N_HEADS, CHUNK, CHUNK)), _const_spec((CHUNK, RET_W)), _const_spec((CHUNK, RET_W)),
        _const_spec((N_HEADS, 1, HEAD_W)), layer_spec("gn"), layer_spec("ln"),
        layer_spec("ws_tril"), layer_spec("bs_full"),
        layer_spec("conv_w"), layer_spec("conv_b"), _const_spec((1, D_MODEL)),
    ]
    out_specs = [
        pl.BlockSpec((None, tm, D_MODEL), lambda s: (*back(s), 0)),
        pl.BlockSpec((None, N_HEADS, HEAD_W, HEAD_W), lambda s: (front(s)[0], 0, 0, 0)),
        pl.BlockSpec((None, CONV_W - 1, 2 * D_FF), lambda s: (back(s)[0], 0, 0)),
    ]
    out_shape = [
        jax.ShapeDtypeStruct((batch, seq, D_MODEL), F32),
        jax.ShapeDtypeStruct((batch, N_HEADS, HEAD_W, HEAD_W), F32),
        jax.ShapeDtypeStruct((batch, CONV_W - 1, 2 * D_FF), F32),
    ]
    scratch = [
        pltpu.VMEM((tm, D_MODEL), BF16),
        pltpu.VMEM((tm, IN_W), F32),
        pltpu.VMEM((tm, RET_W + GM_W), BF16),
        pltpu.VMEM((tm, D_MODEL), F32),
        pltpu.VMEM((tm, D_MODEL), F32),
        pltpu.VMEM((tm, D_MODEL), BF16),
        pltpu.VMEM((tm, D_FF), BF16),
        pltpu.VMEM((tm, D_MODEL), F32),
    ] + [pltpu.VMEM((CONV_PAD + tm, 2 * GATE_GROUP_W), F32)] * (D_FF // GATE_GROUP_W)
    return pl.pallas_call(
        body,
        grid=(n_tiles + 1,),
        in_specs=in_specs,
        out_specs=out_specs,
        out_shape=out_shape,
        scratch_shapes=scratch,
        compiler_params=pltpu.CompilerParams(
            dimension_semantics=("arbitrary",), vmem_limit_bytes=V7X_VMEM_LIMIT_BYTES),
        name="prompt_layer",
    )(x, mod, mod, lw["g_mix"], lw["g_ffn"], lw["w_in"], lw["w_out"], lw["w_up"], lw["w_down"],
      tabs["cos"], tabs["sin"], tabs["dmask"], tabs["qdec"], tabs["kdec"], tabs["cdec"],
      lw["gn"], lw["ln"], lw["ws_tril"], lw["bs_full"], lw["conv_w"], lw["conv_b"], g_final)


def _sample_in_body(x_ref, mod_ref, gmix_ref, win_ref, cos_ref, sin_ref, ln_ref, ws0_ref, bs0_ref,
                    q_ref, k_ref, v_ref, sg_ref, ogm_ref, vn_ref):
    sh1, sc1 = mod_ref[:, 0:D_MODEL], mod_ref[:, D_MODEL:2 * D_MODEL]
    h = _rms_unit(x_ref[...]) * gmix_ref[...]
    h = (h * (1.0 + sc1) + sh1).astype(BF16)
    p = _dot(h, win_ref[...])
    for hh in range(N_HEADS):
        cs = slice(hh * HEAD_W, (hh + 1) * HEAD_W)
        q_ref[:, cs] = _rope(p[:, hh * HEAD_W:(hh + 1) * HEAD_W], cos_ref[...], sin_ref[...])
        k_ref[:, cs] = _rope(p[:, RET_W + hh * HEAD_W:RET_W + (hh + 1) * HEAD_W],
                             cos_ref[...], sin_ref[...]) * (HEAD_W ** -0.5)
        u = _gelu_tanh(p[:, 4 * RET_W + hh * HEAD_W:4 * RET_W + (hh + 1) * HEAD_W])
        vv = _gelu_tanh(p[:, 4 * RET_W + GM_W + hh * HEAD_W:4 * RET_W + GM_W + (hh + 1) * HEAD_W])
        vn = _center_unit(vv) * ln_ref[:, cs]
        vn_ref[:, cs] = vn
        ogm_ref[:, cs] = u * (ws0_ref[:, cs] * vn + bs0_ref[:, cs])
    v_ref[...] = p[:, 2 * RET_W:3 * RET_W]
    sg_ref[...] = _silu(p[:, 3 * RET_W:4 * RET_W])


def _sample_ret_body(q_ref, k_ref, v_ref, s_hbm, gam_ref, *rest, sb, layer, first, n_steps):
    o_ref, snew_ref, ring_ref, sem_ref = rest[-4:]
    i = pl.program_id(0)

    def fetch(j):
        slot = lax.rem(j, STATE_RING)
        return pltpu.make_async_copy(s_hbm.at[layer, pl.ds(j * sb, sb)], ring_ref.at[slot], sem_ref.at[slot])

    @pl.when(i == 0)
    def _():
        for j in range(min(STATE_RING - 1, n_steps)):
            fetch(j).start()

    @pl.when(i + STATE_RING - 1 < n_steps)
    def _():
        fetch(i + STATE_RING - 1).start()

    fetch(i).wait()
    s_ref = ring_ref.at[lax.rem(i, STATE_RING)]
    if first:
        for other in range(snew_ref.shape[0]):
            if other != layer:
                snew_ref[other] = jnp.zeros(snew_ref.shape[1:], F32)
        snew_ref = snew_ref.at[layer]
    for hh in range(N_HEADS):
        cs = slice(hh * HEAD_W, (hh + 1) * HEAD_W)
        gam = gam_ref[:, cs]
        v = v_ref[:, cs]
        k = k_ref[:, cs]
        qb = q_ref[:, cs].astype(BF16)
        kt = jnp.concatenate([k, jnp.zeros((HEAD_W - sb, HEAD_W), F32)], axis=0).T
        for s in range(sb):
            state = s_ref[s, hh]
            snew_ref[s, hh] = gam * state + kt[:, s:s + 1] * v[s:s + 1, :]
            o_ref[s:s + 1, cs] = gam * _dot(qb, state.astype(BF16))[s:s + 1, :]
        qk = jnp.sum(q_ref[:, cs] * k, axis=-1, keepdims=True)
        o_ref[:, cs] = o_ref[:, cs] + qk * v


def _sample_out_body(x_ref, mod_ref, o_ref, sg_ref, ogm_ref, gn_ref, wout_ref, gffn_ref, wup_ref,
                     cbuf_ref, convw_ref, convb_ref, wdown_ref, gfin_ref, *rest, final_norm, first):
    y_ref, cnew_ref = rest[-2:]
    if first:
        @pl.when(pl.program_id(0) > 0)
        def _():
            cnew_ref[...] = jnp.zeros_like(cnew_ref)

        pl.when(pl.program_id(0) == 0)(functools.partial(
            _sample_out_layer, x_ref, mod_ref, o_ref, sg_ref, ogm_ref, gn_ref, wout_ref, gffn_ref, wup_ref,
            cbuf_ref, convw_ref, convb_ref, wdown_ref, gfin_ref, y_ref, cnew_ref, final_norm))
    else:
        _sample_out_layer(x_ref, mod_ref, o_ref, sg_ref, ogm_ref, gn_ref, wout_ref, gffn_ref, wup_ref,
                          cbuf_ref, convw_ref, convb_ref, wdown_ref, gfin_ref, y_ref, cnew_ref, final_norm)


def _sample_out_layer(x_ref, mod_ref, o_ref, sg_ref, ogm_ref, gn_ref, wout_ref, gffn_ref, wup_ref,
                      cbuf_ref, convw_ref, convb_ref, wdown_ref, gfin_ref, y_ref, cnew_ref, final_norm):
    gt1 = mod_ref[:, 2 * D_MODEL:3 * D_MODEL]
    sh2, sc2 = mod_ref[:, 3 * D_MODEL:4 * D_MODEL], mod_ref[:, 4 * D_MODEL:5 * D_MODEL]
    gt2 = mod_ref[:, 5 * D_MODEL:6 * D_MODEL]
    parts = []
    for hh in range(N_HEADS):
        cs = slice(hh * HEAD_W, (hh + 1) * HEAD_W)
        on = _center_unit(o_ref[:, cs]) * gn_ref[:, cs]
        parts.append((sg_ref[:, cs] * on).astype(BF16))
    parts.append(ogm_ref[...].astype(BF16))
    mix = _dot(jnp.concatenate(parts, axis=1), wout_ref[...])
    x1 = x_ref[...] + gt1 * mix
    h = _rms_unit(x1) * gffn_ref[...]
    h = (h * (1.0 + sc2) + sh2).astype(BF16)
    a = _dot(h, wup_ref[...])
    gw = GATE_GROUP_W
    f_parts = []
    for j in range(D_FF // gw):
        halves = []
        for half in range(2):
            c0 = half * D_FF + j * gw
            a_blk = a[:, (2 * j + half) * gw:(2 * j + half + 1) * gw]
            buf0, buf1 = cbuf_ref[:, 0, c0:c0 + gw], cbuf_ref[:, 1, c0:c0 + gw]
            halves.append(convb_ref[:, c0:c0 + gw] + buf0 * convw_ref[0:1, c0:c0 + gw]
                          + buf1 * convw_ref[1:2, c0:c0 + gw] + a_blk * convw_ref[2:3, c0:c0 + gw])
            cnew_ref[:, 0, c0:c0 + gw] = buf1
            cnew_ref[:, 1, c0:c0 + gw] = a_blk
        f_parts.append((_silu(halves[0]) * halves[1]).astype(BF16))
    f = jnp.concatenate(f_parts, axis=1)
    x2 = x1 + gt2 * _dot(f, wdown_ref[...])
    if final_norm:
        x2 = _rms_unit(x2) * gfin_ref[...]
    y_ref[...] = x2


def _whole(shape, pipeline_mode=None):
    zeros = (0,) * len(shape)
    return pl.BlockSpec(shape, lambda *_: zeros, pipeline_mode=pipeline_mode)


def _layer_slab(shape, layer, pipeline_mode=None):
    zeros = (0,) * (len(shape) - 1)
    return pl.BlockSpec((None,) + tuple(shape[1:]), lambda *_: (layer,) + zeros, pipeline_mode=pipeline_mode)


def _sample_layer(layer, x, mod, state_ret, state_conv, ret_acc, conv_acc, lw, tabs, g_final, final_norm):
    n = x.shape[0]
    params = pltpu.CompilerParams(dimension_semantics=("arbitrary",), vmem_limit_bytes=V7X_VMEM_LIMIT_BYTES)
    act = jax.ShapeDtypeStruct((n, RET_W), F32)
    in_args = (x, mod, lw["g_mix"], lw["w_in"], tabs["cos_s"], tabs["sin_s"], lw["ln"], lw["ws0"], lw["bs0"])
    in_specs = [_whole(a.shape) for a in in_args]
    in_specs[1] = pl.BlockSpec((None, n, N_MOD * D_MODEL), lambda i: (layer, 0, 0))
    for i in (2, 3, 6, 7, 8):
        in_specs[i] = _layer_slab(in_args[i].shape, layer)
    q, k, v, sg, ogm, vn = pl.pallas_call(
        _sample_in_body,
        grid=(1,),
        in_specs=in_specs,
        out_specs=[_whole(act.shape)] * 6,
        out_shape=[act] * 6,
        compiler_params=params,
        name="sample_in",
    )(*in_args)

    sb = SAMPLE_SEQ_BLOCK
    row_spec = pl.BlockSpec((sb, RET_W), lambda i: (i, 0))
    state_spec = pl.BlockSpec((None, sb, N_HEADS, HEAD_W, HEAD_W), lambda i: (layer, i, 0, 0, 0))
    carried = [] if ret_acc is None else [ret_acc]
    depth = state_ret.shape[0]
    first = ret_acc is None
    all_layers_spec = pl.BlockSpec((depth, sb, N_HEADS, HEAD_W, HEAD_W), lambda i: (0, i, 0, 0, 0))
    o, ret_acc = pl.pallas_call(
        functools.partial(_sample_ret_body, sb=sb, layer=layer, first=first, n_steps=n // sb),
        grid=(n // sb,),
        in_specs=[row_spec, row_spec, row_spec, pl.BlockSpec(memory_space=pl.ANY), _whole((1, RET_W))]
        + [pl.BlockSpec(memory_space=pl.ANY)] * len(carried),
        out_specs=[row_spec, all_layers_spec if first else state_spec],
        out_shape=[act, jax.ShapeDtypeStruct(state_ret.shape, F32)],
        scratch_shapes=[pltpu.VMEM((STATE_RING, sb, N_HEADS, HEAD_W, HEAD_W), F32),
                        pltpu.SemaphoreType.DMA((STATE_RING,))],
        input_output_aliases={5: 1} if carried else {},
        compiler_params=params,
        name="sample_ret",
    )(q, k, v, state_ret, tabs["gamma_s"], *carried)

    out_args = (x, mod, o, sg, ogm, lw["gn"], lw["w_out"], lw["g_ffn"], lw["w_up"], state_conv,
                lw["conv_w"], lw["conv_b"], lw["w_down"], g_final)
    once = pl.Buffered(1)
    in_specs = [_whole(a.shape, once) for a in out_args]
    in_specs[1] = pl.BlockSpec((None, n, N_MOD * D_MODEL), lambda i: (layer, 0, 0), pipeline_mode=once)
    for i in (5, 6, 7, 8, 9, 10, 11, 12):
        in_specs[i] = _layer_slab(out_args[i].shape, layer, once)
    first = conv_acc is None
    carried = [] if first else [conv_acc]
    slab_spec = pl.BlockSpec((None,) + tuple(state_conv.shape[1:]),
                             lambda i: (lax.rem(layer + i, depth), 0, 0, 0), pipeline_mode=once)

    y, conv_acc = pl.pallas_call(
        functools.partial(_sample_out_body, final_norm=final_norm, first=first),
        grid=(depth if first else 1,),
        in_specs=in_specs + [pl.BlockSpec(memory_space=pl.ANY)] * len(carried),
        out_specs=[_whole((n, D_MODEL)), slab_spec],
        out_shape=[jax.ShapeDtypeStruct((n, D_MODEL), F32), jax.ShapeDtypeStruct(state_conv.shape, F32)],
        input_output_aliases={len(out_args): 1} if carried else {},
        compiler_params=params,
        name="sample_out",
    )(*out_args, *carried)
    return y, ret_acc, conv_acc, vn


def _rope_tables(pos):
    half = HEAD_W // 2
    freqs = np.exp(-math.log(ROPE_BASE) * np.arange(half, dtype=np.float64) / half)
    ang = np.asarray(pos, dtype=np.float64)[:, None] * freqs[None, :]
    cos, sin = np.cos(ang), np.sin(ang)
    return (np.concatenate([cos, cos], axis=-1).astype(np.float32),
            np.concatenate([-sin, sin], axis=-1).astype(np.float32))


def _decay_tables(chunk):
    lg = np.log1p(-np.exp2(-5.0 - np.arange(N_HEADS, dtype=np.float64)))
    i = np.arange(chunk, dtype=np.float64)
    diff = i[:, None] - i[None, :]
    dmask = np.where(diff[None] >= 0.0, np.exp(np.maximum(diff, 0.0)[None] * lg[:, None, None]), 0.0)
    q_dec = np.exp((i[:, None] + 1.0) * lg[None, :])
    k_dec = np.exp((chunk - 1.0 - i)[:, None] * lg[None, :])
    chunk_dec = np.exp(chunk * lg)
    return tuple(a.astype(np.float32) for a in (dmask, q_dec, k_dec, chunk_dec))


def _per_head_lanes(a):
    return a.repeat(HEAD_W, axis=-1)


def kernel(x_prompt, x_sample, state_ret, state_conv, c_prompt, c_sample, w_ada, b_ada, g_mix, w_in,
           ret_gn_gain, gmlp_ln_gain, w_s, b_s, w_out, g_ffn, w_up, conv_w, conv_b, w_down, g_final):
    depth = w_in.shape[0]
    batch, seq, _ = x_prompt.shape
    n_dec, dec_seq, _ = x_sample.shape
    assert dec_seq == 1 and seq % PROMPT_TILE == 0 and PROMPT_TILE % CHUNK == 0

    cos_p, sin_p = _rope_tables(np.arange(seq))
    cos_s, sin_s = _rope_tables(PAST_LEN + np.arange(dec_seq))
    dmask, q_dec, k_dec, chunk_dec = _decay_tables(CHUNK)
    _, _, _, gamma_s = _decay_tables(dec_seq)
    tabs = {
        "cos": cos_p, "sin": sin_p, "cos_s": cos_s, "sin_s": sin_s,
        "dmask": dmask, "qdec": _per_head_lanes(q_dec), "kdec": _per_head_lanes(k_dec),
        "cdec": np.broadcast_to(chunk_dec[:, None, None], (N_HEADS, 1, HEAD_W)),
        "gamma_s": _per_head_lanes(gamma_s[None, :]),
    }
    tabs = {name: jnp.asarray(a) for name, a in tabs.items()}

    mod = _adaln(jnp.concatenate([c_sample, c_prompt], axis=0), w_ada, b_ada)
    mod_p = mod[:, n_dec:].reshape(depth, batch, N_MOD, D_MODEL)
    tril = np.tril(np.ones((CHUNK, CHUNK), dtype=bool))
    lw = {
        "g_mix": g_mix.reshape(depth, 1, D_MODEL), "g_ffn": g_ffn.reshape(depth, 1, D_MODEL),
        "w_in": _to_bf16(w_in), "w_out": _to_bf16(w_out),
        "w_up": _to_bf16(w_up, group_gate_columns=True), "w_down": _to_bf16(w_down),
        "gn": ret_gn_gain.reshape(depth, 1, RET_W), "ln": gmlp_ln_gain.reshape(depth, 1, GM_W),
        "ws_tril": jnp.where(tril, w_s, 0.0).astype(BF16),
        "bs_full": _per_head_lanes(b_s.transpose(0, 2, 1)),
        "ws0": _per_head_lanes(w_s[:, None, :, 0, 0]), "bs0": _per_head_lanes(b_s[:, None, :, 0]),
        "conv_w": conv_w, "conv_b": conv_b.reshape(depth, 1, 2 * D_FF),
    }
    g_fin = g_final.reshape(1, D_MODEL)

    xp, xs = x_prompt, x_sample.reshape(n_dec, D_MODEL)
    ret_p, conv_p, v_s = [], [], []
    ret_s = conv_s = None
    for l in range(depth):
        last = l == depth - 1
        xp, sp, bp = _prompt_layer(l, xp, mod_p, lw, tabs, g_fin, last)
        xs, ret_s, conv_s, vs = _sample_layer(l, xs, mod, state_ret, state_conv, ret_s, conv_s, lw, tabs,
                                              g_fin, last)
        ret_p.append(sp); conv_p.append(bp)
        v_s.append(vs.reshape(n_dec, dec_seq, GM_W))
    return (xp, xs.reshape(n_dec, dec_seq, D_MODEL), jnp.stack(ret_p), jnp.stack(conv_p),
            ret_s, conv_s, jnp.stack(v_s))
```
